```python
import jax, jax.numpy as jnp
from jax import lax
import numpy as np

D_MODEL = 4096
BATCH = 1
SEQ = 8192
DEPTH = 1

HEAD_DIM = 128
N_HEADS = D_MODEL // HEAD_DIM
D_MIX = N_HEADS * HEAD_DIM
SB_HEADS = N_HEADS // 2
NSA_HEADS = N_HEADS - SB_HEADS
NSA_KV = 2
NSA_HPG = NSA_HEADS // NSA_KV
L_CMP = 32
STRIDE_CMP = 16
CMP_HIDDEN = 256
L_SLC = 64
N_TOPK = 16
WINDOW = 512
Q_BLOCK = 128
D_FF = -(-8 * D_MODEL // (3 * 256)) * 256
EPS = 1e-6
NEG_INF = -1e30
FORCE_BONUS = 1e6

SB_COLS = 3 * SB_HEADS * HEAD_DIM
NSA_Q_COLS = NSA_HEADS * HEAD_DIM
NSA_KV_COLS = 3 * 2 * NSA_KV * HEAD_DIM
NSA_GATE_COLS = 3 * NSA_HEADS
D_IN = SB_COLS + NSA_Q_COLS + NSA_KV_COLS + NSA_GATE_COLS

kernel_name = 'hymba_stickbreaking_nsa_block'


def rmsnorm(x, g):
    xf = x.astype(jnp.float32)
    xf = xf * lax.rsqrt(jnp.mean(xf * xf, axis=-1, keepdims=True) + EPS)
    return xf.astype(x.dtype) * g


def masked_softmax(s, mask):
    p = jax.nn.softmax(jnp.where(mask, s, NEG_INF), axis=-1)
    return p * mask


def to_heads(t, n):
    b, s, _ = t.shape
    return t.reshape(b, s, n, HEAD_DIM).transpose(0, 2, 1, 3)


def stick_breaking_attention(q, k, v):
    B, H, S, D = q.shape
    nqb = S // Q_BLOCK
    scale = D ** -0.5
    qb = q.reshape(B, H, nqb, Q_BLOCK, D).transpose(2, 0, 1, 3, 4)
    key_pos = jnp.arange(S)

    def block(args):
        i, q_blk = args
        t = i * Q_BLOCK + jnp.arange(Q_BLOCK)
        z = jnp.einsum('bhqd,bhkd->bhqk', q_blk, k).astype(jnp.float32) * scale
        mask = key_pos[None, :] < t[:, None]
        log_not = jnp.where(mask, jax.nn.log_sigmoid(-z), 0.0)
        tail = lax.cumsum(log_not, axis=3, reverse=True) - log_not
        a = jnp.where(mask, jnp.exp(jax.nn.log_sigmoid(z) + tail), 0.0)
        return jnp.einsum('bhqk,bhkd->bhqd', a.astype(v.dtype), v)

    out = lax.map(block, (jnp.arange(nqb), qb))
    return out.transpose(1, 2, 0, 3, 4).reshape(B, H, S, D)


def compress_blocks(kv, pos_emb, w1, w2):
    S = kv.shape[2]
    n_cmp = (S - L_CMP) // STRIDE_CMP + 1
    idx = jnp.arange(n_cmp)[:, None] * STRIDE_CMP + jnp.arange(L_CMP)[None, :]
    blocks = kv[:, :, idx] + pos_emb
    flat = blocks.reshape(blocks.shape[0], blocks.shape[1], n_cmp, L_CMP * HEAD_DIM)
    return jax.nn.gelu(flat @ w1) @ w2


def cmp_to_slc_matrix(n_cmp, n_slc):
    rs = L_SLC // STRIDE_CMP
    rc = L_CMP // STRIDE_CMP
    j = jnp.arange(n_slc)[:, None, None]
    m = jnp.arange(rs)[None, :, None]
    n = jnp.arange(rc)[None, None, :]
    src = (rs * j - m - n).reshape(n_slc, rs * rc)
    hit = src[None, :, :] == jnp.arange(n_cmp)[:, None, None]
    return jnp.sum(hit, axis=-1).astype(jnp.float32)


def gather_blocks(kb, idx):
    return jax.vmap(jax.vmap(lambda a, i: a[i]))(kb, idx)


def native_sparse_attention(q, k_cmp, v_cmp, k_slc, v_slc, k_win, v_win, gates, slopes):
    B, G, HPG, S, D = q.shape
    nqb = S // Q_BLOCK
    n_cmp = k_cmp.shape[2]
    n_slc = S // L_SLC
    n_top = min(N_TOPK, n_slc)
    scale = D ** -0.5
    sl = slopes[None, :, :, None, None]
    cmp_end = jnp.arange(n_cmp) * STRIDE_CMP + L_CMP - 1
    m_sel = cmp_to_slc_matrix(n_cmp, n_slc)
    k_slc_b = k_slc.reshape(B, G, n_slc, L_SLC, D)
    v_slc_b = v_slc.reshape(B, G, n_slc, L_SLC, D)
    pad = ((0, 0), (0, 0), (WINDOW, 0), (0, 0))
    k_win_p = jnp.pad(k_win, pad)
    v_win_p = jnp.pad(v_win, pad)
    blk = jnp.arange(n_slc)
    qb = q.reshape(B, G, HPG, nqb, Q_BLOCK, D).transpose(3, 0, 1, 2, 4, 5)
    gb = gates.reshape(B, G, HPG, nqb, Q_BLOCK, 3).transpose(3, 0, 1, 2, 4, 5)

    def block(args):
        i, q_blk, g_blk = args
        t = i * Q_BLOCK + jnp.arange(Q_BLOCK)
        dist_c = (t[:, None] - cmp_end[None, :]).astype(jnp.float32)
        s_c = jnp.einsum('bghqd,bgnd->bghqn', q_blk, k_cmp).astype(jnp.float32) * scale - sl * dist_c
        p_c = masked_softmax(s_c, cmp_end[None, :] <= t[:, None])
        o_c = jnp.einsum('bghqn,bgnd->bghqd', p_c.astype(v_cmp.dtype), v_cmp)
        imp = jnp.einsum('bghqn,nj->bgqj', p_c, m_sel)
        cur = t // L_SLC
        valid = blk[None, :] <= cur[:, None]
        forced = (blk[None, :] == 0) | (blk[None, :] == cur[:, None]) | (blk[None, :] == cur[:, None] - 1)
        score = jnp.where(valid, imp + jnp.where(forced, FORCE_BONUS, 0.0), NEG_INF)
        _, idx = lax.top_k(score, n_top)
        k_sel = gather_blocks(k_slc_b, idx).reshape(B, G, Q_BLOCK, n_top * L_SLC, D)
        v_sel = gather_blocks(v_slc_b, idx).reshape(B, G, Q_BLOCK, n_top * L_SLC, D)
        pos_sel = (idx[..., None] * L_SLC + jnp.arange(L_SLC)).reshape(B, G, Q_BLOCK, n_top * L_SLC)
        diff_s = (t[None, None, :, None] - pos_sel)[:, :, None]
        s_s = jnp.einsum('bghqd,bgqnd->bghqn', q_blk, k_sel).astype(jnp.float32) * scale - sl * diff_s.astype(jnp.float32)
        p_s = masked_softmax(s_s, diff_s >= 0)
        o_s = jnp.einsum('bghqn,bgqnd->bghqd', p_s.astype(v_sel.dtype), v_sel)
        k_w = lax.dynamic_slice_in_dim(k_win_p, i * Q_BLOCK, WINDOW + Q_BLOCK, axis=2)
        v_w = lax.dynamic_slice_in_dim(v_win_p, i * Q_BLOCK, WINDOW + Q_BLOCK, axis=2)
        pos_w = i * Q_BLOCK - WINDOW + jnp.arange(WINDOW + Q_BLOCK)
        diff_w = t[:, None] - pos_w[None, :]
        mask_w = (diff_w >= 0) & (diff_w < WINDOW) & (pos_w[None, :] >= 0)
        s_w = jnp.einsum('bghqd,bgkd->bghqk', q_blk, k_w).astype(jnp.float32) * scale - sl * diff_w.astype(jnp.float32)
        p_w = masked_softmax(s_w, mask_w)
        o_w = jnp.einsum('bghqk,bgkd->bghqd', p_w.astype(v_w.dtype), v_w)
        return g_blk[..., 0:1] * o_c + g_blk[..., 1:2] * o_s + g_blk[..., 2:3] * o_w

    out = lax.map(block, (jnp.arange(nqb), qb, gb))
    return out.transpose(1, 2, 3, 0, 4, 5).reshape(B, G, HPG, S, D)


def alibi_slopes():
    h = jnp.arange(1, NSA_HEADS + 1, dtype=jnp.float32)
    return (2.0 ** (-8.0 * h / NSA_HEADS)).reshape(NSA_KV, NSA_HPG)


def setup_inputs(seed: int = 0) -> dict:
    key = jax.random.key(seed)
    ks = jax.random.split(key, 20)
    f32 = jnp.float32
    nrm = lambda k, shape, s: jax.random.normal(k, shape, f32) * s
    gain = lambda k, n: 1.0 + 0.02 * jax.random.normal(k, (DEPTH, n), f32)
    return {
        'x': jax.random.normal(ks[0], (BATCH, SEQ, D_MODEL), f32),
        'attn_norm': gain(ks[1], D_MODEL),
        'w_in': nrm(ks[2], (DEPTH, D_MODEL, D_IN), D_MODEL ** -0.5),
        'pos_cmp_k': nrm(ks[3], (DEPTH, L_CMP, HEAD_DIM), 0.1),
        'pos_cmp_v': nrm(ks[4], (DEPTH, L_CMP, HEAD_DIM), 0.1),
        'w_cmp_k1': nrm(ks[5], (DEPTH, L_CMP * HEAD_DIM, CMP_HIDDEN), (L_CMP * HEAD_DIM) ** -0.5),
        'w_cmp_k2': nrm(ks[6], (DEPTH, CMP_HIDDEN, HEAD_DIM), CMP_HIDDEN ** -0.5),
        'w_cmp_v1': nrm(ks[7], (DEPTH, L_CMP * HEAD_DIM, CMP_HIDDEN), (L_CMP * HEAD_DIM) ** -0.5),
        'w_cmp_v2': nrm(ks[8], (DEPTH, CMP_HIDDEN, HEAD_DIM), CMP_HIDDEN ** -0.5),
        'norm_sb': gain(ks[9], SB_HEADS * HEAD_DIM),
        'norm_nsa': gain(ks[10], NSA_HEADS * HEAD_DIM),
        'w_out': nrm(ks[11], (DEPTH, D_MIX, D_MODEL), D_MIX ** -0.5),
        'ffn_norm': gain(ks[12], D_MODEL),
        'w_gate': nrm(ks[13], (DEPTH, D_MODEL, D_FF), D_MODEL ** -0.5),
        'w_up': nrm(ks[14], (DEPTH, D_MODEL, D_FF), D_MODEL ** -0.5),
        'w_down': nrm(ks[15], (DEPTH, D_FF, D_MODEL), D_FF ** -0.5),
        'final_norm': 1.0 + 0.02 * jax.random.normal(ks[16], (D_MODEL,), f32),
    }


def reference(x, attn_norm, w_in, pos_cmp_k, pos_cmp_v, w_cmp_k1, w_cmp_k2, w_cmp_v1, w_cmp_v2,
              norm_sb, norm_nsa, w_out, ffn_norm, w_gate, w_up, w_down, final_norm):
    B, S, _ = x.shape
    slopes = alibi_slopes()
    splits = np.cumsum([SB_COLS // 3] * 3 + [NSA_Q_COLS] + [NSA_KV * HEAD_DIM] * 6).tolist()
    for l in range(DEPTH):
        h = rmsnorm(x, attn_norm[l])
        proj = h @ w_in[l]
        (q_sb, k_sb, v_sb, q_n, kc, vc, ksl, vsl, kw, vw, g_n) = jnp.split(proj, splits, axis=-1)
        o_sb = stick_breaking_attention(to_heads(q_sb, SB_HEADS), to_heads(k_sb, SB_HEADS), to_heads(v_sb, SB_HEADS))
        o_sb = o_sb.transpose(0, 2, 1, 3).reshape(B, S, SB_HEADS * HEAD_DIM)
        q_nsa = to_heads(q_n, NSA_HEADS).reshape(B, NSA_KV, NSA_HPG, S, HEAD_DIM)
        k_cmp = compress_blocks(to_heads(kc, NSA_KV), pos_cmp_k[l], w_cmp_k1[l], w_cmp_k2[l])
        v_cmp = compress_blocks(to_heads(vc, NSA_KV), pos_cmp_v[l], w_cmp_v1[l], w_cmp_v2[l])
        gates = jax.nn.sigmoid(g_n).reshape(B, S, NSA_KV, NSA_HPG, 3).transpose(0, 2, 3, 1, 4)
        o_nsa = native_sparse_attention(q_nsa, k_cmp, v_cmp, to_heads(ksl, NSA_KV), to_heads(vsl, NSA_KV),
                                        to_heads(kw, NSA_KV), to_heads(vw, NSA_KV), gates, slopes)
        o_nsa = o_nsa.transpose(0, 3, 1, 2, 4).reshape(B, S, NSA_HEADS * HEAD_DIM)
        mixed = jnp.concatenate([rmsnorm(o_sb, norm_sb[l]), rmsnorm(o_nsa, norm_nsa[l])], axis=-1)
        x = x + mixed @ w_out[l]
        h = rmsnorm(x, ffn_norm[l])
        x = x + (jax.nn.silu(h @ w_gate[l]) * (h @ w_up[l])) @ w_down[l]
    return rmsnorm(x, final_norm)
```

```python
import functools

import numpy as np
import jax
import jax.numpy as jnp
from jax import lax
from jax.experimental import pallas as pl
from jax.experimental.pallas import tpu as pltpu

HEAD_DIM = 128
SB_HEADS = 16
NSA_HEADS = 16
NSA_KV = 2
NSA_HPG = NSA_HEADS // NSA_KV
L_CMP = 32
STRIDE_CMP = 16
CMP_HIDDEN = 256
L_SLC = 64
LOG2_L_SLC = 6
N_TOPK = 16
WINDOW = 512
EPS = 1e-6
NEG_INF = -1e30
FORCE_BONUS = 1e6

LANES = 128
SUBLANES = 8
TQ = 128
TK = 128
VMEM_CAP = 56 * 1024 * 1024
SB_DEAD_LOG = -105.0

F32 = jnp.float32
BF16 = jnp.bfloat16

SB_W = SB_HEADS * HEAD_DIM
NSA_W = NSA_HEADS * HEAD_DIM
GROUP_W = NSA_HPG * HEAD_DIM
MAIN_COLS = 3 * SB_W + NSA_W + 3 * 2 * NSA_KV * HEAD_DIM
GATE_COLS = 3 * NSA_HEADS
CB_KC, CB_VC, CB_KS, CB_VS, CB_KW, CB_VW = 64, 66, 68, 70, 72, 74


def _cparams(sem, vmem_bytes):
    return pltpu.CompilerParams(dimension_semantics=sem, vmem_limit_bytes=int(min(vmem_bytes, VMEM_CAP)))


def _dot(a, b):
    return jnp.dot(a, b, preferred_element_type=F32)


def _dot_nt(a, b):
    return lax.dot_general(a, b, (((1,), (1,)), ((), ())), preferred_element_type=F32)


def _split_bf16(x):
    hi = x.astype(BF16)
    lo = (x - hi.astype(F32)).astype(BF16)
    return hi, lo


def _rms_kernel(x_ref, g_ref, o_ref):
    x = x_ref[...]
    ms = jnp.mean(x * x, axis=-1, keepdims=True)
    o_ref[...] = (x * lax.rsqrt(ms + EPS) * g_ref[...]).astype(o_ref.dtype)


def _rmsnorm(x, g, out_dtype, tm=256):
    m, d = x.shape
    return pl.pallas_call(
        _rms_kernel,
        grid=(m // tm,),
        in_specs=[pl.BlockSpec((tm, d), lambda i: (i, 0)), pl.BlockSpec((1, d), lambda i: (0, 0))],
        out_specs=pl.BlockSpec((tm, d), lambda i: (i, 0)),
        out_shape=jax.ShapeDtypeStruct((m, d), out_dtype),
        compiler_params=_cparams(("parallel",), 6 * tm * d * 4),
        name="rmsnorm",
    )(x, g.reshape(1, d))


def _rms2_kernel(a_ref, b_ref, ga_ref, gb_ref, o_ref):
    wa = a_ref.shape[1]
    for src, g_ref, off in ((a_ref, ga_ref, 0), (b_ref, gb_ref, wa)):
        x = src[...]
        ms = jnp.mean(x * x, axis=-1, keepdims=True)
        o_ref[:, off:off + x.shape[1]] = (x * lax.rsqrt(ms + EPS) * g_ref[...]).astype(o_ref.dtype)


def _rmsnorm_pair(a, b, ga, gb, tm=256):
    m, wa = a.shape
    wb = b.shape[1]
    return pl.pallas_call(
        _rms2_kernel,
        grid=(m // tm,),
        in_specs=[pl.BlockSpec((tm, wa), lambda i: (i, 0)), pl.BlockSpec((tm, wb), lambda i: (i, 0)),
                  pl.BlockSpec((1, wa), lambda i: (0, 0)), pl.BlockSpec((1, wb), lambda i: (0, 0))],
        out_specs=pl.BlockSpec((tm, wa + wb), lambda i: (i, 0)),
        out_shape=jax.ShapeDtypeStruct((m, wa + wb), BF16),
        compiler_params=_cparams(("parallel",), 6 * tm * (wa + wb) * 4),
        name="rmsnorm_pair",
    )(a, b, ga.reshape(1, wa), gb.reshape(1, wb))


def _mm_scale_kernel(a_ref, b_ref, cs_ref, o_ref):
    o_ref[...] = (_dot(a_ref[...], b_ref[...]) * cs_ref[...]).astype(o_ref.dtype)


def _mm_sigmoid_kernel(a_ref, b_ref, o_ref):
    o_ref[...] = jax.nn.sigmoid(_dot(a_ref[...], b_ref[...]))


def _mm_res_kernel(a_ref, b_ref, r_ref, o_ref):
    o_ref[...] = r_ref[...] + _dot(a_ref[...], b_ref[...])


def _mm_swiglu_kernel(a_ref, bg_ref, bu_ref, o_ref):
    a = a_ref[...]
    gate = _dot(a, bg_ref[...])
    up = _dot(a, bu_ref[...])
    o_ref[...] = (jax.nn.silu(gate) * up).astype(o_ref.dtype)


def _mm_acc_res_kernel(a_ref, b_ref, r_ref, o_ref, acc_ref):
    k = pl.program_id(2)

    @pl.when(k == 0)
    def _():
        acc_ref[...] = r_ref[...]

    acc_ref[...] += _dot(a_ref[...], b_ref[...])

    @pl.when(k == pl.num_programs(2) - 1)
    def _():
        o_ref[...] = acc_ref[...]


def _matmul_fullk(kernel_fn, a, bs, extras, out_dtype, tm, tn, name):
    m, k = a.shape
    n = bs[0].shape[1]
    in_specs = [pl.BlockSpec((tm, k), lambda i, j: (i, 0))]
    in_specs += [pl.BlockSpec((k, tn), lambda i, j: (0, j)) for _ in bs]
    vmem = 2 * tm * k * 2 + 2 * len(bs) * k * tn * 2 + 2 * tm * tn * 4 + (2 + len(bs)) * tm * tn * 4
    args = [a, *bs]
    for arr, kind in extras:
        if kind == "col":
            in_specs.append(pl.BlockSpec((1, tn), lambda i, j: (0, j)))
        else:
            in_specs.append(pl.BlockSpec((tm, tn), lambda i, j: (i, j)))
            vmem += 2 * tm * tn * 4
        args.append(arr)
    return pl.pallas_call(
        kernel_fn,
        grid=(m // tm, n // tn),
        in_specs=in_specs,
        out_specs=pl.BlockSpec((tm, tn), lambda i, j: (i, j)),
        out_shape=jax.ShapeDtypeStruct((m, n), out_dtype),
        compiler_params=_cparams(("parallel", "parallel"), vmem + (4 << 20)),
        name=name,
    )(*args)


def _matmul_acc_res(a, b, res, tm, tn, tk):
    m, k = a.shape
    n = b.shape[1]
    vmem = 2 * tm * tk * 2 + 2 * tk * tn * 2 + 5 * tm * tn * 4 + tm * tn * 4
    return pl.pallas_call(
        _mm_acc_res_kernel,
        grid=(m // tm, n // tn, k // tk),
        in_specs=[pl.BlockSpec((tm, tk), lambda i, j, kk: (i, kk)),
                  pl.BlockSpec((tk, tn), lambda i, j, kk: (kk, j)),
                  pl.BlockSpec((tm, tn), lambda i, j, kk: (i, j))],
        out_specs=pl.BlockSpec((tm, tn), lambda i, j, kk: (i, j)),
        out_shape=jax.ShapeDtypeStruct((m, n), F32),
        scratch_shapes=[pltpu.VMEM((tm, tn), F32)],
        compiler_params=_cparams(("parallel", "parallel", "arbitrary"), vmem + (4 << 20)),
        name="matmul_acc_res",
    )(a, b, res)


def _sb_kernel(q_ref, k_ref, v_ref, u_ref, o_ref, carry_ref, acc_ref, *, hg):
    i = pl.program_id(1)
    row = lax.broadcasted_iota(jnp.int32, (TQ, TK), 0)
    col = lax.broadcasted_iota(jnp.int32, (TQ, TK), 1)
    causal = col < row
    u = u_ref[...]

    def tile(kb, diag):
        row0 = pl.multiple_of(kb * TK, TK)
        worst = None
        for h in range(hg):
            cs = slice(h * HEAD_DIM, (h + 1) * HEAD_DIM)
            z = _dot_nt(q_ref[:, cs], k_ref[pl.ds(row0, TK), cs])
            log1p_e = jnp.log(1.0 + jnp.exp(-jnp.abs(z)))
            log_sig = jnp.minimum(z, 0.0) - log1p_e
            log_not = log_sig - z
            if diag:
                log_not = jnp.where(causal, log_not, 0.0)
            hi, lo = _split_bf16(log_not)
            sums = _dot(hi, u) + _dot(lo, u)
            tail = sums[:, :TK]
            total = sums[:, TK:]
            if diag:
                a = jnp.where(causal, jnp.exp(log_sig + tail), 0.0)
                new_carry = total
            else:
                carry = carry_ref[h]
                a = jnp.exp(log_sig + tail + carry)
                new_carry = carry + total
            pv = _dot(a.astype(BF16), v_ref[pl.ds(row0, TK), cs])
            if diag:
                acc_ref[h] = pv
            else:
                acc_ref[h] += pv
            carry_ref[h] = new_carry
            worst = new_carry if worst is None else jnp.maximum(worst, new_carry)
        return jnp.max(worst)

    worst0 = tile(i, True)

    def cond(c):
        kb, worst = c
        return jnp.logical_and(kb >= 0, worst > SB_DEAD_LOG)

    def body(c):
        kb, _ = c
        return kb - 1, tile(kb, False)

    lax.while_loop(cond, body, (i - 1, worst0))
    for h in range(hg):
        o_ref[:, h * HEAD_DIM:(h + 1) * HEAD_DIM] = acc_ref[h]


def _sb_attention(proj, hg=2):
    s = proj.shape[0]
    w = hg * HEAD_DIM
    nblk = SB_W // w
    tri = np.arange(TK)[:, None] > np.arange(TK)[None, :]
    u = jnp.asarray(np.concatenate([tri, np.ones((TK, TK), bool)], axis=1), dtype=BF16)
    vmem = 2 * 2 * s * w * 2 + 2 * TQ * w * 2 + 2 * TQ * w * 4 + 2 * hg * TQ * TK * 4 + (8 << 20)
    return pl.pallas_call(
        functools.partial(_sb_kernel, hg=hg),
        grid=(nblk, s // TQ),
        in_specs=[pl.BlockSpec((TQ, w), lambda g, i: (i, g)),
                  pl.BlockSpec((s, w), lambda g, i: (0, nblk + g)),
                  pl.BlockSpec((s, w), lambda g, i: (0, 2 * nblk + g)),
                  pl.BlockSpec((TK, 2 * TK), lambda g, i: (0, 0))],
        out_specs=pl.BlockSpec((TQ, w), lambda g, i: (i, g)),
        out_shape=jax.ShapeDtypeStruct((s, SB_W), F32),
        scratch_shapes=[pltpu.VMEM((hg, TQ, TK), F32), pltpu.VMEM((hg, TQ, HEAD_DIM), F32)],
        compiler_params=_cparams(("parallel", "arbitrary"), vmem),
        name="sb_attention",
    )(proj, proj, proj, u)


def _compress_kernel(r_ref, pos_ref, w1_ref, w2_ref, o_ref):
    r = r_ref[0]
    half = r.shape[1]
    first = _dot(r, w1_ref[:half, :])
    second = _dot(r, w1_ref[half:, :])
    second = pltpu.roll(second, second.shape[0] - 1, 0)
    pos_term = _dot(pos_ref[...].astype(BF16), w1_ref[...])[0:1, :]
    hidden = jax.nn.gelu(first + second + pos_term)
    o_ref[0] = _dot(hidden.astype(BF16), w2_ref[...]).astype(o_ref.dtype)


def _compress(r, pos, w1, w2):
    g, n, half = r.shape
    pos_rows = jnp.zeros((SUBLANES, 2 * half), F32).at[0].set(pos.reshape(-1))
    return pl.pallas_call(
        _compress_kernel,
        grid=(g,),
        in_specs=[pl.BlockSpec((1, n, half), lambda i: (i, 0, 0)),
                  pl.BlockSpec((SUBLANES, 2 * half), lambda i: (0, 0)),
                  pl.BlockSpec((2 * half, CMP_HIDDEN), lambda i: (0, 0)),
                  pl.BlockSpec((CMP_HIDDEN, HEAD_DIM), lambda i: (0, 0))],
        out_specs=pl.BlockSpec((1, n, HEAD_DIM), lambda i: (i, 0, 0)),
        out_shape=jax.ShapeDtypeStruct((g, n, HEAD_DIM), BF16),
        compiler_params=_cparams(("parallel",), 24 << 20),
        name="nsa_compress",
    )(r, pos_rows, w1.astype(BF16), w2.astype(BF16))


def _nsa_cmp_kernel(slopes_ref, q_ref, kc_ref, vc_ref, msel_ref, oc_ref, sel_ref, flag_ref, *, n_top):
    g = pl.program_id(0)
    i = pl.program_id(1)
    kc = kc_ref[0]
    vc = vc_ref[0]
    n_cmp = kc.shape[0]
    n_slc = msel_ref.shape[1]
    t = i * TQ + lax.broadcasted_iota(jnp.int32, (TQ, n_cmp), 0)
    cmp_end = lax.broadcasted_iota(jnp.int32, (TQ, n_cmp), 1) * STRIDE_CMP + (L_CMP - 1)
    dist = (t - cmp_end).astype(F32)
    bias = jnp.where(dist >= 0.0, 0.0, NEG_INF)
    t_col = i * TQ + lax.broadcasted_iota(jnp.int32, (TQ, 1), 0)
    row_valid = jnp.where(t_col >= L_CMP - 1, 1.0, 0.0)
    p_sum = jnp.zeros((TQ, n_cmp), F32)
    for h in range(NSA_HPG):
        cs = slice(h * HEAD_DIM, (h + 1) * HEAD_DIM)
        slope = slopes_ref[g * NSA_HPG + h]
        s = _dot_nt(q_ref[:, cs], kc) - slope * dist + bias
        m = jnp.max(s, axis=-1, keepdims=True)
        e = jnp.exp(s - m)
        p = e * (row_valid / jnp.sum(e, axis=-1, keepdims=True))
        oc_ref[:, cs] = _dot(p.astype(BF16), vc)
        p_sum = p_sum + p
    hi, lo = _split_bf16(p_sum)
    msel = msel_ref[...]
    imp = _dot(hi, msel) + _dot(lo, msel)
    score_src = imp.T
    blk = lax.broadcasted_iota(jnp.int32, (n_slc, TQ), 0)
    cur = lax.shift_right_logical(i * TQ + lax.broadcasted_iota(jnp.int32, (n_slc, TQ), 1), LOG2_L_SLC)
    valid = blk <= cur
    bonus = jnp.where(blk == 0, FORCE_BONUS, 0.0)
    bonus = jnp.where(blk == cur, FORCE_BONUS, bonus)
    bonus = jnp.where(blk == cur - 1, FORCE_BONUS, bonus)
    score = jnp.where(valid, score_src + bonus, NEG_INF)
    nb = n_slc // SUBLANES
    parts = [score[b * SUBLANES:(b + 1) * SUBLANES, :] for b in range(nb)]
    beaten = [jnp.zeros((SUBLANES, TQ), F32) for _ in range(nb)]
    sub = lax.broadcasted_iota(jnp.int32, (SUBLANES, TQ), 0)
    for jp in range(n_slc):
        rival = jnp.broadcast_to(score[jp:jp + 1, :], (SUBLANES, TQ))
        bj, sj = divmod(jp, SUBLANES)
        for b in range(nb):
            strict = jnp.where(rival > parts[b], 1.0, 0.0)
            ties_too = jnp.where(rival >= parts[b], 1.0, 0.0)
            if b < bj:
                wins = strict
            elif b > bj:
                wins = ties_too
            else:
                wins = jnp.where(sub > sj, ties_too, strict)
            beaten[b] = beaten[b] + wins
    rank = jnp.concatenate(beaten, axis=0)
    chosen = jnp.where(valid, jnp.where(rank < n_top, 1.0, 0.0), 0.0)
    chosen_q = chosen.T
    sel_ref[0, 0] = chosen_q.astype(sel_ref.dtype)
    flag_ref[0, 0] = jnp.max(chosen_q, axis=0, keepdims=True)


def _select_matrix(n_cmp_pad, n_slc):
    rs = L_SLC // STRIDE_CMP
    rc = L_CMP // STRIDE_CMP
    m = np.zeros((n_cmp_pad, n_slc), np.float32)
    for j in range(n_slc):
        for a in range(rs):
            for b in range(rc):
                src = rs * j - a - b
                if 0 <= src < n_cmp_pad - 1:
                    m[src, j] += 1.0
    return jnp.asarray(m, dtype=BF16)


def _nsa_cmp(proj, k_cmp, v_cmp, slopes):
    s = proj.shape[0]
    n_cmp = k_cmp.shape[1]
    n_slc = s // L_SLC
    nq = s // TQ
    msel = _select_matrix(n_cmp, n_slc)
    qblk = (3 * SB_W) // GROUP_W
    return pl.pallas_call(
        functools.partial(_nsa_cmp_kernel, n_top=min(N_TOPK, n_slc)),
        grid=(NSA_KV, nq),
        in_specs=[pl.BlockSpec(memory_space=pltpu.SMEM),
                  pl.BlockSpec((TQ, GROUP_W), lambda g, i: (i, qblk + g)),
                  pl.BlockSpec((1, n_cmp, HEAD_DIM), lambda g, i: (g, 0, 0)),
                  pl.BlockSpec((1, n_cmp, HEAD_DIM), lambda g, i: (g, 0, 0)),
                  pl.BlockSpec((n_cmp, n_slc), lambda g, i: (0, 0))],
        out_specs=[pl.BlockSpec((TQ, GROUP_W), lambda g, i: (i, g)),
                   pl.BlockSpec((1, 1, TQ, n_slc), lambda g, i: (g, i, 0, 0)),
                   pl.BlockSpec((1, 1, 1, n_slc), lambda g, i: (g, i, 0, 0))],
        out_shape=[jax.ShapeDtypeStruct((s, NSA_W), F32),
                   jax.ShapeDtypeStruct((NSA_KV, nq, TQ, n_slc), BF16),
                   jax.ShapeDtypeStruct((NSA_KV, nq, 1, n_slc), F32)],
        compiler_params=_cparams(("parallel", "parallel"), 32 << 20),
        name="nsa_cmp_select",
    )(slopes, proj, k_cmp, v_cmp, msel)


def _nsa_main_kernel(flags_ref, slopes_ref, q_ref, ks_ref, vs_ref, kw_ref, vw_ref, sel_ref, oc_ref, gate_ref,
                     o_ref, acc_s, m_s, acc_w, m_w):
    g = pl.program_id(0)
    i = pl.program_id(1)
    nq = pl.num_programs(1)
    row = lax.broadcasted_iota(jnp.int32, (TQ, TK), 0)
    col = lax.broadcasted_iota(jnp.int32, (TQ, TK), 1)
    rel = (row - col).astype(F32)
    ones_cols = jnp.ones((TK, HEAD_DIM), BF16)
    slopes = [slopes_ref[g * NSA_HPG + h] for h in range(NSA_HPG)]

    def attend(k_ref, v_ref, kb, bias, acc, m_ref, first):
        row0 = pl.multiple_of(kb * TK, TK)
        k = k_ref[pl.ds(row0, TK), :]
        v_aug = jnp.concatenate([v_ref[pl.ds(row0, TK), :], ones_cols], axis=1)
        dist = rel + ((i - kb) * TQ).astype(F32)
        for h in range(NSA_HPG):
            s = _dot_nt(q_ref[:, h * HEAD_DIM:(h + 1) * HEAD_DIM], k) - slopes[h] * dist + bias
            m_tile = jnp.broadcast_to(jnp.max(s, axis=-1, keepdims=True), (TQ, TK))
            if first:
                m_new = m_tile
            else:
                m_old = m_ref[h]
                m_new = jnp.maximum(m_old, m_tile)
            p = jnp.exp(s - m_new)
            pv = _dot(p.astype(BF16), v_aug)
            if first:
                acc[h] = pv
            else:
                alpha = jnp.exp(m_old - m_new)
                acc[h] = acc[h] * jnp.concatenate([alpha, alpha], axis=1) + pv
            m_ref[h] = m_new

    def select_bias(kb, extra_mask):
        blk = lax.broadcasted_iota(jnp.int32, (sel_ref.shape[3], TK), 0)
        key_blk = kb * (TK // L_SLC) + lax.shift_right_logical(
            lax.broadcasted_iota(jnp.int32, (sel_ref.shape[3], TK), 1), LOG2_L_SLC)
        expand = jnp.where(blk == key_blk, 1.0, 0.0).astype(BF16)
        picked = _dot(sel_ref[0, 0], expand) > 0.5
        if extra_mask is not None:
            picked = jnp.logical_and(picked, extra_mask)
        return jnp.where(picked, 0.0, NEG_INF)

    attend(ks_ref, vs_ref, i, select_bias(i, col <= row), acc_s, m_s, True)

    def slc_body(kb, _):
        @pl.when(flags_ref[(g * nq + i) * nq + kb] > 0)
        def _():
            attend(ks_ref, vs_ref, kb, select_bias(kb, None), acc_s, m_s, False)
        return 0

    lax.fori_loop(0, i, slc_body, 0)

    attend(kw_ref, vw_ref, i, jnp.where(col <= row, 0.0, NEG_INF), acc_w, m_w, True)
    n_back = WINDOW // TK
    for d in range(1, n_back + 1):
        bias = jnp.where(col > row, 0.0, NEG_INF) if d == n_back else jnp.zeros((TQ, TK), F32)

        @pl.when(i - d >= 0)
        def _():
            attend(kw_ref, vw_ref, i - d, bias, acc_w, m_w, False)

    gates = gate_ref[...]
    for h in range(NSA_HPG):
        cs = slice(h * HEAD_DIM, (h + 1) * HEAD_DIM)
        a_s = acc_s[h]
        a_w = acc_w[h]
        o_sel = a_s[:, :HEAD_DIM] / a_s[:, HEAD_DIM:HEAD_DIM + 1]
        o_win = a_w[:, :HEAD_DIM] / a_w[:, HEAD_DIM:HEAD_DIM + 1]
        o_ref[:, cs] = (gates[:, 3 * h:3 * h + 1] * oc_ref[:, cs] + gates[:, 3 * h + 1:3 * h + 2] * o_sel
                        + gates[:, 3 * h + 2:3 * h + 3] * o_win)


def _nsa_main(proj, sel, flags, o_c, gates, slopes):
    s = proj.shape[0]
    nq = s // TQ
    n_slc = s // L_SLC
    qblk = (3 * SB_W) // GROUP_W
    kv_spec = lambda cb: pl.BlockSpec((s, HEAD_DIM), lambda g, i, f: (0, cb + g))
    grid_spec = pltpu.PrefetchScalarGridSpec(
        num_scalar_prefetch=1,
        grid=(NSA_KV, nq),
        in_specs=[pl.BlockSpec(memory_space=pltpu.SMEM),
                  pl.BlockSpec((TQ, GROUP_W), lambda g, i, f: (i, qblk + g)),
                  kv_spec(CB_KS), kv_spec(CB_VS), kv_spec(CB_KW), kv_spec(CB_VW),
                  pl.BlockSpec((1, 1, TQ, n_slc), lambda g, i, f: (g, i, 0, 0)),
                  pl.BlockSpec((TQ, GROUP_W), lambda g, i, f: (i, g)),
                  pl.BlockSpec((TQ, LANES), lambda g, i, f: (i, g))],
        out_specs=pl.BlockSpec((TQ, GROUP_W), lambda g, i, f: (i, g)),
        scratch_shapes=[pltpu.VMEM((NSA_HPG, TQ, 2 * HEAD_DIM), F32), pltpu.VMEM((NSA_HPG, TQ, TK), F32),
                        pltpu.VMEM((NSA_HPG, TQ, 2 * HEAD_DIM), F32), pltpu.VMEM((NSA_HPG, TQ, TK), F32)],
    )
    vmem = 2 * 4 * s * HEAD_DIM * 2 + (16 << 20)
    return pl.pallas_call(
        _nsa_main_kernel,
        grid_spec=grid_spec,
        out_shape=jax.ShapeDtypeStruct((s, NSA_W), F32),
        compiler_params=_cparams(("parallel", "arbitrary"), vmem),
        name="nsa_select_window",
    )(flags, slopes, proj, proj, proj, proj, proj, sel, o_c, gates)


def _pad_cols(w, n):
    return jnp.pad(w, ((0, 0), (0, n - w.shape[1])))


def kernel(x, attn_norm, w_in, pos_cmp_k, pos_cmp_v, w_cmp_k1, w_cmp_k2, w_cmp_v1, w_cmp_v2, norm_sb, norm_nsa,
           w_out, ffn_norm, w_gate, w_up, w_down, final_norm):
    batch, s, d_model = x.shape
    assert batch == 1 and s % 1024 == 0 and w_in.shape[2] == MAIN_COLS + GATE_COLS
    depth = w_in.shape[0]
    d_ff = w_gate.shape[2]
    d_ff_pad = -(-d_ff // 1024) * 1024
    scale = HEAD_DIM ** -0.5
    head_idx = jnp.arange(1, NSA_HEADS + 1, dtype=F32)
    slopes = 2.0 ** (-8.0 * head_idx / NSA_HEADS)
    col_scale = jnp.concatenate([jnp.full((SB_W,), scale, F32), jnp.ones((2 * SB_W,), F32),
                                 jnp.full((NSA_W,), scale, F32), jnp.ones((MAIN_COLS - 3 * SB_W - NSA_W,), F32)])
    col_scale = col_scale.reshape(1, MAIN_COLS)
    xs = x[0]
    for l in range(depth):
        w_main = w_in[l][:, :MAIN_COLS].astype(BF16)
        w_g = w_in[l][:, MAIN_COLS:].reshape(d_model, NSA_KV, 3 * NSA_HPG)
        w_g = jnp.pad(w_g, ((0, 0), (0, 0), (0, LANES - 3 * NSA_HPG))).reshape(d_model, NSA_KV * LANES).astype(BF16)

        h = _rmsnorm(xs, attn_norm[l], BF16)
        proj = _matmul_fullk(_mm_scale_kernel, h, [w_main], [(col_scale, "col")], BF16, 1024, 512, "in_proj")
        gates = _matmul_fullk(_mm_sigmoid_kernel, h, [w_g], [], F32, 1024, NSA_KV * LANES, "gate_proj")

        o_sb = _sb_attention(proj)

        def blocks_view(cb):
            cols = proj[:, cb * HEAD_DIM:(cb + NSA_KV) * HEAD_DIM]
            r = cols.reshape(s // STRIDE_CMP, STRIDE_CMP, NSA_KV, HEAD_DIM).transpose(2, 0, 1, 3)
            return r.reshape(NSA_KV, s // STRIDE_CMP, STRIDE_CMP * HEAD_DIM)

        k_cmp = _compress(blocks_view(CB_KC), pos_cmp_k[l], w_cmp_k1[l], w_cmp_k2[l])
        v_cmp = _compress(blocks_view(CB_VC), pos_cmp_v[l], w_cmp_v1[l], w_cmp_v2[l])
        o_c, sel, tile_any = _nsa_cmp(proj, k_cmp, v_cmp, slopes)
        nq = s // TQ
        flags = tile_any.reshape(NSA_KV, nq, nq, TK // L_SLC).max(axis=-1)
        flags = (flags > 0.5).astype(jnp.int32).reshape(-1)
        o_nsa = _nsa_main(proj, sel, flags, o_c, gates, slopes)

        mixed = _rmsnorm_pair(o_sb, o_nsa, norm_sb[l], norm_nsa[l])
        xs = _matmul_fullk(_mm_res_kernel, mixed, [w_out[l].astype(BF16)], [(xs, "tile")], F32, 1024, 512, "out_proj")

        h = _rmsnorm(xs, ffn_norm[l], BF16)
        wg = _pad_cols(w_gate[l].astype(BF16), d_ff_pad)
        wu = _pad_cols(w_up[l].astype(BF16), d_ff_pad)
        wd = jnp.pad(w_down[l].astype(BF16), ((0, d_ff_pad - d_ff), (0, 0)))
        act = _matmul_fullk(_mm_swiglu_kernel, h, [wg, wu], [], BF16, 1024, 512, "ffn_gate_up")
        xs = _matmul_acc_res(act, wd, xs, 1024, 1024, d_ff_pad // 4)
    return _rmsnorm(xs, final_norm, F32)[None]
```

```python
import functools

import numpy as np
import jax
import jax.numpy as jnp
from jax import lax
from jax.experimental import pallas as pl
from jax.experimental.pallas import tpu as pltpu

HEAD_DIM = 128
SB_HEADS = 16
NSA_HEADS = 16
NSA_KV = 2
NSA_HPG = NSA_HEADS // NSA_KV
L_CMP = 32
STRIDE_CMP = 16
CMP_HIDDEN = 256
L_SLC = 64
LOG2_L_SLC = 6
N_TOPK = 16
WINDOW = 512
EPS = 1e-6
NEG_INF = -1e30
FORCE_BONUS = 1e6

LANES = 128
SUBLANES = 8
TQ = 128
TK = 128
SB_WIN = 3
NSA_TK = 256
VMEM_CAP = 56 * 1024 * 1024
SB_DEAD_LOG = -105.0

F32 = jnp.float32
BF16 = jnp.bfloat16

SB_W = SB_HEADS * HEAD_DIM
NSA_W = NSA_HEADS * HEAD_DIM
GROUP_W = NSA_HPG * HEAD_DIM
MAIN_COLS = 3 * SB_W + NSA_W + 3 * 2 * NSA_KV * HEAD_DIM
GATE_COLS = 3 * NSA_HEADS
CB_KC, CB_VC, CB_KS, CB_VS, CB_KW, CB_VW = 64, 66, 68, 70, 72, 74


def _cparams(sem, vmem_bytes):
    return pltpu.CompilerParams(dimension_semantics=sem, vmem_limit_bytes=int(min(vmem_bytes, VMEM_CAP)))


def _dot(a, b):
    return jnp.dot(a, b, preferred_element_type=F32)


def _dot_nt(a, b):
    return lax.dot_general(a, b, (((1,), (1,)), ((), ())), preferred_element_type=F32)


def _split_bf16(x):
    hi = x.astype(BF16)
    lo = (x - hi.astype(F32)).astype(BF16)
    return hi, lo


def _rms_kernel(x_ref, g_ref, o_ref):
    x = x_ref[...]
    ms = jnp.mean(x * x, axis=-1, keepdims=True)
    o_ref[...] = (x * lax.rsqrt(ms + EPS) * g_ref[...]).astype(o_ref.dtype)


def _rmsnorm(x, g, out_dtype, tm=256):
    m, d = x.shape
    return pl.pallas_call(
        _rms_kernel,
        grid=(m // tm,),
        in_specs=[pl.BlockSpec((tm, d), lambda i: (i, 0)), pl.BlockSpec((1, d), lambda i: (0, 0))],
        out_specs=pl.BlockSpec((tm, d), lambda i: (i, 0)),
        out_shape=jax.ShapeDtypeStruct((m, d), out_dtype),
        compiler_params=_cparams(("parallel",), 6 * tm * d * 4),
        name="rmsnorm",
    )(x, g.reshape(1, d))


def _rms2_kernel(a_ref, b_ref, ga_ref, gb_ref, o_ref):
    wa = a_ref.shape[1]
    for src, g_ref, off in ((a_ref, ga_ref, 0), (b_ref, gb_ref, wa)):
        x = src[...]
        ms = jnp.mean(x * x, axis=-1, keepdims=True)
        o_ref[:, off:off + x.shape[1]] = (x * lax.rsqrt(ms + EPS) * g_ref[...]).astype(o_ref.dtype)


def _rmsnorm_pair(a, b, ga, gb, tm=256):
    m, wa = a.shape
    wb = b.shape[1]
    return pl.pallas_call(
        _rms2_kernel,
        grid=(m // tm,),
        in_specs=[pl.BlockSpec((tm, wa), lambda i: (i, 0)), pl.BlockSpec((tm, wb), lambda i: (i, 0)),
                  pl.BlockSpec((1, wa), lambda i: (0, 0)), pl.BlockSpec((1, wb), lambda i: (0, 0))],
        out_specs=pl.BlockSpec((tm, wa + wb), lambda i: (i, 0)),
        out_shape=jax.ShapeDtypeStruct((m, wa + wb), BF16),
        compiler_params=_cparams(("parallel",), 6 * tm * (wa + wb) * 4),
        name="rmsnorm_pair",
    )(a, b, ga.reshape(1, wa), gb.reshape(1, wb))


def _mm_scale_kernel(a_ref, b_ref, cs_ref, o_ref):
    o_ref[...] = (_dot(a_ref[...], b_ref[...].astype(BF16)) * cs_ref[...]).astype(o_ref.dtype)


def _mm_sigmoid_kernel(a_ref, b_ref, o_ref):
    o_ref[...] = jax.nn.sigmoid(_dot(a_ref[...], b_ref[...].astype(BF16)))


def _mm_res_kernel(a_ref, b_ref, r_ref, o_ref):
    o_ref[...] = r_ref[...] + _dot(a_ref[...], b_ref[...].astype(BF16))


def _mm_swiglu_kernel(a_ref, bg_ref, bu_ref, o_ref):
    a = a_ref[...]
    gate = _dot(a, bg_ref[...].astype(BF16))
    up = _dot(a, bu_ref[...].astype(BF16))
    o_ref[...] = (jax.nn.silu(gate) * up).astype(o_ref.dtype)


def _mm_acc_res_kernel(a_ref, b_ref, r_ref, o_ref, acc_ref):
    k = pl.program_id(2)

    @pl.when(k == 0)
    def _():
        acc_ref[...] = r_ref[...]

    acc_ref[...] += _dot(a_ref[...], b_ref[...])

    @pl.when(k == pl.num_programs(2) - 1)
    def _():
        o_ref[...] = acc_ref[...]


def _matmul_fullk(kernel_fn, a, bs, layer, n, extras, out_dtype, tm, tn, name):
    m, k = a.shape
    in_specs = [pl.BlockSpec((tm, k), lambda i, j: (i, 0))]
    in_specs += [pl.BlockSpec((None, k, tn), lambda i, j: (layer, 0, j)) for _ in bs]
    w_bytes = bs[0].dtype.itemsize
    vmem = 2 * tm * k * 2 + len(bs) * k * tn * (2 * w_bytes + 2) + 2 * tm * tn * 4 + (2 + len(bs)) * tm * tn * 4
    args = [a, *bs]
    for arr, kind in extras:
        if kind == "col":
            in_specs.append(pl.BlockSpec((1, tn), lambda i, j: (0, j)))
        else:
            in_specs.append(pl.BlockSpec((tm, tn), lambda i, j: (i, j)))
            vmem += 2 * tm * tn * 4
        args.append(arr)
    return pl.pallas_call(
        kernel_fn,
        grid=(m // tm, n // tn),
        in_specs=in_specs,
        out_specs=pl.BlockSpec((tm, tn), lambda i, j: (i, j)),
        out_shape=jax.ShapeDtypeStruct((m, n), out_dtype),
        compiler_params=_cparams(("parallel", "parallel"), vmem + (4 << 20)),
        name=name,
    )(*args)


def _matmul_acc_res(a, b, res, tm, tn, tk):
    m, k = a.shape
    n = b.shape[1]
    vmem = 2 * tm * tk * 2 + 2 * tk * tn * 2 + 5 * tm * tn * 4 + tm * tn * 4
    return pl.pallas_call(
        _mm_acc_res_kernel,
        grid=(m // tm, n // tn, k // tk),
        in_specs=[pl.BlockSpec((tm, tk), lambda i, j, kk: (i, kk)),
                  pl.BlockSpec((tk, tn), lambda i, j, kk: (kk, j)),
                  pl.BlockSpec((tm, tn), lambda i, j, kk: (i, j))],
        out_specs=pl.BlockSpec((tm, tn), lambda i, j, kk: (i, j)),
        out_shape=jax.ShapeDtypeStruct((m, n), F32),
        scratch_shapes=[pltpu.VMEM((tm, tn), F32)],
        compiler_params=_cparams(("parallel", "parallel", "arbitrary"), vmem + (4 << 20)),
        name="matmul_acc_res",
    )(a, b, res)


def _sb_kernel(q_ref, k_ref, v_ref, u_ref, o_ref, carry_ref, acc_ref, *, hg):
    i = pl.program_id(1)
    u = u_ref[...]
    heads = [slice(h * HEAD_DIM, (h + 1) * HEAD_DIM) for h in range(hg)]

    def logs(z):
        log1p_e = jnp.log(1.0 + jnp.exp(-jnp.abs(z)))
        log_sig = jnp.minimum(z, 0.0) - log1p_e
        return log_sig, log_sig - z

    def tile_sums(log_not, n):
        hi, lo = _split_bf16(log_not)
        return [jnp.concatenate([hi[:, t * TK:(t + 1) * TK], lo[:, t * TK:(t + 1) * TK]], axis=1) for t in range(n)]

    base = jnp.maximum(i - (SB_WIN - 1), 0)
    row0 = pl.multiple_of(base * TK, TK)
    wk = SB_WIN * TK
    key_pos = base * TK + lax.broadcasted_iota(jnp.int32, (TQ, wk), 1)
    visible = key_pos < i * TQ + lax.broadcasted_iota(jnp.int32, (TQ, wk), 0)
    zs = [jnp.where(visible, _dot_nt(q_ref[:, cs], k_ref[pl.ds(row0, wk), cs]), NEG_INF) for cs in heads]
    lg = [logs(z) for z in zs]
    lhs = [part for (_, log_not) in lg for part in tile_sums(log_not, SB_WIN)]
    sums = _dot(jnp.concatenate(lhs, axis=0), u)
    worst = None
    for h, cs in enumerate(heads):
        carry = None
        shifted = [None] * SB_WIN
        for t in reversed(range(SB_WIN)):
            blk = sums[(h * SB_WIN + t) * TQ:(h * SB_WIN + t + 1) * TQ]
            tail, total = blk[:, :TK], blk[:, TK:]
            shifted[t] = tail if carry is None else tail + carry
            carry = total if carry is None else carry + total
        a = jnp.exp(lg[h][0] + jnp.concatenate(shifted, axis=1))
        acc_ref[h] = _dot(a.astype(BF16), v_ref[pl.ds(row0, wk), cs])
        carry_ref[h] = carry
        worst = carry if worst is None else jnp.maximum(worst, carry)

    def tile(kb):
        r0 = pl.multiple_of(kb * TK, TK)
        lg1 = [logs(_dot_nt(q_ref[:, cs], k_ref[pl.ds(r0, TK), cs])) for cs in heads]
        sums1 = _dot(jnp.concatenate([tile_sums(log_not, 1)[0] for (_, log_not) in lg1], axis=0), u)
        far = None
        for h, cs in enumerate(heads):
            blk = sums1[h * TQ:(h + 1) * TQ]
            carry = carry_ref[h]
            a = jnp.exp(lg1[h][0] + blk[:, :TK] + carry)
            acc_ref[h] += _dot(a.astype(BF16), v_ref[pl.ds(r0, TK), cs])
            carry = carry + blk[:, TK:]
            carry_ref[h] = carry
            far = carry if far is None else jnp.maximum(far, carry)
        return jnp.max(far)

    def cond(c):
        kb, far = c
        return jnp.logical_and(kb >= 0, far > SB_DEAD_LOG)

    def body(c):
        kb, _ = c
        return kb - 1, tile(kb)

    lax.while_loop(cond, body, (base - 1, jnp.max(worst)))
    for h, cs in enumerate(heads):
        o_ref[:, cs] = acc_ref[h]


def _sb_attention(proj, hg=4):
    s = proj.shape[0]
    w = hg * HEAD_DIM
    nblk = SB_W // w
    tri = np.arange(TK)[:, None] > np.arange(TK)[None, :]
    half = np.concatenate([tri, np.ones((TK, TK), bool)], axis=1)
    u = jnp.asarray(np.concatenate([half, half], axis=0), dtype=BF16)
    vmem = 2 * 2 * s * w * 2 + 2 * TQ * w * 2 + 2 * TQ * w * 4 + 2 * hg * TQ * TK * 4 + (12 << 20)
    return pl.pallas_call(
        functools.partial(_sb_kernel, hg=hg),
        grid=(nblk, s // TQ),
        in_specs=[pl.BlockSpec((TQ, w), lambda g, i: (i, g)),
                  pl.BlockSpec((s, w), lambda g, i: (0, nblk + g)),
                  pl.BlockSpec((s, w), lambda g, i: (0, 2 * nblk + g)),
                  pl.BlockSpec((2 * TK, 2 * TK), lambda g, i: (0, 0))],
        out_specs=pl.BlockSpec((TQ, w), lambda g, i: (i, g)),
        out_shape=jax.ShapeDtypeStruct((s, SB_W), F32),
        scratch_shapes=[pltpu.VMEM((hg, TQ, TK), F32), pltpu.VMEM((hg, TQ, HEAD_DIM), F32)],
        compiler_params=_cparams(("parallel", "arbitrary"), vmem),
        name="sb_attention",
    )(proj, proj, proj, u)


def _compress_kernel(r_ref, pos_ref, w1_ref, w2_ref, o_ref):
    r = r_ref[0]
    half = r.shape[1]
    first = _dot(r, w1_ref[:half, :])
    second = _dot(r, w1_ref[half:, :])
    second = pltpu.roll(second, second.shape[0] - 1, 0)
    pos_term = _dot(pos_ref[...].astype(BF16), w1_ref[...])[0:1, :]
    hidden = jax.nn.gelu(first + second + pos_term)
    o_ref[0] = _dot(hidden.astype(BF16), w2_ref[...]).astype(o_ref.dtype)


def _compress(r, pos, w1, w2):
    g, n, half = r.shape
    pos_rows = jnp.zeros((SUBLANES, 2 * half), F32).at[0].set(pos.reshape(-1))
    return pl.pallas_call(
        _compress_kernel,
        grid=(g,),
        in_specs=[pl.BlockSpec((1, n, half), lambda i: (i, 0, 0)),
                  pl.BlockSpec((SUBLANES, 2 * half), lambda i: (0, 0)),
                  pl.BlockSpec((2 * half, CMP_HIDDEN), lambda i: (0, 0)),
                  pl.BlockSpec((CMP_HIDDEN, HEAD_DIM), lambda i: (0, 0))],
        out_specs=pl.BlockSpec((1, n, HEAD_DIM), lambda i: (i, 0, 0)),
        out_shape=jax.ShapeDtypeStruct((g, n, HEAD_DIM), BF16),
        compiler_params=_cparams(("parallel",), 24 << 20),
        name="nsa_compress",
    )(r, pos_rows, w1.astype(BF16), w2.astype(BF16))


def _stack_heads(q_ref, qs_ref):
    for h in range(NSA_HPG):
        qs_ref[h * TQ:(h + 1) * TQ, :] = q_ref[:, h * HEAD_DIM:(h + 1) * HEAD_DIM]


def _nsa_cmp_kernel(slopes_ref, q_ref, kc_ref, vc_ref, msel_ref, oc_ref, sel_ref, flag_ref, qs_ref, *, n_top):
    g = pl.program_id(0)
    i = pl.program_id(1)
    kc = kc_ref[0]
    vc = vc_ref[0]
    n_cmp = kc.shape[0]
    n_slc = msel_ref.shape[1]
    t_rel = lax.broadcasted_iota(jnp.int32, (TQ, n_cmp), 0)
    end_rel = lax.broadcasted_iota(jnp.int32, (TQ, n_cmp), 1) * STRIDE_CMP + (L_CMP - 1) - i * TQ
    bias = jnp.where(end_rel <= t_rel, 0.0, NEG_INF)
    end_row = end_rel[0:1, :].astype(F32)
    t_col = i * TQ + lax.broadcasted_iota(jnp.int32, (TQ, 1), 0)
    row_valid = jnp.where(t_col >= L_CMP - 1, 1.0, 0.0)
    _stack_heads(q_ref, qs_ref)
    z = _dot_nt(qs_ref[...], kc)
    p_sum = jnp.zeros((TQ, n_cmp), F32)
    ps = []
    for h in range(NSA_HPG):
        s = z[h * TQ:(h + 1) * TQ] + (bias + slopes_ref[g * NSA_HPG + h] * end_row)
        m = jnp.max(s, axis=-1, keepdims=True)
        e = jnp.exp(s - m)
        p = e * (row_valid / jnp.sum(e, axis=-1, keepdims=True))
        ps.append(p.astype(BF16))
        p_sum = p_sum + p
    oc = _dot(jnp.concatenate(ps, axis=0), vc)
    for h in range(NSA_HPG):
        oc_ref[:, h * HEAD_DIM:(h + 1) * HEAD_DIM] = oc[h * TQ:(h + 1) * TQ]
    hi, lo = _split_bf16(p_sum)
    msel = msel_ref[...]
    imp = _dot(hi, msel) + _dot(lo, msel)
    score_src = imp.T
    blk = lax.broadcasted_iota(jnp.int32, (n_slc, TQ), 0)
    cur = lax.shift_right_logical(i * TQ + lax.broadcasted_iota(jnp.int32, (n_slc, TQ), 1), LOG2_L_SLC)
    valid = blk <= cur
    bonus = jnp.where(blk == 0, FORCE_BONUS, 0.0)
    bonus = jnp.where(blk == cur, FORCE_BONUS, bonus)
    bonus = jnp.where(blk == cur - 1, FORCE_BONUS, bonus)
    score = jnp.where(valid, score_src + bonus, NEG_INF)
    nb = n_slc // SUBLANES
    parts = [score[b * SUBLANES:(b + 1) * SUBLANES, :] for b in range(nb)]
    beaten = [jnp.zeros((SUBLANES, TQ), F32) for _ in range(nb)]
    sub = lax.broadcasted_iota(jnp.int32, (SUBLANES, TQ), 0)
    for jp in range(n_slc):
        rival = jnp.broadcast_to(score[jp:jp + 1, :], (SUBLANES, TQ))
        bj, sj = divmod(jp, SUBLANES)
        for b in range(nb):
            strict = jnp.where(rival > parts[b], 1.0, 0.0)
            ties_too = jnp.where(rival >= parts[b], 1.0, 0.0)
            if b < bj:
                wins = strict
            elif b > bj:
                wins = ties_too
            else:
                wins = jnp.where(sub > sj, ties_too, strict)
            beaten[b] = beaten[b] + wins
    rank = jnp.concatenate(beaten, axis=0)
    chosen = jnp.where(valid, jnp.where(rank < n_top, 1.0, 0.0), 0.0)
    chosen_q = chosen.T
    sel_ref[0, 0] = chosen_q.astype(sel_ref.dtype)
    flag_ref[0, 0] = jnp.max(chosen_q, axis=0, keepdims=True)


def _select_matrix(n_cmp_pad, n_slc):
    rs = L_SLC // STRIDE_CMP
    rc = L_CMP // STRIDE_CMP
    m = np.zeros((n_cmp_pad, n_slc), np.float32)
    for j in range(n_slc):
        for a in range(rs):
            for b in range(rc):
                src = rs * j - a - b
                if 0 <= src < n_cmp_pad - 1:
                    m[src, j] += 1.0
    return jnp.asarray(m, dtype=BF16)


def _nsa_cmp(proj, k_cmp, v_cmp, slopes):
    s = proj.shape[0]
    n_cmp = k_cmp.shape[1]
    n_slc = s // L_SLC
    nq = s // TQ
    msel = _select_matrix(n_cmp, n_slc)
    qblk = (3 * SB_W) // GROUP_W
    return pl.pallas_call(
        functools.partial(_nsa_cmp_kernel, n_top=min(N_TOPK, n_slc)),
        grid=(NSA_KV, nq),
        in_specs=[pl.BlockSpec(memory_space=pltpu.SMEM),
                  pl.BlockSpec((TQ, GROUP_W), lambda g, i: (i, qblk + g)),
                  pl.BlockSpec((1, n_cmp, HEAD_DIM), lambda g, i: (g, 0, 0)),
                  pl.BlockSpec((1, n_cmp, HEAD_DIM), lambda g, i: (g, 0, 0)),
                  pl.BlockSpec((n_cmp, n_slc), lambda g, i: (0, 0))],
        out_specs=[pl.BlockSpec((TQ, GROUP_W), lambda g, i: (i, g)),
                   pl.BlockSpec((1, 1, TQ, n_slc), lambda g, i: (g, i, 0, 0)),
                   pl.BlockSpec((1, 1, 1, n_slc), lambda g, i: (g, i, 0, 0))],
        out_shape=[jax.ShapeDtypeStruct((s, NSA_W), F32),
                   jax.ShapeDtypeStruct((NSA_KV, nq, TQ, n_slc), BF16),
                   jax.ShapeDtypeStruct((NSA_KV, nq, 1, n_slc), F32)],
        scratch_shapes=[pltpu.VMEM((NSA_HPG * TQ, HEAD_DIM), BF16)],
        compiler_params=_cparams(("parallel", "parallel"), 40 << 20),
        name="nsa_cmp_select",
    )(slopes, proj, k_cmp, v_cmp, msel)


def _nsa_main_kernel(flags_ref, slopes_ref, q_ref, ks_ref, vs_ref, kw_ref, vw_ref, sel_ref, oc_ref, gate_ref,
                     o_ref, qs_ref, acc_s, m_s, acc_w, m_w):
    g = pl.program_id(0)
    i = pl.program_id(1)
    n_kt = pl.num_programs(1) * TQ // NSA_TK
    _stack_heads(q_ref, qs_ref)
    slopes = [slopes_ref[g * NSA_HPG + h] for h in range(NSA_HPG)]
    ones_cols = jnp.ones((NSA_TK, HEAD_DIM), BF16)
    t_rel = lax.broadcasted_iota(jnp.int32, (TQ, NSA_TK), 0)
    lane = lax.broadcasted_iota(jnp.int32, (TQ, NSA_TK), 1)
    lane_row = lax.broadcasted_iota(jnp.int32, (1, NSA_TK), 1)
    diag = lax.div(i * TQ, NSA_TK)

    def attend(k_ref, v_ref, kt, bias, acc, m_ref, first):
        row0 = pl.multiple_of(kt * NSA_TK, NSA_TK)
        z = _dot_nt(qs_ref[...], k_ref[pl.ds(row0, NSA_TK), :])
        v_aug = jnp.concatenate([v_ref[pl.ds(row0, NSA_TK), :], ones_cols], axis=1)
        key_rel = (kt * NSA_TK - i * TQ + lane_row).astype(F32)
        ps, alphas = [], []
        for h in range(NSA_HPG):
            rows = slice(h * TQ, (h + 1) * TQ)
            s = z[rows] + (bias + slopes[h] * key_rel)
            m_tile = jnp.max(s, axis=-1, keepdims=True)
            if first:
                m_new = jnp.broadcast_to(m_tile, (TQ, LANES))
            else:
                m_old = m_ref[rows]
                m_new = jnp.maximum(m_old, m_tile)
                alphas.append(jnp.exp(m_old - m_new))
            m_ref[rows] = m_new
            ps.append(jnp.exp(s - jnp.concatenate([m_new] * (NSA_TK // LANES), axis=1)).astype(BF16))
        pv = _dot(jnp.concatenate(ps, axis=0), v_aug)
        for h in range(NSA_HPG):
            rows = slice(h * TQ, (h + 1) * TQ)
            if first:
                acc[rows] = pv[rows]
            else:
                acc[rows] = acc[rows] * jnp.concatenate([alphas[h], alphas[h]], axis=1) + pv[rows]

    def select_bias(kt, causal):
        n_slc = sel_ref.shape[3]
        blk = lax.broadcasted_iota(jnp.int32, (n_slc, NSA_TK), 0)
        key_blk = kt * (NSA_TK // L_SLC) + lax.shift_right_logical(
            lax.broadcasted_iota(jnp.int32, (n_slc, NSA_TK), 1), LOG2_L_SLC)
        expand = jnp.where(blk == key_blk, 1.0, 0.0).astype(BF16)
        picked = _dot(sel_ref[0, 0], expand)
        if causal:
            picked = jnp.where(kt * NSA_TK - i * TQ + lane <= t_rel, picked, 0.0)
        return jnp.where(picked > 0.5, 0.0, NEG_INF)

    attend(ks_ref, vs_ref, diag, select_bias(diag, True), acc_s, m_s, True)

    def slc_body(kt, _):
        @pl.when(flags_ref[(g * pl.num_programs(1) + i) * n_kt + kt] > 0)
        def _():
            attend(ks_ref, vs_ref, kt, select_bias(kt, False), acc_s, m_s, False)
        return 0

    lax.fori_loop(0, diag, slc_body, 0)

    def window_bias(kt):
        dist = i * TQ + t_rel - (kt * NSA_TK + lane)
        return jnp.where(dist >= 0, jnp.where(dist < WINDOW, 0.0, NEG_INF), NEG_INF)

    attend(kw_ref, vw_ref, diag, window_bias(diag), acc_w, m_w, True)
    for d in range(1, WINDOW // NSA_TK + 1):
        @pl.when(diag - d >= 0)
        def _():
            attend(kw_ref, vw_ref, diag - d, window_bias(diag - d), acc_w, m_w, False)

    gates = gate_ref[...]
    for h in range(NSA_HPG):
        cs = slice(h * HEAD_DIM, (h + 1) * HEAD_DIM)
        a_s = acc_s[h * TQ:(h + 1) * TQ]
        a_w = acc_w[h * TQ:(h + 1) * TQ]
        o_sel = a_s[:, :HEAD_DIM] / a_s[:, HEAD_DIM:HEAD_DIM + 1]
        o_win = a_w[:, :HEAD_DIM] / a_w[:, HEAD_DIM:HEAD_DIM + 1]
        o_ref[:, cs] = (gates[:, 3 * h:3 * h + 1] * oc_ref[:, cs] + gates[:, 3 * h + 1:3 * h + 2] * o_sel
                        + gates[:, 3 * h + 2:3 * h + 3] * o_win)


def _nsa_main(proj, sel, flags, o_c, gates, slopes):
    s = proj.shape[0]
    nq = s // TQ
    n_slc = s // L_SLC
    qblk = (3 * SB_W) // GROUP_W
    kv_spec = lambda cb: pl.BlockSpec((s, HEAD_DIM), lambda g, i, f: (0, cb + g))
    rows = NSA_HPG * TQ
    grid_spec = pltpu.PrefetchScalarGridSpec(
        num_scalar_prefetch=1,
        grid=(NSA_KV, nq),
        in_specs=[pl.BlockSpec(memory_space=pltpu.SMEM),
                  pl.BlockSpec((TQ, GROUP_W), lambda g, i, f: (i, qblk + g)),
                  kv_spec(CB_KS), kv_spec(CB_VS), kv_spec(CB_KW), kv_spec(CB_VW),
                  pl.BlockSpec((1, 1, TQ, n_slc), lambda g, i, f: (g, i, 0, 0)),
                  pl.BlockSpec((TQ, GROUP_W), lambda g, i, f: (i, g)),
                  pl.BlockSpec((TQ, LANES), lambda g, i, f: (i, g))],
        out_specs=pl.BlockSpec((TQ, GROUP_W), lambda g, i, f: (i, g)),
        scratch_shapes=[pltpu.VMEM((rows, HEAD_DIM), BF16),
                        pltpu.VMEM((rows, 2 * HEAD_DIM), F32), pltpu.VMEM((rows, LANES), F32),
                        pltpu.VMEM((rows, 2 * HEAD_DIM), F32), pltpu.VMEM((rows, LANES), F32)],
    )
    vmem = 2 * 4 * s * HEAD_DIM * 2 + (20 << 20)
    return pl.pallas_call(
        _nsa_main_kernel,
        grid_spec=grid_spec,
        out_shape=jax.ShapeDtypeStruct((s, NSA_W), F32),
        compiler_params=_cparams(("parallel", "arbitrary"), vmem),
        name="nsa_select_window",
    )(flags, slopes, proj, proj, proj, proj, proj, sel, o_c, gates)


def kernel(x, attn_norm, w_in, pos_cmp_k, pos_cmp_v, w_cmp_k1, w_cmp_k2, w_cmp_v1, w_cmp_v2, norm_sb, norm_nsa,
           w_out, ffn_norm, w_gate, w_up, w_down, final_norm):
    batch, s, d_model = x.shape
    assert batch == 1 and s % 1024 == 0 and w_in.shape[2] == MAIN_COLS + GATE_COLS
    depth = w_in.shape[0]
    d_ff = w_gate.shape[2]
    assert d_ff % (2 * LANES) == 0 and (d_ff // 2) % LANES == 0
    scale = HEAD_DIM ** -0.5
    head_idx = jnp.arange(1, NSA_HEADS + 1, dtype=F32)
    slopes = 2.0 ** (-8.0 * head_idx / NSA_HEADS)
    col_scale = jnp.concatenate([jnp.full((SB_W,), scale, F32), jnp.ones((2 * SB_W,), F32),
                                 jnp.full((NSA_W,), scale, F32), jnp.ones((MAIN_COLS - 3 * SB_W - NSA_W,), F32)])
    col_scale = col_scale.reshape(1, MAIN_COLS)
    xs = x[0]
    for l in range(depth):
        w_g = w_in[l][:, MAIN_COLS:].reshape(d_model, NSA_KV, 3 * NSA_HPG)
        w_g = jnp.pad(w_g, ((0, 0), (0, 0), (0, LANES - 3 * NSA_HPG))).reshape(1, d_model, NSA_KV * LANES)

        h = _rmsnorm(xs, attn_norm[l], BF16)
        proj = _matmul_fullk(_mm_scale_kernel, h, [w_in], l, MAIN_COLS, [(col_scale, "col")], BF16, 1024, 512,
                             "in_proj")
        gates = _matmul_fullk(_mm_sigmoid_kernel, h, [w_g], 0, NSA_KV * LANES, [], F32, 1024, NSA_KV * LANES,
                              "gate_proj")

        o_sb = _sb_attention(proj)

        def blocks_view(cb):
            cols = proj[:, cb * HEAD_DIM:(cb + NSA_KV) * HEAD_DIM]
            r = cols.reshape(s // STRIDE_CMP, STRIDE_CMP, NSA_KV, HEAD_DIM).transpose(2, 0, 1, 3)
            return r.reshape(NSA_KV, s // STRIDE_CMP, STRIDE_CMP * HEAD_DIM)

        k_cmp = _compress(blocks_view(CB_KC), pos_cmp_k[l], w_cmp_k1[l], w_cmp_k2[l])
        v_cmp = _compress(blocks_view(CB_VC), pos_cmp_v[l], w_cmp_v1[l], w_cmp_v2[l])
        o_c, sel, tile_any = _nsa_cmp(proj, k_cmp, v_cmp, slopes)
        nq = s // TQ
        flags = tile_any.reshape(NSA_KV, nq, s // NSA_TK, NSA_TK // L_SLC).max(axis=-1)
        flags = (flags > 0.5).astype(jnp.int32).reshape(-1)
        o_nsa = _nsa_main(proj, sel, flags, o_c, gates, slopes)

        mixed = _rmsnorm_pair(o_sb, o_nsa, norm_sb[l], norm_nsa[l])
        xs = _matmul_fullk(_mm_res_kernel, mixed, [w_out], l, d_model, [(xs, "tile")], F32, 1024, 512, "out_proj")

        h = _rmsnorm(xs, ffn_norm[l], BF16)
        act = _matmul_fullk(_mm_swiglu_kernel, h, [w_gate, w_up], l, d_ff, [], BF16, 1024, 2 * LANES, "ffn_gate_up")
        xs = _matmul_acc_res(act, w_down[l].astype(BF16), xs, 1024, 512, d_ff // 2)
    return _rmsnorm(xs, final_norm, F32)[None]
```

```python
import functools

import numpy as np
import jax
import jax.numpy as jnp
from jax import lax
from jax.experimental import pallas as pl
from jax.experimental.pallas import tpu as pltpu

HEAD_DIM = 128
SB_HEADS = 16
NSA_HEADS = 16
NSA_KV = 2
NSA_HPG = NSA_HEADS // NSA_KV
L_CMP = 32
STRIDE_CMP = 16
CMP_HIDDEN = 256
L_SLC = 64
LOG2_L_SLC = 6
N_TOPK = 16
WINDOW = 512
EPS = 1e-6
NEG_INF = -1e30
FORCE_BONUS = 1e6

LANES = 128
SUBLANES = 8
TQ = 128
TK = 128
SB_WIN = 3
NSA_TK = 256
VMEM_CAP = 56 * 1024 * 1024
SB_DEAD_LOG = -105.0

F32 = jnp.float32
BF16 = jnp.bfloat16

SB_W = SB_HEADS * HEAD_DIM
NSA_W = NSA_HEADS * HEAD_DIM
GROUP_W = NSA_HPG * HEAD_DIM
MAIN_COLS = 3 * SB_W + NSA_W + 3 * 2 * NSA_KV * HEAD_DIM
GATE_COLS = 3 * NSA_HEADS
CB_KC, CB_VC, CB_KS, CB_VS, CB_KW, CB_VW = 64, 66, 68, 70, 72, 74


def _cparams(sem, vmem_bytes):
    return pltpu.CompilerParams(dimension_semantics=sem, vmem_limit_bytes=int(min(vmem_bytes, VMEM_CAP)))


def _dot(a, b):
    return jnp.dot(a, b, preferred_element_type=F32)


def _dot_nt(a, b):
    return lax.dot_general(a, b, (((1,), (1,)), ((), ())), preferred_element_type=F32)


def _split_bf16(x):
    hi = x.astype(BF16)
    lo = (x - hi.astype(F32)).astype(BF16)
    return hi, lo


def _rms_kernel(x_ref, g_ref, o_ref):
    x = x_ref[...]
    ms = jnp.mean(x * x, axis=-1, keepdims=True)
    o_ref[...] = (x * lax.rsqrt(ms + EPS) * g_ref[...]).astype(o_ref.dtype)


def _rmsnorm(x, g, out_dtype, tm=256):
    m, d = x.shape
    return pl.pallas_call(
        _rms_kernel,
        grid=(m // tm,),
        in_specs=[pl.BlockSpec((tm, d), lambda i: (i, 0)), pl.BlockSpec((1, d), lambda i: (0, 0))],
        out_specs=pl.BlockSpec((tm, d), lambda i: (i, 0)),
        out_shape=jax.ShapeDtypeStruct((m, d), out_dtype),
        compiler_params=_cparams(("parallel",), 6 * tm * d * 4),
        name="rmsnorm",
    )(x, g.reshape(1, d))


def _rms2_kernel(a_ref, b_ref, ga_ref, gb_ref, o_ref):
    wa = a_ref.shape[1]
    for src, g_ref, off in ((a_ref, ga_ref, 0), (b_ref, gb_ref, wa)):
        x = src[...]
        ms = jnp.mean(x * x, axis=-1, keepdims=True)
        o_ref[:, off:off + x.shape[1]] = (x * lax.rsqrt(ms + EPS) * g_ref[...]).astype(o_ref.dtype)


def _rmsnorm_pair(a, b, ga, gb, tm=256):
    m, wa = a.shape
    wb = b.shape[1]
    return pl.pallas_call(
        _rms2_kernel,
        grid=(m // tm,),
        in_specs=[pl.BlockSpec((tm, wa), lambda i: (i, 0)), pl.BlockSpec((tm, wb), lambda i: (i, 0)),
                  pl.BlockSpec((1, wa), lambda i: (0, 0)), pl.BlockSpec((1, wb), lambda i: (0, 0))],
        out_specs=pl.BlockSpec((tm, wa + wb), lambda i: (i, 0)),
        out_shape=jax.ShapeDtypeStruct((m, wa + wb), BF16),
        compiler_params=_cparams(("parallel",), 6 * tm * (wa + wb) * 4),
        name="rmsnorm_pair",
    )(a, b, ga.reshape(1, wa), gb.reshape(1, wb))


def _mm_scale_nt_kernel(a_ref, bt_ref, cs_ref, o_ref):
    o_ref[...] = (_dot_nt(a_ref[...], bt_ref[...].astype(BF16)) * cs_ref[...]).astype(o_ref.dtype)


def _mm_sigmoid_kernel(a_ref, b_ref, o_ref):
    o_ref[...] = jax.nn.sigmoid(_dot(a_ref[...], b_ref[...].astype(BF16)))


def _mm_res_kernel(a_ref, b_ref, r_ref, o_ref):
    o_ref[...] = r_ref[...] + _dot(a_ref[...], b_ref[...].astype(BF16))


def _mm_swiglu_kernel(a_ref, bg_ref, bu_ref, o_ref):
    a = a_ref[...]
    gate = _dot(a, bg_ref[...].astype(BF16))
    up = _dot(a, bu_ref[...].astype(BF16))
    o_ref[...] = (jax.nn.silu(gate) * up).astype(o_ref.dtype)


def _mm_acc_res_kernel(a_ref, b_ref, r_ref, o_ref, acc_ref):
    k = pl.program_id(2)

    @pl.when(k == 0)
    def _():
        acc_ref[...] = r_ref[...]

    acc_ref[...] += _dot(a_ref[...], b_ref[...])

    @pl.when(k == pl.num_programs(2) - 1)
    def _():
        o_ref[...] = acc_ref[...]


def _matmul_fullk(kernel_fn, a, bs, layer, n, extras, out_dtype, tm, tn, name, transposed=False):
    m, k = a.shape
    in_specs = [pl.BlockSpec((tm, k), lambda i, j: (i, 0))]
    if transposed:
        in_specs += [pl.BlockSpec((None, tn, k), lambda i, j: (layer, j, 0)) for _ in bs]
    else:
        in_specs += [pl.BlockSpec((None, k, tn), lambda i, j: (layer, 0, j)) for _ in bs]
    w_bytes = bs[0].dtype.itemsize
    vmem = 2 * tm * k * 2 + len(bs) * k * tn * (2 * w_bytes + 2) + 2 * tm * tn * 4 + (2 + len(bs)) * tm * tn * 4
    args = [a, *bs]
    for arr, kind in extras:
        if kind == "col":
            in_specs.append(pl.BlockSpec((1, tn), lambda i, j: (0, j)))
        else:
            in_specs.append(pl.BlockSpec((tm, tn), lambda i, j: (i, j)))
            vmem += 2 * tm * tn * 4
        args.append(arr)
    return pl.pallas_call(
        kernel_fn,
        grid=(m // tm, n // tn),
        in_specs=in_specs,
        out_specs=pl.BlockSpec((tm, tn), lambda i, j: (i, j)),
        out_shape=jax.ShapeDtypeStruct((m, n), out_dtype),
        compiler_params=_cparams(("parallel", "parallel"), vmem + (4 << 20)),
        name=name,
    )(*args)


def _matmul_acc_res(a, b, res, tm, tn, tk):
    m, k = a.shape
    n = b.shape[1]
    vmem = 2 * tm * tk * 2 + 2 * tk * tn * 2 + 5 * tm * tn * 4 + tm * tn * 4
    return pl.pallas_call(
        _mm_acc_res_kernel,
        grid=(m // tm, n // tn, k // tk),
        in_specs=[pl.BlockSpec((tm, tk), lambda i, j, kk: (i, kk)),
                  pl.BlockSpec((tk, tn), lambda i, j, kk: (kk, j)),
                  pl.BlockSpec((tm, tn), lambda i, j, kk: (i, j))],
        out_specs=pl.BlockSpec((tm, tn), lambda i, j, kk: (i, j)),
        out_shape=jax.ShapeDtypeStruct((m, n), F32),
        scratch_shapes=[pltpu.VMEM((tm, tn), F32)],
        compiler_params=_cparams(("parallel", "parallel", "arbitrary"), vmem + (4 << 20)),
        name="matmul_acc_res",
    )(a, b, res)


def _sb_kernel(q_ref, k_ref, v_ref, u_ref, o_ref, carry_ref, acc_ref, *, hg):
    i = pl.program_id(1)
    u = u_ref[...]
    heads = [slice(h * HEAD_DIM, (h + 1) * HEAD_DIM) for h in range(hg)]

    def logs(z):
        log1p_e = jnp.log(1.0 + jnp.exp(-jnp.abs(z)))
        log_sig = jnp.minimum(z, 0.0) - log1p_e
        return log_sig, log_sig - z

    def tile_sums(log_not, n):
        hi, lo = _split_bf16(log_not)
        return [jnp.concatenate([hi[:, t * TK:(t + 1) * TK], lo[:, t * TK:(t + 1) * TK]], axis=1) for t in range(n)]

    base = jnp.maximum(i - (SB_WIN - 1), 0)
    row0 = pl.multiple_of(base * TK, TK)
    wk = SB_WIN * TK
    key_pos = base * TK + lax.broadcasted_iota(jnp.int32, (TQ, wk), 1)
    visible = key_pos < i * TQ + lax.broadcasted_iota(jnp.int32, (TQ, wk), 0)
    zs = [jnp.where(visible, _dot_nt(q_ref[:, cs], k_ref[pl.ds(row0, wk), cs]), NEG_INF) for cs in heads]
    lg = [logs(z) for z in zs]
    lhs = [part for (_, log_not) in lg for part in tile_sums(log_not, SB_WIN)]
    sums = _dot(jnp.concatenate(lhs, axis=0), u)
    worst = None
    for h, cs in enumerate(heads):
        carry = None
        shifted = [None] * SB_WIN
        for t in reversed(range(SB_WIN)):
            blk = sums[(h * SB_WIN + t) * TQ:(h * SB_WIN + t + 1) * TQ]
            tail, total = blk[:, :TK], blk[:, TK:]
            shifted[t] = tail if carry is None else tail + carry
            carry = total if carry is None else carry + total
        a = jnp.exp(lg[h][0] + jnp.concatenate(shifted, axis=1))
        acc_ref[h] = _dot(a.astype(BF16), v_ref[pl.ds(row0, wk), cs])
        carry_ref[h] = carry
        worst = carry if worst is None else jnp.maximum(worst, carry)

    def tile(kb):
        r0 = pl.multiple_of(kb * TK, TK)
        lg1 = [logs(_dot_nt(q_ref[:, cs], k_ref[pl.ds(r0, TK), cs])) for cs in heads]
        sums1 = _dot(jnp.concatenate([tile_sums(log_not, 1)[0] for (_, log_not) in lg1], axis=0), u)
        far = None
        for h, cs in enumerate(heads):
            blk = sums1[h * TQ:(h + 1) * TQ]
            carry = carry_ref[h]
            a = jnp.exp(lg1[h][0] + blk[:, :TK] + carry)
            acc_ref[h] += _dot(a.astype(BF16), v_ref[pl.ds(r0, TK), cs])
            carry = carry + blk[:, TK:]
            carry_ref[h] = carry
            far = carry if far is None else jnp.maximum(far, carry)
        return jnp.max(far)

    def cond(c):
        kb, far = c
        return jnp.logical_and(kb >= 0, far > SB_DEAD_LOG)

    def body(c):
        kb, _ = c
        return kb - 1, tile(kb)

    lax.while_loop(cond, body, (base - 1, jnp.max(worst)))
    for h, cs in enumerate(heads):
        o_ref[:, cs] = acc_ref[h]


def _sb_attention(proj, hg=4):
    s = proj.shape[0]
    w = hg * HEAD_DIM
    nblk = SB_W // w
    tri = np.arange(TK)[:, None] > np.arange(TK)[None, :]
    half = np.concatenate([tri, np.ones((TK, TK), bool)], axis=1)
    u = jnp.asarray(np.concatenate([half, half], axis=0), dtype=BF16)
    vmem = 2 * 2 * s * w * 2 + 2 * TQ * w * 2 + 2 * TQ * w * 4 + 2 * hg * TQ * TK * 4 + (12 << 20)
    return pl.pallas_call(
        functools.partial(_sb_kernel, hg=hg),
        grid=(nblk, s // TQ),
        in_specs=[pl.BlockSpec((TQ, w), lambda g, i: (i, g)),
                  pl.BlockSpec((s, w), lambda g, i: (0, nblk + g)),
                  pl.BlockSpec((s, w), lambda g, i: (0, 2 * nblk + g)),
                  pl.BlockSpec((2 * TK, 2 * TK), lambda g, i: (0, 0))],
        out_specs=pl.BlockSpec((TQ, w), lambda g, i: (i, g)),
        out_shape=jax.ShapeDtypeStruct((s, SB_W), F32),
        scratch_shapes=[pltpu.VMEM((hg, TQ, TK), F32), pltpu.VMEM((hg, TQ, HEAD_DIM), F32)],
        compiler_params=_cparams(("parallel", "arbitrary"), vmem),
        name="sb_attention",
    )(proj, proj, proj, u)


def _compress_kernel(r_ref, pos_ref, w1_ref, w2_ref, o_ref):
    r = r_ref[0]
    half = r.shape[1]
    first = _dot(r, w1_ref[:half, :])
    second = _dot(r, w1_ref[half:, :])
    second = pltpu.roll(second, second.shape[0] - 1, 0)
    pos_term = _dot(pos_ref[...].astype(BF16), w1_ref[...])[0:1, :]
    hidden = jax.nn.gelu(first + second + pos_term)
    o_ref[0] = _dot(hidden.astype(BF16), w2_ref[...]).astype(o_ref.dtype)


def _compress(r, pos, w1, w2):
    g, n, half = r.shape
    pos_rows = jnp.zeros((SUBLANES, 2 * half), F32).at[0].set(pos.reshape(-1))
    return pl.pallas_call(
        _compress_kernel,
        grid=(g,),
        in_specs=[pl.BlockSpec((1, n, half), lambda i: (i, 0, 0)),
                  pl.BlockSpec((SUBLANES, 2 * half), lambda i: (0, 0)),
                  pl.BlockSpec((2 * half, CMP_HIDDEN), lambda i: (0, 0)),
                  pl.BlockSpec((CMP_HIDDEN, HEAD_DIM), lambda i: (0, 0))],
        out_specs=pl.BlockSpec((1, n, HEAD_DIM), lambda i: (i, 0, 0)),
        out_shape=jax.ShapeDtypeStruct((g, n, HEAD_DIM), BF16),
        compiler_params=_cparams(("parallel",), 24 << 20),
        name="nsa_compress",
    )(r, pos_rows, w1.astype(BF16), w2.astype(BF16))


def _stack_heads(q_ref, qs_ref):
    for h in range(NSA_HPG):
        qs_ref[h * TQ:(h + 1) * TQ, :] = q_ref[:, h * HEAD_DIM:(h + 1) * HEAD_DIM]


def _nsa_cmp_kernel(slopes_ref, q_ref, kc_ref, vc_ref, msel_ref, oc_ref, sel_ref, flag_ref, qs_ref, *, n_top):
    g = pl.program_id(0)
    i = pl.program_id(1)
    kc = kc_ref[0]
    vc = vc_ref[0]
    n_cmp = kc.shape[0]
    n_slc = msel_ref.shape[1]
    t_rel = lax.broadcasted_iota(jnp.int32, (TQ, n_cmp), 0)
    end_rel = lax.broadcasted_iota(jnp.int32, (TQ, n_cmp), 1) * STRIDE_CMP + (L_CMP - 1) - i * TQ
    bias = jnp.where(end_rel <= t_rel, 0.0, NEG_INF)
    end_row = end_rel[0:1, :].astype(F32)
    t_col = i * TQ + lax.broadcasted_iota(jnp.int32, (TQ, 1), 0)
    row_valid = jnp.where(t_col >= L_CMP - 1, 1.0, 0.0)
    _stack_heads(q_ref, qs_ref)
    z = _dot_nt(qs_ref[...], kc)
    p_sum = jnp.zeros((TQ, n_cmp), F32)
    ps = []
    for h in range(NSA_HPG):
        s = z[h * TQ:(h + 1) * TQ] + (bias + slopes_ref[g * NSA_HPG + h] * end_row)
        m = jnp.max(s, axis=-1, keepdims=True)
        e = jnp.exp(s - m)
        p = e * (row_valid / jnp.sum(e, axis=-1, keepdims=True))
        ps.append(p.astype(BF16))
        p_sum = p_sum + p
    oc = _dot(jnp.concatenate(ps, axis=0), vc)
    for h in range(NSA_HPG):
        oc_ref[:, h * HEAD_DIM:(h + 1) * HEAD_DIM] = oc[h * TQ:(h + 1) * TQ]
    hi, lo = _split_bf16(p_sum)
    msel = msel_ref[...]
    imp = _dot(hi, msel) + _dot(lo, msel)
    score_src = imp.T
    blk = lax.broadcasted_iota(jnp.int32, (n_slc, TQ), 0)
    cur = lax.shift_right_logical(i * TQ + lax.broadcasted_iota(jnp.int32, (n_slc, TQ), 1), LOG2_L_SLC)
    valid = blk <= cur
    bonus = jnp.where(blk == 0, FORCE_BONUS, 0.0)
    bonus = jnp.where(blk == cur, FORCE_BONUS, bonus)
    bonus = jnp.where(blk == cur - 1, FORCE_BONUS, bonus)
    score = jnp.where(valid, score_src + bonus, NEG_INF)
    nb = n_slc // SUBLANES
    parts = [score[b * SUBLANES:(b + 1) * SUBLANES, :] for b in range(nb)]
    beaten = [jnp.zeros((SUBLANES, TQ), F32) for _ in range(nb)]
    sub = lax.broadcasted_iota(jnp.int32, (SUBLANES, TQ), 0)
    for jp in range(n_slc):
        rival = jnp.broadcast_to(score[jp:jp + 1, :], (SUBLANES, TQ))
        bj, sj = divmod(jp, SUBLANES)
        for b in range(nb):
            strict = jnp.where(rival > parts[b], 1.0, 0.0)
            ties_too = jnp.where(rival >= parts[b], 1.0, 0.0)
            if b < bj:
                wins = strict
            elif b > bj:
                wins = ties_too
            else:
                wins = jnp.where(sub > sj, ties_too, strict)
            beaten[b] = beaten[b] + wins
    rank = jnp.concatenate(beaten, axis=0)
    chosen = jnp.where(valid, jnp.where(rank < n_top, 1.0, 0.0), 0.0)
    chosen_q = chosen.T
    sel_ref[0, 0] = chosen_q.astype(sel_ref.dtype)
    flag_ref[0, 0] = jnp.max(chosen_q, axis=0, keepdims=True)


def _select_matrix(n_cmp_pad, n_slc):
    rs = L_SLC // STRIDE_CMP
    rc = L_CMP // STRIDE_CMP
    m = np.zeros((n_cmp_pad, n_slc), np.float32)
    for j in range(n_slc):
        for a in range(rs):
            for b in range(rc):
                src = rs * j - a - b
                if 0 <= src < n_cmp_pad - 1:
                    m[src, j] += 1.0
    return jnp.asarray(m, dtype=BF16)


def _nsa_cmp(proj, k_cmp, v_cmp, slopes):
    s = proj.shape[0]
    n_cmp = k_cmp.shape[1]
    n_slc = s // L_SLC
    nq = s // TQ
    msel = _select_matrix(n_cmp, n_slc)
    qblk = (3 * SB_W) // GROUP_W
    return pl.pallas_call(
        functools.partial(_nsa_cmp_kernel, n_top=min(N_TOPK, n_slc)),
        grid=(NSA_KV, nq),
        in_specs=[pl.BlockSpec(memory_space=pltpu.SMEM),
                  pl.BlockSpec((TQ, GROUP_W), lambda g, i: (i, qblk + g)),
                  pl.BlockSpec((1, n_cmp, HEAD_DIM), lambda g, i: (g, 0, 0)),
                  pl.BlockSpec((1, n_cmp, HEAD_DIM), lambda g, i: (g, 0, 0)),
                  pl.BlockSpec((n_cmp, n_slc), lambda g, i: (0, 0))],
        out_specs=[pl.BlockSpec((TQ, GROUP_W), lambda g, i: (i, g)),
                   pl.BlockSpec((1, 1, TQ, n_slc), lambda g, i: (g, i, 0, 0)),
                   pl.BlockSpec((1, 1, 1, n_slc), lambda g, i: (g, i, 0, 0))],
        out_shape=[jax.ShapeDtypeStruct((s, NSA_W), F32),
                   jax.ShapeDtypeStruct((NSA_KV, nq, TQ, n_slc), BF16),
                   jax.ShapeDtypeStruct((NSA_KV, nq, 1, n_slc), F32)],
        scratch_shapes=[pltpu.VMEM((NSA_HPG * TQ, HEAD_DIM), BF16)],
        compiler_params=_cparams(("parallel", "parallel"), 40 << 20),
        name="nsa_cmp_select",
    )(slopes, proj, k_cmp, v_cmp, msel)


def _nsa_main_kernel(tiles_ref, count_ref, slopes_ref, q_ref, ks_ref, vs_ref, kw_ref, vw_ref, sel_ref, oc_ref,
                     gate_ref, o_ref, qs_ref, acc_s, m_s):
    g = pl.program_id(0)
    i = pl.program_id(1)
    n_kt = pl.num_programs(1) * TQ // NSA_TK
    _stack_heads(q_ref, qs_ref)
    slopes = [slopes_ref[g * NSA_HPG + h] for h in range(NSA_HPG)]
    t_rel = lax.broadcasted_iota(jnp.int32, (TQ, NSA_TK), 0)
    lane = lax.broadcasted_iota(jnp.int32, (TQ, NSA_TK), 1)
    diag = lax.div(i * TQ, NSA_TK)

    def scores(k_ref, v_ref, pieces, biases):
        z = jnp.concatenate([_dot_nt(qs_ref[...], k_ref[pl.ds(r0, n), :]) for r0, n in pieces], axis=1)
        v_aug = jnp.concatenate(
            [jnp.concatenate([v_ref[pl.ds(r0, n), :], jnp.ones((n, HEAD_DIM), BF16)], axis=1) for r0, n in pieces],
            axis=0)
        key_rel = jnp.concatenate(
            [(r0 - i * TQ + lax.broadcasted_iota(jnp.int32, (1, n), 1)).astype(F32) for r0, n in pieces], axis=1)
        bias = jnp.concatenate(biases, axis=1)
        return [z[h * TQ:(h + 1) * TQ] + (bias + slopes[h] * key_rel) for h in range(NSA_HPG)], v_aug

    def attend(k_ref, v_ref, pieces, biases, acc, m_ref, first):
        s_all, v_aug = scores(k_ref, v_ref, pieces, biases)
        width = s_all[0].shape[1]
        ps, alphas = [], []
        for h, s in enumerate(s_all):
            rows = slice(h * TQ, (h + 1) * TQ)
            m_tile = jnp.max(s, axis=-1, keepdims=True)
            if first:
                m_new = jnp.broadcast_to(m_tile, (TQ, LANES))
            else:
                m_old = m_ref[rows]
                m_new = jnp.maximum(m_old, m_tile)
                alphas.append(jnp.exp(m_old - m_new))
            m_ref[rows] = m_new
            ps.append(jnp.exp(s - jnp.concatenate([m_new] * (width // LANES), axis=1)).astype(BF16))
        pv = _dot(jnp.concatenate(ps, axis=0), v_aug)
        for h in range(NSA_HPG):
            rows = slice(h * TQ, (h + 1) * TQ)
            if first:
                acc[rows] = pv[rows]
            else:
                acc[rows] = acc[rows] * jnp.concatenate([alphas[h], alphas[h]], axis=1) + pv[rows]

    def tile_rows(kt):
        return pl.multiple_of(kt * NSA_TK, NSA_TK), NSA_TK

    def select_bias(kt, causal):
        n_slc = sel_ref.shape[3]
        blk = lax.broadcasted_iota(jnp.int32, (n_slc, NSA_TK), 0)
        key_blk = kt * (NSA_TK // L_SLC) + lax.shift_right_logical(
            lax.broadcasted_iota(jnp.int32, (n_slc, NSA_TK), 1), LOG2_L_SLC)
        expand = jnp.where(blk == key_blk, 1.0, 0.0).astype(BF16)
        picked = _dot(sel_ref[0, 0], expand)
        if causal:
            picked = jnp.where(kt * NSA_TK - i * TQ + lane <= t_rel, picked, 0.0)
        return jnp.where(picked > 0.5, 0.0, NEG_INF)

    attend(ks_ref, vs_ref, [tile_rows(diag)], [select_bias(diag, True)], acc_s, m_s, True)
    step = g * pl.num_programs(1) + i
    n_far = count_ref[step]

    def pair_body(p, _):
        ta = tiles_ref[step * n_kt + 2 * p]
        tb = tiles_ref[step * n_kt + 2 * p + 1]
        attend(ks_ref, vs_ref, [tile_rows(ta), tile_rows(tb)], [select_bias(ta, False), select_bias(tb, False)],
               acc_s, m_s, False)
        return 0

    lax.fori_loop(0, lax.shift_right_logical(n_far, 1), pair_body, 0)

    @pl.when(lax.rem(n_far, 2) == 1)
    def _():
        tl = tiles_ref[step * n_kt + n_far - 1]
        attend(ks_ref, vs_ref, [tile_rows(tl)], [select_bias(tl, False)], acc_s, m_s, False)

    wn = WINDOW + TQ
    w0 = pl.multiple_of(jnp.maximum(i * TQ - WINDOW, 0), TQ)
    dist = i * TQ + lax.broadcasted_iota(jnp.int32, (TQ, wn), 0) - (w0 + lax.broadcasted_iota(jnp.int32, (TQ, wn), 1))
    w_bias = jnp.where(dist >= 0, jnp.where(dist < WINDOW, 0.0, NEG_INF), NEG_INF)
    s_win, v_win = scores(kw_ref, vw_ref, [(w0, wn)], [w_bias])
    p_win = [jnp.exp(s - jnp.max(s, axis=-1, keepdims=True)).astype(BF16) for s in s_win]
    pv_win = _dot(jnp.concatenate(p_win, axis=0), v_win)

    gates = gate_ref[...]
    for h in range(NSA_HPG):
        cs = slice(h * HEAD_DIM, (h + 1) * HEAD_DIM)
        a_s = acc_s[h * TQ:(h + 1) * TQ]
        a_w = pv_win[h * TQ:(h + 1) * TQ]
        o_sel = a_s[:, :HEAD_DIM] / a_s[:, HEAD_DIM:HEAD_DIM + 1]
        o_win = a_w[:, :HEAD_DIM] / a_w[:, HEAD_DIM:HEAD_DIM + 1]
        o_ref[:, cs] = (gates[:, 3 * h:3 * h + 1] * oc_ref[:, cs] + gates[:, 3 * h + 1:3 * h + 2] * o_sel
                        + gates[:, 3 * h + 2:3 * h + 3] * o_win)


def _far_tile_lists(tile_any, s):
    nq = s // TQ
    n_kt = s // NSA_TK
    hit = tile_any.reshape(NSA_KV, nq, n_kt, NSA_TK // L_SLC).max(axis=-1) > 0.5
    kt = jnp.arange(n_kt, dtype=jnp.int32)
    diag = (jnp.arange(nq, dtype=jnp.int32) * TQ) // NSA_TK
    far = jnp.logical_and(hit, kt[None, None, :] < diag[None, :, None])
    tiles = jnp.sort(jnp.where(far, kt[None, None, :], n_kt), axis=-1)
    tiles = jnp.minimum(tiles, n_kt - 1)
    return tiles.reshape(-1), jnp.sum(far, axis=-1, dtype=jnp.int32).reshape(-1)


def _nsa_main(proj, sel, tile_any, o_c, gates, slopes):
    s = proj.shape[0]
    nq = s // TQ
    n_slc = s // L_SLC
    qblk = (3 * SB_W) // GROUP_W
    tiles, counts = _far_tile_lists(tile_any, s)
    kv_spec = lambda cb: pl.BlockSpec((s, HEAD_DIM), lambda g, i, t, c: (0, cb + g))
    rows = NSA_HPG * TQ
    grid_spec = pltpu.PrefetchScalarGridSpec(
        num_scalar_prefetch=2,
        grid=(NSA_KV, nq),
        in_specs=[pl.BlockSpec(memory_space=pltpu.SMEM),
                  pl.BlockSpec((TQ, GROUP_W), lambda g, i, t, c: (i, qblk + g)),
                  kv_spec(CB_KS), kv_spec(CB_VS), kv_spec(CB_KW), kv_spec(CB_VW),
                  pl.BlockSpec((1, 1, TQ, n_slc), lambda g, i, t, c: (g, i, 0, 0)),
                  pl.BlockSpec((TQ, GROUP_W), lambda g, i, t, c: (i, g)),
                  pl.BlockSpec((TQ, LANES), lambda g, i, t, c: (i, g))],
        out_specs=pl.BlockSpec((TQ, GROUP_W), lambda g, i, t, c: (i, g)),
        scratch_shapes=[pltpu.VMEM((rows, HEAD_DIM), BF16),
                        pltpu.VMEM((rows, 2 * HEAD_DIM), F32), pltpu.VMEM((rows, LANES), F32)],
    )
    vmem = 2 * 4 * s * HEAD_DIM * 2 + (24 << 20)
    return pl.pallas_call(
        _nsa_main_kernel,
        grid_spec=grid_spec,
        out_shape=jax.ShapeDtypeStruct((s, NSA_W), F32),
        compiler_params=_cparams(("parallel", "arbitrary"), vmem),
        name="nsa_select_window",
    )(tiles, counts, slopes, proj, proj, proj, proj, proj, sel, o_c, gates)


def kernel(x, attn_norm, w_in, pos_cmp_k, pos_cmp_v, w_cmp_k1, w_cmp_k2, w_cmp_v1, w_cmp_v2, norm_sb, norm_nsa,
           w_out, ffn_norm, w_gate, w_up, w_down, final_norm):
    batch, s, d_model = x.shape
    assert batch == 1 and s % 1024 == 0 and w_in.shape[2] == MAIN_COLS + GATE_COLS
    depth = w_in.shape[0]
    d_ff = w_gate.shape[2]
    assert d_ff % (2 * LANES) == 0 and (d_ff // 2) % LANES == 0
    scale = HEAD_DIM ** -0.5
    head_idx = jnp.arange(1, NSA_HEADS + 1, dtype=F32)
    slopes = 2.0 ** (-8.0 * head_idx / NSA_HEADS)
    col_scale = jnp.concatenate([jnp.full((SB_W,), scale, F32), jnp.ones((2 * SB_W,), F32),
                                 jnp.full((NSA_W,), scale, F32), jnp.ones((MAIN_COLS - 3 * SB_W - NSA_W,), F32)])
    col_scale = col_scale.reshape(1, MAIN_COLS)
    xs = x[0]
    w_in_t = jnp.swapaxes(w_in, 1, 2)
    for l in range(depth):
        w_g = w_in[l][:, MAIN_COLS:].reshape(d_model, NSA_KV, 3 * NSA_HPG)
        w_g = jnp.pad(w_g, ((0, 0), (0, 0), (0, LANES - 3 * NSA_HPG))).reshape(1, d_model, NSA_KV * LANES)

        h = _rmsnorm(xs, attn_norm[l], BF16)
        proj = _matmul_fullk(_mm_scale_nt_kernel, h, [w_in_t], l, MAIN_COLS, [(col_scale, "col")], BF16, 1024, 512,
                             "in_proj", transposed=True)
        gates = _matmul_fullk(_mm_sigmoid_kernel, h, [w_g], 0, NSA_KV * LANES, [], F32, 1024, NSA_KV * LANES,
                              "gate_proj")

        o_sb = _sb_attention(proj)

        def blocks_view(cb):
            cols = proj[:, cb * HEAD_DIM:(cb + NSA_KV) * HEAD_DIM]
            r = cols.reshape(s // STRIDE_CMP, STRIDE_CMP, NSA_KV, HEAD_DIM).transpose(2, 0, 1, 3)
            return r.reshape(NSA_KV, s // STRIDE_CMP, STRIDE_CMP * HEAD_DIM)

        k_cmp = _compress(blocks_view(CB_KC), pos_cmp_k[l], w_cmp_k1[l], w_cmp_k2[l])
        v_cmp = _compress(blocks_view(CB_VC), pos_cmp_v[l], w_cmp_v1[l], w_cmp_v2[l])
        o_c, sel, tile_any = _nsa_cmp(proj, k_cmp, v_cmp, slopes)
        o_nsa = _nsa_main(proj, sel, tile_any, o_c, gates, slopes)

        mixed = _rmsnorm_pair(o_sb, o_nsa, norm_sb[l], norm_nsa[l])
        xs = _matmul_fullk(_mm_res_kernel, mixed, [w_out], l, d_model, [(xs, "tile")], F32, 1024, 512, "out_proj")

        h = _rmsnorm(xs, ffn_norm[l], BF16)
        act = _matmul_fullk(_mm_swiglu_kernel, h, [w_gate, w_up], l, d_ff, [], BF16, 1024, 2 * LANES, "ffn_gate_up")
        xs = _matmul_acc_res(act, w_down[l].astype(BF16), xs, 1024, 512, d_ff // 2)
    return _rmsnorm(xs, final_norm, F32)[None]
```

```python
import functools

import numpy as np
import jax
import jax.numpy as jnp
from jax import lax
from jax.experimental import pallas as pl
from jax.experimental.pallas import tpu as pltpu

HEAD_DIM = 128
SB_HEADS = 16
NSA_HEADS = 16
NSA_KV = 2
NSA_HPG = NSA_HEADS // NSA_KV
L_CMP = 32
STRIDE_CMP = 16
CMP_HIDDEN = 256
L_SLC = 64
LOG2_L_SLC = 6
N_TOPK = 16
WINDOW = 512
EPS = 1e-6
NEG_INF = -1e30
FORCE_BONUS = 1e6

LANES = 128
SUBLANES = 8
TQ = 128
TK = 128
SB_WIN = 3
SB_QT = 2
NSA_TK = 256
VMEM_CAP = 56 * 1024 * 1024
SB_DEAD_LOG = -105.0

F32 = jnp.float32
BF16 = jnp.bfloat16

SB_W = SB_HEADS * HEAD_DIM
NSA_W = NSA_HEADS * HEAD_DIM
GROUP_W = NSA_HPG * HEAD_DIM
MAIN_COLS = 3 * SB_W + NSA_W + 3 * 2 * NSA_KV * HEAD_DIM
GATE_COLS = 3 * NSA_HEADS
CB_KC, CB_VC, CB_KS, CB_VS, CB_KW, CB_VW = 64, 66, 68, 70, 72, 74


def _cparams(sem, vmem_bytes):
    return pltpu.CompilerParams(dimension_semantics=sem, vmem_limit_bytes=int(min(vmem_bytes, VMEM_CAP)))


def _dot(a, b):
    return jnp.dot(a, b, preferred_element_type=F32)


def _dot_nt(a, b):
    return lax.dot_general(a, b, (((1,), (1,)), ((), ())), preferred_element_type=F32)


def _by_row_halves(dot_fn, a, b):
    half = a.shape[0] // 2
    return jnp.concatenate([dot_fn(a[:half], b), dot_fn(a[half:], b)], axis=0)


def _split_bf16(x):
    hi = x.astype(BF16)
    lo = (x - hi.astype(F32)).astype(BF16)
    return hi, lo


def _rms_kernel(x_ref, g_ref, o_ref):
    x = x_ref[...]
    ms = jnp.mean(x * x, axis=-1, keepdims=True)
    o_ref[...] = (x * lax.rsqrt(ms + EPS) * g_ref[...]).astype(o_ref.dtype)


def _rmsnorm(x, g, out_dtype, tm=256):
    m, d = x.shape
    return pl.pallas_call(
        _rms_kernel,
        grid=(m // tm,),
        in_specs=[pl.BlockSpec((tm, d), lambda i: (i, 0)), pl.BlockSpec((1, d), lambda i: (0, 0))],
        out_specs=pl.BlockSpec((tm, d), lambda i: (i, 0)),
        out_shape=jax.ShapeDtypeStruct((m, d), out_dtype),
        compiler_params=_cparams(("parallel",), 6 * tm * d * 4),
        name="rmsnorm",
    )(x, g.reshape(1, d))


def _rms2_kernel(a_ref, b_ref, ga_ref, gb_ref, o_ref):
    wa = a_ref.shape[1]
    for src, g_ref, off in ((a_ref, ga_ref, 0), (b_ref, gb_ref, wa)):
        x = src[...]
        ms = jnp.mean(x * x, axis=-1, keepdims=True)
        o_ref[:, off:off + x.shape[1]] = (x * lax.rsqrt(ms + EPS) * g_ref[...]).astype(o_ref.dtype)


def _rmsnorm_pair(a, b, ga, gb, tm=256):
    m, wa = a.shape
    wb = b.shape[1]
    return pl.pallas_call(
        _rms2_kernel,
        grid=(m // tm,),
        in_specs=[pl.BlockSpec((tm, wa), lambda i: (i, 0)), pl.BlockSpec((tm, wb), lambda i: (i, 0)),
                  pl.BlockSpec((1, wa), lambda i: (0, 0)), pl.BlockSpec((1, wb), lambda i: (0, 0))],
        out_specs=pl.BlockSpec((tm, wa + wb), lambda i: (i, 0)),
        out_shape=jax.ShapeDtypeStruct((m, wa + wb), BF16),
        compiler_params=_cparams(("parallel",), 6 * tm * (wa + wb) * 4),
        name="rmsnorm_pair",
    )(a, b, ga.reshape(1, wa), gb.reshape(1, wb))


def _mm_scale_nt_kernel(a_ref, bt_ref, cs_ref, o_ref):
    o_ref[...] = (_dot_nt(a_ref[...], bt_ref[...].astype(BF16)) * cs_ref[...]).astype(o_ref.dtype)


def _mm_sigmoid_kernel(a_ref, b_ref, o_ref):
    o_ref[...] = jax.nn.sigmoid(_dot(a_ref[...], b_ref[...].astype(BF16)))


def _mm_res_kernel(a_ref, b_ref, r_ref, o_ref):
    o_ref[...] = r_ref[...] + _dot(a_ref[...], b_ref[...].astype(BF16))


def _mm_swiglu_kernel(a_ref, bg_ref, bu_ref, o_ref):
    a = a_ref[...]
    gate = _dot(a, bg_ref[...].astype(BF16))
    up = _dot(a, bu_ref[...].astype(BF16))
    o_ref[...] = (jax.nn.silu(gate) * up).astype(o_ref.dtype)


def _mm_acc_res_kernel(a_ref, b_ref, r_ref, o_ref, acc_ref):
    k = pl.program_id(2)

    @pl.when(k == 0)
    def _():
        acc_ref[...] = r_ref[...]

    acc_ref[...] += _dot(a_ref[...], b_ref[...])

    @pl.when(k == pl.num_programs(2) - 1)
    def _():
        o_ref[...] = acc_ref[...]


def _matmul_fullk(kernel_fn, a, bs, layer, n, extras, out_dtype, tm, tn, name, transposed=False):
    m, k = a.shape
    in_specs = [pl.BlockSpec((tm, k), lambda i, j: (i, 0))]
    if transposed:
        in_specs += [pl.BlockSpec((None, tn, k), lambda i, j: (layer, j, 0)) for _ in bs]
    else:
        in_specs += [pl.BlockSpec((None, k, tn), lambda i, j: (layer, 0, j)) for _ in bs]
    w_bytes = bs[0].dtype.itemsize
    vmem = 2 * tm * k * 2 + len(bs) * k * tn * (2 * w_bytes + 2) + 2 * tm * tn * 4 + (2 + len(bs)) * tm * tn * 4
    args = [a, *bs]
    for arr, kind in extras:
        if kind == "col":
            in_specs.append(pl.BlockSpec((1, tn), lambda i, j: (0, j)))
        else:
            in_specs.append(pl.BlockSpec((tm, tn), lambda i, j: (i, j)))
            vmem += 2 * tm * tn * 4
        args.append(arr)
    return pl.pallas_call(
        kernel_fn,
        grid=(m // tm, n // tn),
        in_specs=in_specs,
        out_specs=pl.BlockSpec((tm, tn), lambda i, j: (i, j)),
        out_shape=jax.ShapeDtypeStruct((m, n), out_dtype),
        compiler_params=_cparams(("parallel", "parallel"), vmem + (4 << 20)),
        name=name,
    )(*args)


def _matmul_acc_res(a, b, res, tm, tn, tk):
    m, k = a.shape
    n = b.shape[1]
    vmem = 2 * tm * tk * 2 + 2 * tk * tn * 2 + 5 * tm * tn * 4 + tm * tn * 4
    return pl.pallas_call(
        _mm_acc_res_kernel,
        grid=(m // tm, n // tn, k // tk),
        in_specs=[pl.BlockSpec((tm, tk), lambda i, j, kk: (i, kk)),
                  pl.BlockSpec((tk, tn), lambda i, j, kk: (kk, j)),
                  pl.BlockSpec((tm, tn), lambda i, j, kk: (i, j))],
        out_specs=pl.BlockSpec((tm, tn), lambda i, j, kk: (i, j)),
        out_shape=jax.ShapeDtypeStruct((m, n), F32),
        scratch_shapes=[pltpu.VMEM((tm, tn), F32)],
        compiler_params=_cparams(("parallel", "parallel", "arbitrary"), vmem + (4 << 20)),
        name="matmul_acc_res",
    )(a, b, res)


def _sb_kernel(q_ref, k_ref, v_ref, u_ref, o_ref, carry_ref, acc_ref, *, hg):
    step = pl.program_id(1)
    u = u_ref[...]
    heads = [slice(h * HEAD_DIM, (h + 1) * HEAD_DIM) for h in range(hg)]

    def logs(z):
        log1p_e = jnp.log(1.0 + jnp.exp(-jnp.abs(z)))
        log_sig = jnp.minimum(z, 0.0) - log1p_e
        return log_sig, log_sig - z

    def tile_sums(log_not, n):
        hi, lo = _split_bf16(log_not)
        return [jnp.concatenate([hi[:, t * TK:(t + 1) * TK], lo[:, t * TK:(t + 1) * TK]], axis=1) for t in range(n)]

    def window(sub):
        qt = step * SB_QT + sub
        rows = slice(sub * TQ, (sub + 1) * TQ)
        base = jnp.maximum(qt - (SB_WIN - 1), 0)
        row0 = pl.multiple_of(base * TK, TK)
        wk = SB_WIN * TK
        key_pos = base * TK + lax.broadcasted_iota(jnp.int32, (TQ, wk), 1)
        visible = key_pos < qt * TQ + lax.broadcasted_iota(jnp.int32, (TQ, wk), 0)
        zs = [jnp.where(visible, _dot_nt(q_ref[rows, cs], k_ref[pl.ds(row0, wk), cs]), NEG_INF) for cs in heads]
        lg = [logs(z) for z in zs]
        lhs = [part for (_, log_not) in lg for part in tile_sums(log_not, SB_WIN)]
        sums = _by_row_halves(_dot, jnp.concatenate(lhs, axis=0), u)
        worst = None
        for h, cs in enumerate(heads):
            carry = None
            shifted = [None] * SB_WIN
            for t in reversed(range(SB_WIN)):
                blk = sums[(h * SB_WIN + t) * TQ:(h * SB_WIN + t + 1) * TQ]
                tail, total = blk[:, :TK], blk[:, TK:]
                shifted[t] = tail if carry is None else tail + carry
                carry = total if carry is None else carry + total
            a = jnp.exp(lg[h][0] + jnp.concatenate(shifted, axis=1))
            acc_ref[sub * hg + h] = _dot(a.astype(BF16), v_ref[pl.ds(row0, wk), cs])
            carry_ref[sub * hg + h] = carry
            worst = carry if worst is None else jnp.maximum(worst, carry)
        return base, jnp.max(worst)

    def far_sweep(sub, base, worst):
        rows = slice(sub * TQ, (sub + 1) * TQ)

        def tile(kb):
            r0 = pl.multiple_of(kb * TK, TK)
            lg1 = [logs(_dot_nt(q_ref[rows, cs], k_ref[pl.ds(r0, TK), cs])) for cs in heads]
            sums1 = _dot(jnp.concatenate([tile_sums(log_not, 1)[0] for (_, log_not) in lg1], axis=0), u)
            far = None
            for h, cs in enumerate(heads):
                blk = sums1[h * TQ:(h + 1) * TQ]
                carry = carry_ref[sub * hg + h]
                a = jnp.exp(lg1[h][0] + blk[:, :TK] + carry)
                acc_ref[sub * hg + h] += _dot(a.astype(BF16), v_ref[pl.ds(r0, TK), cs])
                carry = carry + blk[:, TK:]
                carry_ref[sub * hg + h] = carry
                far = carry if far is None else jnp.maximum(far, carry)
            return jnp.max(far)

        def cond(c):
            kb, far = c
            return jnp.logical_and(kb >= 0, far > SB_DEAD_LOG)

        def body(c):
            kb, _ = c
            return kb - 1, tile(kb)

        lax.while_loop(cond, body, (base - 1, worst))

    starts = [window(sub) for sub in range(SB_QT)]
    for sub, (base, worst) in enumerate(starts):
        far_sweep(sub, base, worst)
    for sub in range(SB_QT):
        for h, cs in enumerate(heads):
            o_ref[sub * TQ:(sub + 1) * TQ, cs] = acc_ref[sub * hg + h]


def _sb_attention(proj, hg=4):
    s = proj.shape[0]
    w = hg * HEAD_DIM
    nblk = SB_W // w
    tri = np.arange(TK)[:, None] > np.arange(TK)[None, :]
    half = np.concatenate([tri, np.ones((TK, TK), bool)], axis=1)
    u = jnp.asarray(np.concatenate([half, half], axis=0), dtype=BF16)
    tq = SB_QT * TQ
    vmem = 2 * 2 * s * w * 2 + 2 * tq * w * 2 + 2 * tq * w * 4 + 2 * SB_QT * hg * TQ * TK * 4 + (16 << 20)
    return pl.pallas_call(
        functools.partial(_sb_kernel, hg=hg),
        grid=(nblk, s // tq),
        in_specs=[pl.BlockSpec((tq, w), lambda g, i: (i, g)),
                  pl.BlockSpec((s, w), lambda g, i: (0, nblk + g)),
                  pl.BlockSpec((s, w), lambda g, i: (0, 2 * nblk + g)),
                  pl.BlockSpec((2 * TK, 2 * TK), lambda g, i: (0, 0))],
        out_specs=pl.BlockSpec((tq, w), lambda g, i: (i, g)),
        out_shape=jax.ShapeDtypeStruct((s, SB_W), F32),
        scratch_shapes=[pltpu.VMEM((SB_QT * hg, TQ, TK), F32), pltpu.VMEM((SB_QT * hg, TQ, HEAD_DIM), F32)],
        compiler_params=_cparams(("parallel", "arbitrary"), vmem),
        name="sb_attention",
    )(proj, proj, proj, u)


def _compress_kernel(r_ref, pos_ref, w1_ref, w2_ref, o_ref):
    r = r_ref[0]
    half = r.shape[1]
    first = _dot(r, w1_ref[:half, :])
    second = _dot(r, w1_ref[half:, :])
    second = pltpu.roll(second, second.shape[0] - 1, 0)
    pos_term = _dot(pos_ref[...].astype(BF16), w1_ref[...])[0:1, :]
    hidden = jax.nn.gelu(first + second + pos_term)
    o_ref[0] = _dot(hidden.astype(BF16), w2_ref[...]).astype(o_ref.dtype)


def _compress(r, pos, w1, w2):
    g, n, half = r.shape
    pos_rows = jnp.zeros((SUBLANES, 2 * half), F32).at[0].set(pos.reshape(-1))
    return pl.pallas_call(
        _compress_kernel,
        grid=(g,),
        in_specs=[pl.BlockSpec((1, n, half), lambda i: (i, 0, 0)),
                  pl.BlockSpec((SUBLANES, 2 * half), lambda i: (0, 0)),
                  pl.BlockSpec((2 * half, CMP_HIDDEN), lambda i: (0, 0)),
                  pl.BlockSpec((CMP_HIDDEN, HEAD_DIM), lambda i: (0, 0))],
        out_specs=pl.BlockSpec((1, n, HEAD_DIM), lambda i: (i, 0, 0)),
        out_shape=jax.ShapeDtypeStruct((g, n, HEAD_DIM), BF16),
        compiler_params=_cparams(("parallel",), 24 << 20),
        name="nsa_compress",
    )(r, pos_rows, w1.astype(BF16), w2.astype(BF16))


def _stack_heads(q_ref, qs_ref):
    for h in range(NSA_HPG):
        qs_ref[h * TQ:(h + 1) * TQ, :] = q_ref[:, h * HEAD_DIM:(h + 1) * HEAD_DIM]


def _nsa_cmp_kernel(slopes_ref, q_ref, kc_ref, vc_ref, msel_ref, oc_ref, sel_ref, flag_ref, qs_ref, *, n_top):
    g = pl.program_id(0)
    i = pl.program_id(1)
    kc = kc_ref[0]
    vc = vc_ref[0]
    n_cmp = kc.shape[0]
    n_slc = msel_ref.shape[1]
    t_rel = lax.broadcasted_iota(jnp.int32, (TQ, n_cmp), 0)
    end_rel = lax.broadcasted_iota(jnp.int32, (TQ, n_cmp), 1) * STRIDE_CMP + (L_CMP - 1) - i * TQ
    bias = jnp.where(end_rel <= t_rel, 0.0, NEG_INF)
    end_row = end_rel[0:1, :].astype(F32)
    t_col = i * TQ + lax.broadcasted_iota(jnp.int32, (TQ, 1), 0)
    row_valid = jnp.where(t_col >= L_CMP - 1, 1.0, 0.0)
    _stack_heads(q_ref, qs_ref)
    z = _by_row_halves(_dot_nt, qs_ref[...], kc)
    p_sum = jnp.zeros((TQ, n_cmp), F32)
    ps = []
    for h in range(NSA_HPG):
        s = z[h * TQ:(h + 1) * TQ] + (bias + slopes_ref[g * NSA_HPG + h] * end_row)
        m = jnp.max(s, axis=-1, keepdims=True)
        e = jnp.exp(s - m)
        p = e * (row_valid / jnp.sum(e, axis=-1, keepdims=True))
        ps.append(p.astype(BF16))
        p_sum = p_sum + p
    oc = _by_row_halves(_dot, jnp.concatenate(ps, axis=0), vc)
    for h in range(NSA_HPG):
        oc_ref[:, h * HEAD_DIM:(h + 1) * HEAD_DIM] = oc[h * TQ:(h + 1) * TQ]
    hi, lo = _split_bf16(p_sum)
    msel = msel_ref[...]
    imp = _dot(hi, msel) + _dot(lo, msel)
    score_src = imp.T
    blk = lax.broadcasted_iota(jnp.int32, (n_slc, TQ), 0)
    cur = lax.shift_right_logical(i * TQ + lax.broadcasted_iota(jnp.int32, (n_slc, TQ), 1), LOG2_L_SLC)
    valid = blk <= cur
    bonus = jnp.where(blk == 0, FORCE_BONUS, 0.0)
    bonus = jnp.where(blk == cur, FORCE_BONUS, bonus)
    bonus = jnp.where(blk == cur - 1, FORCE_BONUS, bonus)
    score = jnp.where(valid, score_src + bonus, NEG_INF)
    blk_f = blk.astype(F32)
    taken = jnp.zeros((n_slc, TQ), F32)
    for _ in range(n_top):
        best = jnp.max(score, axis=0, keepdims=True)
        first = jnp.min(jnp.where(score == best, blk_f, float(n_slc)), axis=0, keepdims=True)
        hit = blk_f == first
        taken = jnp.where(hit, 1.0, taken)
        score = jnp.where(hit, -jnp.inf, score)
    chosen = jnp.where(valid, taken, 0.0)
    chosen_q = chosen.T
    sel_ref[0, 0] = chosen_q.astype(sel_ref.dtype)
    flag_ref[0, 0] = jnp.max(chosen_q, axis=0, keepdims=True)


def _select_matrix(n_cmp_pad, n_slc):
    rs = L_SLC // STRIDE_CMP
    rc = L_CMP // STRIDE_CMP
    m = np.zeros((n_cmp_pad, n_slc), np.float32)
    for j in range(n_slc):
        for a in range(rs):
            for b in range(rc):
                src = rs * j - a - b
                if 0 <= src < n_cmp_pad - 1:
                    m[src, j] += 1.0
    return jnp.asarray(m, dtype=BF16)


def _nsa_cmp(proj, k_cmp, v_cmp, slopes):
    s = proj.shape[0]
    n_cmp = k_cmp.shape[1]
    n_slc = s // L_SLC
    nq = s // TQ
    msel = _select_matrix(n_cmp, n_slc)
    qblk = (3 * SB_W) // GROUP_W
    return pl.pallas_call(
        functools.partial(_nsa_cmp_kernel, n_top=min(N_TOPK, n_slc)),
        grid=(NSA_KV, nq),
        in_specs=[pl.BlockSpec(memory_space=pltpu.SMEM),
                  pl.BlockSpec((TQ, GROUP_W), lambda g, i: (i, qblk + g)),
                  pl.BlockSpec((1, n_cmp, HEAD_DIM), lambda g, i: (g, 0, 0)),
                  pl.BlockSpec((1, n_cmp, HEAD_DIM), lambda g, i: (g, 0, 0)),
                  pl.BlockSpec((n_cmp, n_slc), lambda g, i: (0, 0))],
        out_specs=[pl.BlockSpec((TQ, GROUP_W), lambda g, i: (i, g)),
                   pl.BlockSpec((1, 1, TQ, n_slc), lambda g, i: (g, i, 0, 0)),
                   pl.BlockSpec((1, 1, 1, n_slc), lambda g, i: (g, i, 0, 0))],
        out_shape=[jax.ShapeDtypeStruct((s, NSA_W), F32),
                   jax.ShapeDtypeStruct((NSA_KV, nq, TQ, n_slc), BF16),
                   jax.ShapeDtypeStruct((NSA_KV, nq, 1, n_slc), F32)],
        scratch_shapes=[pltpu.VMEM((NSA_HPG * TQ, HEAD_DIM), BF16)],
        compiler_params=_cparams(("parallel", "parallel"), 40 << 20),
        name="nsa_cmp_select",
    )(slopes, proj, k_cmp, v_cmp, msel)


def _nsa_main_kernel(tiles_ref, count_ref, slopes_ref, q_ref, ks_ref, vs_ref, kw_ref, vw_ref, sel_ref, oc_ref,
                     gate_ref, o_ref, qs_ref, acc_s, m_s):
    g = pl.program_id(0)
    i = pl.program_id(1)
    n_kt = pl.num_programs(1) * TQ // NSA_TK
    _stack_heads(q_ref, qs_ref)
    slopes = [slopes_ref[g * NSA_HPG + h] for h in range(NSA_HPG)]
    t_rel = lax.broadcasted_iota(jnp.int32, (TQ, NSA_TK), 0)
    lane = lax.broadcasted_iota(jnp.int32, (TQ, NSA_TK), 1)
    diag = lax.div(i * TQ, NSA_TK)

    def scores(k_ref, v_ref, pieces, biases):
        z = jnp.concatenate([_by_row_halves(_dot_nt, qs_ref[...], k_ref[pl.ds(r0, n), :]) for r0, n in pieces],
                            axis=1)
        v_aug = jnp.concatenate(
            [jnp.concatenate([v_ref[pl.ds(r0, n), :], jnp.ones((n, HEAD_DIM), BF16)], axis=1) for r0, n in pieces],
            axis=0)
        key_rel = jnp.concatenate(
            [(r0 - i * TQ + lax.broadcasted_iota(jnp.int32, (1, n), 1)).astype(F32) for r0, n in pieces], axis=1)
        bias = jnp.concatenate(biases, axis=1)
        return [z[h * TQ:(h + 1) * TQ] + (bias + slopes[h] * key_rel) for h in range(NSA_HPG)], v_aug

    def attend(k_ref, v_ref, pieces, biases, acc, m_ref, first):
        s_all, v_aug = scores(k_ref, v_ref, pieces, biases)
        width = s_all[0].shape[1]
        ps, alphas = [], []
        for h, s in enumerate(s_all):
            rows = slice(h * TQ, (h + 1) * TQ)
            m_tile = jnp.max(s, axis=-1, keepdims=True)
            if first:
                m_new = jnp.broadcast_to(m_tile, (TQ, LANES))
            else:
                m_old = m_ref[rows]
                m_new = jnp.maximum(m_old, m_tile)
                alphas.append(jnp.exp(m_old - m_new))
            m_ref[rows] = m_new
            ps.append(jnp.exp(s - jnp.concatenate([m_new] * (width // LANES), axis=1)).astype(BF16))
        pv = _by_row_halves(_dot, jnp.concatenate(ps, axis=0), v_aug)
        for h in range(NSA_HPG):
            rows = slice(h * TQ, (h + 1) * TQ)
            if first:
                acc[rows] = pv[rows]
            else:
                acc[rows] = acc[rows] * jnp.concatenate([alphas[h], alphas[h]], axis=1) + pv[rows]

    def tile_rows(kt):
        return pl.multiple_of(kt * NSA_TK, NSA_TK), NSA_TK

    def select_bias(kt, causal):
        n_slc = sel_ref.shape[3]
        blk = lax.broadcasted_iota(jnp.int32, (n_slc, NSA_TK), 0)
        key_blk = kt * (NSA_TK // L_SLC) + lax.shift_right_logical(
            lax.broadcasted_iota(jnp.int32, (n_slc, NSA_TK), 1), LOG2_L_SLC)
        expand = jnp.where(blk == key_blk, 1.0, 0.0).astype(BF16)
        picked = _dot(sel_ref[0, 0], expand)
        if causal:
            picked = jnp.where(kt * NSA_TK - i * TQ + lane <= t_rel, picked, 0.0)
        return jnp.where(picked > 0.5, 0.0, NEG_INF)

    attend(ks_ref, vs_ref, [tile_rows(diag)], [select_bias(diag, True)], acc_s, m_s, True)
    step = g * pl.num_programs(1) + i
    n_far = count_ref[step]

    def pair_body(p, _):
        ta = tiles_ref[step * n_kt + 2 * p]
        tb = tiles_ref[step * n_kt + 2 * p + 1]
        attend(ks_ref, vs_ref, [tile_rows(ta), tile_rows(tb)], [select_bias(ta, False), select_bias(tb, False)],
               acc_s, m_s, False)
        return 0

    lax.fori_loop(0, lax.shift_right_logical(n_far, 1), pair_body, 0)

    @pl.when(lax.rem(n_far, 2) == 1)
    def _():
        tl = tiles_ref[step * n_kt + n_far - 1]
        attend(ks_ref, vs_ref, [tile_rows(tl)], [select_bias(tl, False)], acc_s, m_s, False)

    wn = WINDOW + TQ
    w0 = pl.multiple_of(jnp.maximum(i * TQ - WINDOW, 0), TQ)
    dist = i * TQ + lax.broadcasted_iota(jnp.int32, (TQ, wn), 0) - (w0 + lax.broadcasted_iota(jnp.int32, (TQ, wn), 1))
    w_bias = jnp.where(dist >= 0, jnp.where(dist < WINDOW, 0.0, NEG_INF), NEG_INF)
    s_win, v_win = scores(kw_ref, vw_ref, [(w0, wn)], [w_bias])
    p_win = [jnp.exp(s - jnp.max(s, axis=-1, keepdims=True)).astype(BF16) for s in s_win]
    pv_win = _by_row_halves(_dot, jnp.concatenate(p_win, axis=0), v_win)

    gates = gate_ref[...]
    for h in range(NSA_HPG):
        cs = slice(h * HEAD_DIM, (h + 1) * HEAD_DIM)
        a_s = acc_s[h * TQ:(h + 1) * TQ]
        a_w = pv_win[h * TQ:(h + 1) * TQ]
        o_sel = a_s[:, :HEAD_DIM] / a_s[:, HEAD_DIM:HEAD_DIM + 1]
        o_win = a_w[:, :HEAD_DIM] / a_w[:, HEAD_DIM:HEAD_DIM + 1]
        o_ref[:, cs] = (gates[:, 3 * h:3 * h + 1] * oc_ref[:, cs] + gates[:, 3 * h + 1:3 * h + 2] * o_sel
                        + gates[:, 3 * h + 2:3 * h + 3] * o_win)


def _far_tile_lists(tile_any, s):
    nq = s // TQ
    n_kt = s // NSA_TK
    hit = tile_any.reshape(NSA_KV, nq, n_kt, NSA_TK // L_SLC).max(axis=-1) > 0.5
    kt = jnp.arange(n_kt, dtype=jnp.int32)
    diag = (jnp.arange(nq, dtype=jnp.int32) * TQ) // NSA_TK
    far = jnp.logical_and(hit, kt[None, None, :] < diag[None, :, None])
    tiles = jnp.sort(jnp.where(far, kt[None, None, :], n_kt), axis=-1)
    tiles = jnp.minimum(tiles, n_kt - 1)
    return tiles.reshape(-1), jnp.sum(far, axis=-1, dtype=jnp.int32).reshape(-1)


def _nsa_main(proj, sel, tile_any, o_c, gates, slopes):
    s = proj.shape[0]
    nq = s // TQ
    n_slc = s // L_SLC
    qblk = (3 * SB_W) // GROUP_W
    tiles, counts = _far_tile_lists(tile_any, s)
    kv_spec = lambda cb: pl.BlockSpec((s, HEAD_DIM), lambda g, i, t, c: (0, cb + g))
    rows = NSA_HPG * TQ
    grid_spec = pltpu.PrefetchScalarGridSpec(
        num_scalar_prefetch=2,
        grid=(NSA_KV, nq),
        in_specs=[pl.BlockSpec(memory_space=pltpu.SMEM),
                  pl.BlockSpec((TQ, GROUP_W), lambda g, i, t, c: (i, qblk + g)),
                  kv_spec(CB_KS), kv_spec(CB_VS), kv_spec(CB_KW), kv_spec(CB_VW),
                  pl.BlockSpec((1, 1, TQ, n_slc), lambda g, i, t, c: (g, i, 0, 0)),
                  pl.BlockSpec((TQ, GROUP_W), lambda g, i, t, c: (i, g)),
                  pl.BlockSpec((TQ, LANES), lambda g, i, t, c: (i, g))],
        out_specs=pl.BlockSpec((TQ, GROUP_W), lambda g, i, t, c: (i, g)),
        scratch_shapes=[pltpu.VMEM((rows, HEAD_DIM), BF16),
                        pltpu.VMEM((rows, 2 * HEAD_DIM), F32), pltpu.VMEM((rows, LANES), F32)],
    )
    vmem = 2 * 4 * s * HEAD_DIM * 2 + (24 << 20)
    return pl.pallas_call(
        _nsa_main_kernel,
        grid_spec=grid_spec,
        out_shape=jax.ShapeDtypeStruct((s, NSA_W), F32),
        compiler_params=_cparams(("parallel", "arbitrary"), vmem),
        name="nsa_select_window",
    )(tiles, counts, slopes, proj, proj, proj, proj, proj, sel, o_c, gates)


def kernel(x, attn_norm, w_in, pos_cmp_k, pos_cmp_v, w_cmp_k1, w_cmp_k2, w_cmp_v1, w_cmp_v2, norm_sb, norm_nsa,
           w_out, ffn_norm, w_gate, w_up, w_down, final_norm):
    batch, s, d_model = x.shape
    assert batch == 1 and s % 1024 == 0 and w_in.shape[2] == MAIN_COLS + GATE_COLS
    depth = w_in.shape[0]
    d_ff = w_gate.shape[2]
    assert d_ff % (2 * LANES) == 0 and (d_ff // 2) % LANES == 0
    scale = HEAD_DIM ** -0.5
    head_idx = jnp.arange(1, NSA_HEADS + 1, dtype=F32)
    slopes = 2.0 ** (-8.0 * head_idx / NSA_HEADS)
    col_scale = jnp.concatenate([jnp.full((SB_W,), scale, F32), jnp.ones((2 * SB_W,), F32),
                                 jnp.full((NSA_W,), scale, F32), jnp.ones((MAIN_COLS - 3 * SB_W - NSA_W,), F32)])
    col_scale = col_scale.reshape(1, MAIN_COLS)
    xs = x[0]
    w_in_t = jnp.swapaxes(w_in, 1, 2)
    for l in range(depth):
        w_g = w_in[l][:, MAIN_COLS:].reshape(d_model, NSA_KV, 3 * NSA_HPG)
        w_g = jnp.pad(w_g, ((0, 0), (0, 0), (0, LANES - 3 * NSA_HPG))).reshape(1, d_model, NSA_KV * LANES)

        h = _rmsnorm(xs, attn_norm[l], BF16)
        proj = _matmul_fullk(_mm_scale_nt_kernel, h, [w_in_t], l, MAIN_COLS, [(col_scale, "col")], BF16, 1024, 512,
                             "in_proj", transposed=True)
        gates = _matmul_fullk(_mm_sigmoid_kernel, h, [w_g], 0, NSA_KV * LANES, [], F32, 1024, NSA_KV * LANES,
                              "gate_proj")

        o_sb = _sb_attention(proj)

        def blocks_view(cb):
            cols = proj[:, cb * HEAD_DIM:(cb + NSA_KV) * HEAD_DIM]
            r = cols.reshape(s // STRIDE_CMP, STRIDE_CMP, NSA_KV, HEAD_DIM).transpose(2, 0, 1, 3)
            return r.reshape(NSA_KV, s // STRIDE_CMP, STRIDE_CMP * HEAD_DIM)

        k_cmp = _compress(blocks_view(CB_KC), pos_cmp_k[l], w_cmp_k1[l], w_cmp_k2[l])
        v_cmp = _compress(blocks_view(CB_VC), pos_cmp_v[l], w_cmp_v1[l], w_cmp_v2[l])
        o_c, sel, tile_any = _nsa_cmp(proj, k_cmp, v_cmp, slopes)
        o_nsa = _nsa_main(proj, sel, tile_any, o_c, gates, slopes)

        mixed = _rmsnorm_pair(o_sb, o_nsa, norm_sb[l], norm_nsa[l])
        xs = _matmul_fullk(_mm_res_kernel, mixed, [w_out], l, d_model, [(xs, "tile")], F32, 1024, 512, "out_proj")

        h = _rmsnorm(xs, ffn_norm[l], BF16)
        act = _matmul_fullk(_mm_swiglu_kernel, h, [w_gate, w_up], l, d_ff, [], BF16, 1024, 2 * LANES, "ffn_gate_up")
        xs = _matmul_acc_res(act, w_down[l].astype(BF16), xs, 1024, 512, d_ff // 2)
    return _rmsnorm(xs, final_norm, F32)[None]
```

```python
import functools

import numpy as np
import jax
import jax.numpy as jnp
from jax import lax
from jax.experimental import pallas as pl
from jax.experimental.pallas import tpu as pltpu

HEAD_DIM = 128
SB_HEADS = 16
NSA_HEADS = 16
NSA_KV = 2
NSA_HPG = NSA_HEADS // NSA_KV
L_CMP = 32
STRIDE_CMP = 16
CMP_HIDDEN = 256
L_SLC = 64
LOG2_L_SLC = 6
N_TOPK = 16
WINDOW = 512
EPS = 1e-6
NEG_INF = -1e30
FORCE_BONUS = 1e6

LANES = 128
SUBLANES = 8
TQ = 128
TK = 128
SB_WIN = 3
SB_QT = 2
NSA_TK = 256
VMEM_CAP = 56 * 1024 * 1024
SB_DEAD_LOG = -105.0

F32 = jnp.float32
BF16 = jnp.bfloat16

SB_W = SB_HEADS * HEAD_DIM
NSA_W = NSA_HEADS * HEAD_DIM
GROUP_W = NSA_HPG * HEAD_DIM
MAIN_COLS = 3 * SB_W + NSA_W + 3 * 2 * NSA_KV * HEAD_DIM
GATE_COLS = 3 * NSA_HEADS
CB_KC, CB_VC, CB_KS, CB_VS, CB_KW, CB_VW = 64, 66, 68, 70, 72, 74


def _cparams(sem, vmem_bytes):
    return pltpu.CompilerParams(dimension_semantics=sem, vmem_limit_bytes=int(min(vmem_bytes, VMEM_CAP)))


def _dot(a, b):
    return jnp.dot(a, b, preferred_element_type=F32)


def _dot_nt(a, b):
    return lax.dot_general(a, b, (((1,), (1,)), ((), ())), preferred_element_type=F32)


def _by_row_halves(dot_fn, a, b):
    half = a.shape[0] // 2
    return jnp.concatenate([dot_fn(a[:half], b), dot_fn(a[half:], b)], axis=0)


def _split_bf16(x):
    hi = x.astype(BF16)
    lo = (x - hi.astype(F32)).astype(BF16)
    return hi, lo


def _rms_kernel(x_ref, g_ref, o_ref):
    x = x_ref[...]
    ms = jnp.mean(x * x, axis=-1, keepdims=True)
    o_ref[...] = (x * lax.rsqrt(ms + EPS) * g_ref[...]).astype(o_ref.dtype)


def _rmsnorm(x, g, out_dtype, tm=256):
    m, d = x.shape
    return pl.pallas_call(
        _rms_kernel,
        grid=(m // tm,),
        in_specs=[pl.BlockSpec((tm, d), lambda i: (i, 0)), pl.BlockSpec((1, d), lambda i: (0, 0))],
        out_specs=pl.BlockSpec((tm, d), lambda i: (i, 0)),
        out_shape=jax.ShapeDtypeStruct((m, d), out_dtype),
        compiler_params=_cparams(("parallel",), 6 * tm * d * 4),
        name="rmsnorm",
    )(x, g.reshape(1, d))


def _rms2_kernel(a_ref, b_ref, ga_ref, gb_ref, o_ref):
    wa = a_ref.shape[1]
    for src, g_ref, off in ((a_ref, ga_ref, 0), (b_ref, gb_ref, wa)):
        x = src[...]
        ms = jnp.mean(x * x, axis=-1, keepdims=True)
        o_ref[:, off:off + x.shape[1]] = (x * lax.rsqrt(ms + EPS) * g_ref[...]).astype(o_ref.dtype)


def _rmsnorm_pair(a, b, ga, gb, tm=256):
    m, wa = a.shape
    wb = b.shape[1]
    return pl.pallas_call(
        _rms2_kernel,
        grid=(m // tm,),
        in_specs=[pl.BlockSpec((tm, wa), lambda i: (i, 0)), pl.BlockSpec((tm, wb), lambda i: (i, 0)),
                  pl.BlockSpec((1, wa), lambda i: (0, 0)), pl.BlockSpec((1, wb), lambda i: (0, 0))],
        out_specs=pl.BlockSpec((tm, wa + wb), lambda i: (i, 0)),
        out_shape=jax.ShapeDtypeStruct((m, wa + wb), BF16),
        compiler_params=_cparams(("parallel",), 6 * tm * (wa + wb) * 4),
        name="rmsnorm_pair",
    )(a, b, ga.reshape(1, wa), gb.reshape(1, wb))


def _mm_scale_nt_kernel(a_ref, bt_ref, cs_ref, o_ref):
    o_ref[...] = (_dot_nt(a_ref[...], bt_ref[...].astype(BF16)) * cs_ref[...]).astype(o_ref.dtype)


def _mm_sigmoid_kernel(a_ref, b_ref, o_ref):
    o_ref[...] = jax.nn.sigmoid(_dot(a_ref[...], b_ref[...].astype(BF16)))


def _mm_res_kernel(a_ref, b_ref, r_ref, o_ref):
    o_ref[...] = r_ref[...] + _dot(a_ref[...], b_ref[...].astype(BF16))


def _mm_swiglu_kernel(a_ref, bg_ref, bu_ref, o_ref):
    a = a_ref[...]
    gate = _dot(a, bg_ref[...].astype(BF16))
    up = _dot(a, bu_ref[...].astype(BF16))
    o_ref[...] = (jax.nn.silu(gate) * up).astype(o_ref.dtype)


def _mm_acc_res_kernel(a_ref, b_ref, r_ref, o_ref, acc_ref):
    k = pl.program_id(2)

    @pl.when(k == 0)
    def _():
        acc_ref[...] = r_ref[...]

    acc_ref[...] += _dot(a_ref[...], b_ref[...].astype(BF16))

    @pl.when(k == pl.num_programs(2) - 1)
    def _():
        o_ref[...] = acc_ref[...]


def _matmul_fullk(kernel_fn, a, bs, layer, n, extras, out_dtype, tm, tn, name, transposed=False):
    m, k = a.shape
    in_specs = [pl.BlockSpec((tm, k), lambda i, j: (i, 0))]
    if transposed:
        in_specs += [pl.BlockSpec((None, tn, k), lambda i, j: (layer, j, 0)) for _ in bs]
    else:
        in_specs += [pl.BlockSpec((None, k, tn), lambda i, j: (layer, 0, j)) for _ in bs]
    w_bytes = bs[0].dtype.itemsize
    vmem = 2 * tm * k * 2 + len(bs) * k * tn * (2 * w_bytes + 2) + 2 * tm * tn * 4 + (2 + len(bs)) * tm * tn * 4
    args = [a, *bs]
    for arr, kind in extras:
        if kind == "col":
            in_specs.append(pl.BlockSpec((1, tn), lambda i, j: (0, j)))
        else:
            in_specs.append(pl.BlockSpec((tm, tn), lambda i, j: (i, j)))
            vmem += 2 * tm * tn * 4
        args.append(arr)
    return pl.pallas_call(
        kernel_fn,
        grid=(m // tm, n // tn),
        in_specs=in_specs,
        out_specs=pl.BlockSpec((tm, tn), lambda i, j: (i, j)),
        out_shape=jax.ShapeDtypeStruct((m, n), out_dtype),
        compiler_params=_cparams(("parallel", "parallel"), vmem + (4 << 20)),
        name=name,
    )(*args)


def _matmul_acc_res(a, b, layer, res, tm, tn, tk):
    m, k = a.shape
    n = b.shape[2]
    vmem = 2 * tm * tk * 2 + tk * tn * (2 * b.dtype.itemsize + 2) + 5 * tm * tn * 4 + tm * tn * 4
    return pl.pallas_call(
        _mm_acc_res_kernel,
        grid=(m // tm, n // tn, k // tk),
        in_specs=[pl.BlockSpec((tm, tk), lambda i, j, kk: (i, kk)),
                  pl.BlockSpec((None, tk, tn), lambda i, j, kk: (layer, kk, j)),
                  pl.BlockSpec((tm, tn), lambda i, j, kk: (i, j))],
        out_specs=pl.BlockSpec((tm, tn), lambda i, j, kk: (i, j)),
        out_shape=jax.ShapeDtypeStruct((m, n), F32),
        scratch_shapes=[pltpu.VMEM((tm, tn), F32)],
        compiler_params=_cparams(("parallel", "parallel", "arbitrary"), vmem + (4 << 20)),
        name="matmul_acc_res",
    )(a, b, res)


def _sb_kernel(q_ref, k_ref, v_ref, u_ref, o_ref, carry_ref, acc_ref, *, hg):
    step = pl.program_id(1)
    u = u_ref[...]
    heads = [slice(h * HEAD_DIM, (h + 1) * HEAD_DIM) for h in range(hg)]

    def logs(z):
        log1p_e = jnp.log(1.0 + jnp.exp(-jnp.abs(z)))
        log_sig = jnp.minimum(z, 0.0) - log1p_e
        return log_sig, log_sig - z

    def tile_sums(log_not, n):
        hi, lo = _split_bf16(log_not)
        return [jnp.concatenate([hi[:, t * TK:(t + 1) * TK], lo[:, t * TK:(t + 1) * TK]], axis=1) for t in range(n)]

    def window(sub):
        qt = step * SB_QT + sub
        rows = slice(sub * TQ, (sub + 1) * TQ)
        base = jnp.maximum(qt - (SB_WIN - 1), 0)
        row0 = pl.multiple_of(base * TK, TK)
        wk = SB_WIN * TK
        key_pos = base * TK + lax.broadcasted_iota(jnp.int32, (TQ, wk), 1)
        visible = key_pos < qt * TQ + lax.broadcasted_iota(jnp.int32, (TQ, wk), 0)
        zs = [jnp.where(visible, _dot_nt(q_ref[rows, cs], k_ref[pl.ds(row0, wk), cs]), NEG_INF) for cs in heads]
        lg = [logs(z) for z in zs]
        lhs = [part for (_, log_not) in lg for part in tile_sums(log_not, SB_WIN)]
        sums = _by_row_halves(_dot, jnp.concatenate(lhs, axis=0), u)
        worst = None
        for h, cs in enumerate(heads):
            carry = None
            shifted = [None] * SB_WIN
            for t in reversed(range(SB_WIN)):
                blk = sums[(h * SB_WIN + t) * TQ:(h * SB_WIN + t + 1) * TQ]
                tail, total = blk[:, :TK], blk[:, TK:]
                shifted[t] = tail if carry is None else tail + carry
                carry = total if carry is None else carry + total
            a = jnp.exp(lg[h][0] + jnp.concatenate(shifted, axis=1))
            acc_ref[sub * hg + h] = _dot(a.astype(BF16), v_ref[pl.ds(row0, wk), cs])
            carry_ref[sub * hg + h] = carry
            worst = carry if worst is None else jnp.maximum(worst, carry)
        return base, jnp.max(worst)

    def far_sweep(sub, base, worst):
        rows = slice(sub * TQ, (sub + 1) * TQ)

        def tile(kb):
            r0 = pl.multiple_of(kb * TK, TK)
            lg1 = [logs(_dot_nt(q_ref[rows, cs], k_ref[pl.ds(r0, TK), cs])) for cs in heads]
            sums1 = _dot(jnp.concatenate([tile_sums(log_not, 1)[0] for (_, log_not) in lg1], axis=0), u)
            far = None
            for h, cs in enumerate(heads):
                blk = sums1[h * TQ:(h + 1) * TQ]
                carry = carry_ref[sub * hg + h]
                a = jnp.exp(lg1[h][0] + blk[:, :TK] + carry)
                acc_ref[sub * hg + h] += _dot(a.astype(BF16), v_ref[pl.ds(r0, TK), cs])
                carry = carry + blk[:, TK:]
                carry_ref[sub * hg + h] = carry
                far = carry if far is None else jnp.maximum(far, carry)
            return jnp.max(far)

        def cond(c):
            kb, far = c
            return jnp.logical_and(kb >= 0, far > SB_DEAD_LOG)

        def body(c):
            kb, _ = c
            return kb - 1, tile(kb)

        lax.while_loop(cond, body, (base - 1, worst))

    starts = [window(sub) for sub in range(SB_QT)]
    for sub, (base, worst) in enumerate(starts):
        far_sweep(sub, base, worst)
    for sub in range(SB_QT):
        for h, cs in enumerate(heads):
            o_ref[sub * TQ:(sub + 1) * TQ, cs] = acc_ref[sub * hg + h]


def _sb_attention(proj, hg=4):
    s = proj.shape[0]
    w = hg * HEAD_DIM
    nblk = SB_W // w
    tri = np.arange(TK)[:, None] > np.arange(TK)[None, :]
    half = np.concatenate([tri, np.ones((TK, TK), bool)], axis=1)
    u = jnp.asarray(np.concatenate([half, half], axis=0), dtype=BF16)
    tq = SB_QT * TQ
    vmem = 2 * 2 * s * w * 2 + 2 * tq * w * 2 + 2 * tq * w * 4 + 2 * SB_QT * hg * TQ * TK * 4 + (16 << 20)
    return pl.pallas_call(
        functools.partial(_sb_kernel, hg=hg),
        grid=(nblk, s // tq),
        in_specs=[pl.BlockSpec((tq, w), lambda g, i: (i, g)),
                  pl.BlockSpec((s, w), lambda g, i: (0, nblk + g)),
                  pl.BlockSpec((s, w), lambda g, i: (0, 2 * nblk + g)),
                  pl.BlockSpec((2 * TK, 2 * TK), lambda g, i: (0, 0))],
        out_specs=pl.BlockSpec((tq, w), lambda g, i: (i, g)),
        out_shape=jax.ShapeDtypeStruct((s, SB_W), F32),
        scratch_shapes=[pltpu.VMEM((SB_QT * hg, TQ, TK), F32), pltpu.VMEM((SB_QT * hg, TQ, HEAD_DIM), F32)],
        compiler_params=_cparams(("parallel", "arbitrary"), vmem),
        name="sb_attention",
    )(proj, proj, proj, u)


def _compress_kernel(r_ref, pos_ref, w1_ref, w2_ref, o_ref):
    r = r_ref[0]
    half = r.shape[1]
    first = _dot(r, w1_ref[:half, :])
    second = _dot(r, w1_ref[half:, :])
    second = pltpu.roll(second, second.shape[0] - 1, 0)
    pos_term = _dot(pos_ref[...].astype(BF16), w1_ref[...])[0:1, :]
    hidden = jax.nn.gelu(first + second + pos_term)
    o_ref[0] = _dot(hidden.astype(BF16), w2_ref[...]).astype(o_ref.dtype)


def _compress(r, pos, w1, w2):
    g, n, half = r.shape
    pos_rows = jnp.zeros((SUBLANES, 2 * half), F32).at[0].set(pos.reshape(-1))
    return pl.pallas_call(
        _compress_kernel,
        grid=(g,),
        in_specs=[pl.BlockSpec((1, n, half), lambda i: (i, 0, 0)),
                  pl.BlockSpec((SUBLANES, 2 * half), lambda i: (0, 0)),
                  pl.BlockSpec((2 * half, CMP_HIDDEN), lambda i: (0, 0)),
                  pl.BlockSpec((CMP_HIDDEN, HEAD_DIM), lambda i: (0, 0))],
        out_specs=pl.BlockSpec((1, n, HEAD_DIM), lambda i: (i, 0, 0)),
        out_shape=jax.ShapeDtypeStruct((g, n, HEAD_DIM), BF16),
        compiler_params=_cparams(("parallel",), 24 << 20),
        name="nsa_compress",
    )(r, pos_rows, w1.astype(BF16), w2.astype(BF16))


def _stack_heads(q_ref, qs_ref, group=0):
    for h in range(NSA_HPG):
        r0 = (group * NSA_HPG + h) * TQ
        c0 = (group * NSA_HPG + h) * HEAD_DIM
        qs_ref[r0:r0 + TQ, :] = q_ref[:, c0:c0 + HEAD_DIM]


def _nsa_cmp_kernel(slopes_ref, q_ref, kc_ref, vc_ref, msel_ref, oc_ref, sel_ref, flag_ref, qs_ref, *, n_top):
    i = pl.program_id(0)
    n_cmp = kc_ref.shape[1]
    n_slc = msel_ref.shape[1]
    t_rel = lax.broadcasted_iota(jnp.int32, (TQ, n_cmp), 0)
    end_rel = lax.broadcasted_iota(jnp.int32, (TQ, n_cmp), 1) * STRIDE_CMP + (L_CMP - 1) - i * TQ
    bias = jnp.where(end_rel <= t_rel, 0.0, NEG_INF)
    end_row = end_rel[0:1, :].astype(F32)
    t_col = i * TQ + lax.broadcasted_iota(jnp.int32, (TQ, 1), 0)
    row_valid = jnp.where(t_col >= L_CMP - 1, 1.0, 0.0)
    blk = lax.broadcasted_iota(jnp.int32, (n_slc, TQ), 0)
    cur = lax.shift_right_logical(i * TQ + lax.broadcasted_iota(jnp.int32, (n_slc, TQ), 1), LOG2_L_SLC)
    valid = blk <= cur
    bonus = jnp.where(blk == 0, FORCE_BONUS, 0.0)
    bonus = jnp.where(blk == cur, FORCE_BONUS, bonus)
    bonus = jnp.where(blk == cur - 1, FORCE_BONUS, bonus)
    blk_f = blk.astype(F32)
    msel = msel_ref[...]
    rows = NSA_HPG * TQ
    for g in range(NSA_KV):
        _stack_heads(q_ref, qs_ref, g)
        z = _by_row_halves(_dot_nt, qs_ref[g * rows:(g + 1) * rows, :], kc_ref[g])
        p_sum = jnp.zeros((TQ, n_cmp), F32)
        ps = []
        for h in range(NSA_HPG):
            s = z[h * TQ:(h + 1) * TQ] + (bias + slopes_ref[g * NSA_HPG + h] * end_row)
            m = jnp.max(s, axis=-1, keepdims=True)
            e = jnp.exp(s - m)
            p = e * (row_valid / jnp.sum(e, axis=-1, keepdims=True))
            ps.append(p.astype(BF16))
            p_sum = p_sum + p
        oc = _by_row_halves(_dot, jnp.concatenate(ps, axis=0), vc_ref[g])
        for h in range(NSA_HPG):
            c0 = (g * NSA_HPG + h) * HEAD_DIM
            oc_ref[:, c0:c0 + HEAD_DIM] = oc[h * TQ:(h + 1) * TQ]
        hi, lo = _split_bf16(p_sum)
        imp = _dot(hi, msel) + _dot(lo, msel)
        score = jnp.where(valid, imp.T + bonus, NEG_INF)
        taken = jnp.zeros((n_slc, TQ), F32)
        for _ in range(n_top):
            best = jnp.max(score, axis=0, keepdims=True)
            first = jnp.min(jnp.where(score == best, blk_f, float(n_slc)), axis=0, keepdims=True)
            hit = blk_f == first
            taken = jnp.where(hit, 1.0, taken)
            score = jnp.where(hit, -jnp.inf, score)
        chosen_q = jnp.where(valid, taken, 0.0).T
        sel_ref[g, 0] = chosen_q.astype(sel_ref.dtype)
        flag_ref[g, 0] = jnp.max(chosen_q, axis=0, keepdims=True)


def _select_matrix(n_cmp_pad, n_slc):
    rs = L_SLC // STRIDE_CMP
    rc = L_CMP // STRIDE_CMP
    m = np.zeros((n_cmp_pad, n_slc), np.float32)
    for j in range(n_slc):
        for a in range(rs):
            for b in range(rc):
                src = rs * j - a - b
                if 0 <= src < n_cmp_pad - 1:
                    m[src, j] += 1.0
    return jnp.asarray(m, dtype=BF16)


def _nsa_cmp(proj, k_cmp, v_cmp, slopes):
    s = proj.shape[0]
    n_cmp = k_cmp.shape[1]
    n_slc = s // L_SLC
    nq = s // TQ
    msel = _select_matrix(n_cmp, n_slc)
    qblk = (3 * SB_W) // NSA_W
    return pl.pallas_call(
        functools.partial(_nsa_cmp_kernel, n_top=min(N_TOPK, n_slc)),
        grid=(nq,),
        in_specs=[pl.BlockSpec(memory_space=pltpu.SMEM),
                  pl.BlockSpec((TQ, NSA_W), lambda i: (i, qblk)),
                  pl.BlockSpec((NSA_KV, n_cmp, HEAD_DIM), lambda i: (0, 0, 0)),
                  pl.BlockSpec((NSA_KV, n_cmp, HEAD_DIM), lambda i: (0, 0, 0)),
                  pl.BlockSpec((n_cmp, n_slc), lambda i: (0, 0))],
        out_specs=[pl.BlockSpec((TQ, NSA_W), lambda i: (i, 0)),
                   pl.BlockSpec((NSA_KV, 1, TQ, n_slc), lambda i: (0, i, 0, 0)),
                   pl.BlockSpec((NSA_KV, 1, 1, n_slc), lambda i: (0, i, 0, 0))],
        out_shape=[jax.ShapeDtypeStruct((s, NSA_W), F32),
                   jax.ShapeDtypeStruct((NSA_KV, nq, TQ, n_slc), BF16),
                   jax.ShapeDtypeStruct((NSA_KV, nq, 1, n_slc), F32)],
        scratch_shapes=[pltpu.VMEM((NSA_HEADS * TQ, HEAD_DIM), BF16)],
        compiler_params=_cparams(("parallel",), 40 << 20),
        name="nsa_cmp_select",
    )(slopes, proj, k_cmp, v_cmp, msel)


def _nsa_main_kernel(tiles_ref, count_ref, slopes_ref, q_ref, ks_ref, vs_ref, kw_ref, vw_ref, sel_ref, oc_ref,
                     gate_ref, o_ref, qs_ref, acc_s, m_s):
    g = pl.program_id(0)
    i = pl.program_id(1)
    n_kt = pl.num_programs(1) * TQ // NSA_TK
    _stack_heads(q_ref, qs_ref)
    slopes = [slopes_ref[g * NSA_HPG + h] for h in range(NSA_HPG)]
    t_rel = lax.broadcasted_iota(jnp.int32, (TQ, NSA_TK), 0)
    lane = lax.broadcasted_iota(jnp.int32, (TQ, NSA_TK), 1)
    diag = lax.div(i * TQ, NSA_TK)

    def scores(k_ref, v_ref, pieces, biases):
        z = jnp.concatenate([_by_row_halves(_dot_nt, qs_ref[...], k_ref[pl.ds(r0, n), :]) for r0, n in pieces],
                            axis=1)
        v_aug = jnp.concatenate(
            [jnp.concatenate([v_ref[pl.ds(r0, n), :], jnp.ones((n, HEAD_DIM), BF16)], axis=1) for r0, n in pieces],
            axis=0)
        key_rel = jnp.concatenate(
            [(r0 - i * TQ + lax.broadcasted_iota(jnp.int32, (1, n), 1)).astype(F32) for r0, n in pieces], axis=1)
        bias = jnp.concatenate(biases, axis=1)
        return [z[h * TQ:(h + 1) * TQ] + (bias + slopes[h] * key_rel) for h in range(NSA_HPG)], v_aug

    def attend(k_ref, v_ref, pieces, biases, acc, m_ref, first):
        s_all, v_aug = scores(k_ref, v_ref, pieces, biases)
        width = s_all[0].shape[1]
        ps, alphas = [], []
        for h, s in enumerate(s_all):
            rows = slice(h * TQ, (h + 1) * TQ)
            m_tile = jnp.max(s, axis=-1, keepdims=True)
            if first:
                m_new = jnp.broadcast_to(m_tile, (TQ, LANES))
            else:
                m_old = m_ref[rows]
                m_new = jnp.maximum(m_old, m_tile)
                alphas.append(jnp.exp(m_old - m_new))
            m_ref[rows] = m_new
            ps.append(jnp.exp(s - jnp.concatenate([m_new] * (width // LANES), axis=1)).astype(BF16))
        pv = _by_row_halves(_dot, jnp.concatenate(ps, axis=0), v_aug)
        for h in range(NSA_HPG):
            rows = slice(h * TQ, (h + 1) * TQ)
            if first:
                acc[rows] = pv[rows]
            else:
                acc[rows] = acc[rows] * jnp.concatenate([alphas[h], alphas[h]], axis=1) + pv[rows]

    def tile_rows(kt):
        return pl.multiple_of(kt * NSA_TK, NSA_TK), NSA_TK

    def select_bias(kt, causal):
        n_slc = sel_ref.shape[3]
        blk = lax.broadcasted_iota(jnp.int32, (n_slc, NSA_TK), 0)
        key_blk = kt * (NSA_TK // L_SLC) + lax.shift_right_logical(
            lax.broadcasted_iota(jnp.int32, (n_slc, NSA_TK), 1), LOG2_L_SLC)
        expand = jnp.where(blk == key_blk, 1.0, 0.0).astype(BF16)
        picked = _dot(sel_ref[0, 0], expand)
        if causal:
            picked = jnp.where(kt * NSA_TK - i * TQ + lane <= t_rel, picked, 0.0)
        return jnp.where(picked > 0.5, 0.0, NEG_INF)

    attend(ks_ref, vs_ref, [tile_rows(diag)], [select_bias(diag, True)], acc_s, m_s, True)
    step = g * pl.num_programs(1) + i
    n_far = count_ref[step]

    def pair_body(p, _):
        ta = tiles_ref[step * n_kt + 2 * p]
        tb = tiles_ref[step * n_kt + 2 * p + 1]
        attend(ks_ref, vs_ref, [tile_rows(ta), tile_rows(tb)], [select_bias(ta, False), select_bias(tb, False)],
               acc_s, m_s, False)
        return 0

    lax.fori_loop(0, lax.shift_right_logical(n_far, 1), pair_body, 0)

    @pl.when(lax.rem(n_far, 2) == 1)
    def _():
        tl = tiles_ref[step * n_kt + n_far - 1]
        attend(ks_ref, vs_ref, [tile_rows(tl)], [select_bias(tl, False)], acc_s, m_s, False)

    wn = WINDOW + TQ
    w0 = pl.multiple_of(jnp.maximum(i * TQ - WINDOW, 0), TQ)
    dist = i * TQ + lax.broadcasted_iota(jnp.int32, (TQ, wn), 0) - (w0 + lax.broadcasted_iota(jnp.int32, (TQ, wn), 1))
    w_bias = jnp.where(dist >= 0, jnp.where(dist < WINDOW, 0.0, NEG_INF), NEG_INF)
    s_win, v_win = scores(kw_ref, vw_ref, [(w0, wn)], [w_bias])
    p_win = [jnp.exp(s - jnp.max(s, axis=-1, keepdims=True)).astype(BF16) for s in s_win]
    pv_win = _by_row_halves(_dot, jnp.concatenate(p_win, axis=0), v_win)

    gates = gate_ref[...]
    for h in range(NSA_HPG):
        cs = slice(h * HEAD_DIM, (h + 1) * HEAD_DIM)
        a_s = acc_s[h * TQ:(h + 1) * TQ]
        a_w = pv_win[h * TQ:(h + 1) * TQ]
        o_sel = a_s[:, :HEAD_DIM] / a_s[:, HEAD_DIM:HEAD_DIM + 1]
        o_win = a_w[:, :HEAD_DIM] / a_w[:, HEAD_DIM:HEAD_DIM + 1]
        o_ref[:, cs] = (gates[:, 3 * h:3 * h + 1] * oc_ref[:, cs] + gates[:, 3 * h + 1:3 * h + 2] * o_sel
                        + gates[:, 3 * h + 2:3 * h + 3] * o_win)


def _far_tile_lists(tile_any, s):
    nq = s // TQ
    n_kt = s // NSA_TK
    hit = tile_any.reshape(NSA_KV, nq, n_kt, NSA_TK // L_SLC).max(axis=-1) > 0.5
    kt = jnp.arange(n_kt, dtype=jnp.int32)
    diag = (jnp.arange(nq, dtype=jnp.int32) * TQ) // NSA_TK
    far = jnp.logical_and(hit, kt[None, None, :] < diag[None, :, None])
    tiles = jnp.sort(jnp.where(far, kt[None, None, :], n_kt), axis=-1)
    tiles = jnp.minimum(tiles, n_kt - 1)
    return tiles.reshape(-1), jnp.sum(far, axis=-1, dtype=jnp.int32).reshape(-1)


def _nsa_main(proj, sel, tile_any, o_c, gates, slopes):
    s = proj.shape[0]
    nq = s // TQ
    n_slc = s // L_SLC
    qblk = (3 * SB_W) // GROUP_W
    tiles, counts = _far_tile_lists(tile_any, s)
    kv_spec = lambda cb: pl.BlockSpec((s, HEAD_DIM), lambda g, i, t, c: (0, cb + g))
    rows = NSA_HPG * TQ
    grid_spec = pltpu.PrefetchScalarGridSpec(
        num_scalar_prefetch=2,
        grid=(NSA_KV, nq),
        in_specs=[pl.BlockSpec(memory_space=pltpu.SMEM),
                  pl.BlockSpec((TQ, GROUP_W), lambda g, i, t, c: (i, qblk + g)),
                  kv_spec(CB_KS), kv_spec(CB_VS), kv_spec(CB_KW), kv_spec(CB_VW),
                  pl.BlockSpec((1, 1, TQ, n_slc), lambda g, i, t, c: (g, i, 0, 0)),
                  pl.BlockSpec((TQ, GROUP_W), lambda g, i, t, c: (i, g)),
                  pl.BlockSpec((TQ, LANES), lambda g, i, t, c: (i, g))],
        out_specs=pl.BlockSpec((TQ, GROUP_W), lambda g, i, t, c: (i, g)),
        scratch_shapes=[pltpu.VMEM((rows, HEAD_DIM), BF16),
                        pltpu.VMEM((rows, 2 * HEAD_DIM), F32), pltpu.VMEM((rows, LANES), F32)],
    )
    vmem = 2 * 4 * s * HEAD_DIM * 2 + (24 << 20)
    return pl.pallas_call(
        _nsa_main_kernel,
        grid_spec=grid_spec,
        out_shape=jax.ShapeDtypeStruct((s, NSA_W), F32),
        compiler_params=_cparams(("parallel", "arbitrary"), vmem),
        name="nsa_select_window",
    )(tiles, counts, slopes, proj, proj, proj, proj, proj, sel, o_c, gates)


def kernel(x, attn_norm, w_in, pos_cmp_k, pos_cmp_v, w_cmp_k1, w_cmp_k2, w_cmp_v1, w_cmp_v2, norm_sb, norm_nsa,
           w_out, ffn_norm, w_gate, w_up, w_down, final_norm):
    batch, s, d_model = x.shape
    assert batch == 1 and s % 1024 == 0 and w_in.shape[2] == MAIN_COLS + GATE_COLS
    depth = w_in.shape[0]
    d_ff = w_gate.shape[2]
    assert d_ff % (2 * LANES) == 0 and (d_ff // 2) % LANES == 0
    scale = HEAD_DIM ** -0.5
    head_idx = jnp.arange(1, NSA_HEADS + 1, dtype=F32)
    slopes = 2.0 ** (-8.0 * head_idx / NSA_HEADS)
    col_scale = jnp.concatenate([jnp.full((SB_W,), scale, F32), jnp.ones((2 * SB_W,), F32),
                                 jnp.full((NSA_W,), scale, F32), jnp.ones((MAIN_COLS - 3 * SB_W - NSA_W,), F32)])
    col_scale = col_scale.reshape(1, MAIN_COLS)
    xs = x[0]
    w_in_t = jnp.swapaxes(w_in, 1, 2)
    for l in range(depth):
        w_g = w_in[l][:, MAIN_COLS:].reshape(d_model, NSA_KV, 3 * NSA_HPG)
        w_g = jnp.pad(w_g, ((0, 0), (0, 0), (0, LANES - 3 * NSA_HPG))).reshape(1, d_model, NSA_KV * LANES)

        h = _rmsnorm(xs, attn_norm[l], BF16)
        proj = _matmul_fullk(_mm_scale_nt_kernel, h, [w_in_t], l, MAIN_COLS, [(col_scale, "col")], BF16, 1024, 512,
                             "in_proj", transposed=True)
        gates = _matmul_fullk(_mm_sigmoid_kernel, h, [w_g], 0, NSA_KV * LANES, [], F32, 1024, NSA_KV * LANES,
                              "gate_proj")

        o_sb = _sb_attention(proj)

        def blocks_view(cb):
            cols = proj[:, cb * HEAD_DIM:(cb + NSA_KV) * HEAD_DIM]
            r = cols.reshape(s // STRIDE_CMP, STRIDE_CMP, NSA_KV, HEAD_DIM).transpose(2, 0, 1, 3)
            return r.reshape(NSA_KV, s // STRIDE_CMP, STRIDE_CMP * HEAD_DIM)

        k_cmp = _compress(blocks_view(CB_KC), pos_cmp_k[l], w_cmp_k1[l], w_cmp_k2[l])
        v_cmp = _compress(blocks_view(CB_VC), pos_cmp_v[l], w_cmp_v1[l], w_cmp_v2[l])
        o_c, sel, tile_any = _nsa_cmp(proj, k_cmp, v_cmp, slopes)
        o_nsa = _nsa_main(proj, sel, tile_any, o_c, gates, slopes)

        mixed = _rmsnorm_pair(o_sb, o_nsa, norm_sb[l], norm_nsa[l])
        xs = _matmul_fullk(_mm_res_kernel, mixed, [w_out], l, d_model, [(xs, "tile")], F32, 1024, 512, "out_proj")

        h = _rmsnorm(xs, ffn_norm[l], BF16)
        act = _matmul_fullk(_mm_swiglu_kernel, h, [w_gate, w_up], l, d_ff, [], BF16, 1024, 2 * LANES, "ffn_gate_up")
        xs = _matmul_acc_res(act, w_down, l, xs, 1024, 2 * LANES, d_ff // 2)
    return _rmsnorm(xs, final_norm, F32)[None]
```

```python
import functools

import numpy as np
import jax
import jax.numpy as jnp
from jax import lax
from jax.experimental import pallas as pl
from jax.experimental.pallas import tpu as pltpu

HEAD_DIM = 128
SB_HEADS = 16
NSA_HEADS = 16
NSA_KV = 2
NSA_HPG = NSA_HEADS // NSA_KV
L_CMP = 32
STRIDE_CMP = 16
CMP_HIDDEN = 256
L_SLC = 64
LOG2_L_SLC = 6
N_TOPK = 16
WINDOW = 512
EPS = 1e-6
NEG_INF = -1e30
FORCE_BONUS = 1e6

LANES = 128
SUBLANES = 8
TQ = 128
TK = 128
SB_WIN = 3
SB_QT = 2
NSA_TK = 256
VMEM_CAP = 56 * 1024 * 1024
SB_DEAD_LOG = -105.0

F32 = jnp.float32
BF16 = jnp.bfloat16

SB_W = SB_HEADS * HEAD_DIM
NSA_W = NSA_HEADS * HEAD_DIM
GROUP_W = NSA_HPG * HEAD_DIM
MAIN_COLS = 3 * SB_W + NSA_W + 3 * 2 * NSA_KV * HEAD_DIM
GATE_COLS = 3 * NSA_HEADS
CB_KC, CB_VC, CB_KS, CB_VS, CB_KW, CB_VW = 64, 66, 68, 70, 72, 74


def _cparams(sem, vmem_bytes):
    return pltpu.CompilerParams(dimension_semantics=sem, vmem_limit_bytes=int(min(vmem_bytes, VMEM_CAP)))


def _dot(a, b):
    return jnp.dot(a, b, preferred_element_type=F32)


def _dot_nt(a, b):
    return lax.dot_general(a, b, (((1,), (1,)), ((), ())), preferred_element_type=F32)


def _by_row_halves(dot_fn, a, b):
    half = a.shape[0] // 2
    return jnp.concatenate([dot_fn(a[:half], b), dot_fn(a[half:], b)], axis=0)


def _split_bf16(x):
    hi = x.astype(BF16)
    lo = (x - hi.astype(F32)).astype(BF16)
    return hi, lo


def _rms_kernel(x_ref, g_ref, o_ref):
    x = x_ref[...]
    ms = jnp.mean(x * x, axis=-1, keepdims=True)
    o_ref[...] = (x * lax.rsqrt(ms + EPS) * g_ref[...]).astype(o_ref.dtype)


def _rmsnorm(x, g, out_dtype, tm=256):
    m, d = x.shape
    return pl.pallas_call(
        _rms_kernel,
        grid=(m // tm,),
        in_specs=[pl.BlockSpec((tm, d), lambda i: (i, 0)), pl.BlockSpec((1, d), lambda i: (0, 0))],
        out_specs=pl.BlockSpec((tm, d), lambda i: (i, 0)),
        out_shape=jax.ShapeDtypeStruct((m, d), out_dtype),
        compiler_params=_cparams(("parallel",), 6 * tm * d * 4),
        name="rmsnorm",
    )(x, g.reshape(1, d))


def _row_scale(a_ref):
    a = a_ref[...].astype(F32)
    ms = jnp.mean(a * a, axis=-1, keepdims=True)
    return jnp.broadcast_to(lax.rsqrt(ms + EPS), (a.shape[0], LANES))


def _gained_weight(w, g_ref):
    return (w * jnp.concatenate([g_ref[...]] * (w.shape[1] // LANES), axis=1)).astype(BF16)


def _lanes(x, n):
    return jnp.concatenate([x] * (n // LANES), axis=1)


def _out_proj_kernel(a1_ref, a2_ref, w_ref, g1_ref, g2_ref, res_ref, o_ref, ob_ref, r_ref):
    @pl.when(pl.program_id(1) == 0)
    def _():
        r_ref[0] = _row_scale(a1_ref)
        r_ref[1] = _row_scale(a2_ref)

    k1 = a1_ref.shape[1]
    tn = o_ref.shape[1]
    y1 = _dot(a1_ref[...], _gained_weight(w_ref[:k1, :], g1_ref))
    y2 = _dot(a2_ref[...], _gained_weight(w_ref[k1:, :], g2_ref))
    out = res_ref[...] + (_lanes(r_ref[0], tn) * y1 + _lanes(r_ref[1], tn) * y2)
    o_ref[...] = out
    ob_ref[...] = out.astype(ob_ref.dtype)


def _out_proj(a1, a2, w, layer, g1, g2, res, tm=1024, tn=512):
    m, k1 = a1.shape
    k2 = a2.shape[1]
    n = w.shape[2]
    gain = lambda g: jnp.broadcast_to(g.reshape(-1, 1), (g.shape[0], LANES))
    vmem = (2 * tm * (k1 + k2) * 2 + (k1 + k2) * tn * (2 * 4 + 4 + 2) + 2 * (k1 + k2) * LANES * 4
            + 2 * tm * tn * (4 + 4 + 2) + 3 * tm * tn * 4 + (4 << 20))
    return pl.pallas_call(
        _out_proj_kernel,
        grid=(m // tm, n // tn),
        in_specs=[pl.BlockSpec((tm, k1), lambda i, j: (i, 0)), pl.BlockSpec((tm, k2), lambda i, j: (i, 0)),
                  pl.BlockSpec((None, k1 + k2, tn), lambda i, j: (layer, 0, j)),
                  pl.BlockSpec((k1, LANES), lambda i, j: (0, 0)), pl.BlockSpec((k2, LANES), lambda i, j: (0, 0)),
                  pl.BlockSpec((tm, tn), lambda i, j: (i, j))],
        out_specs=[pl.BlockSpec((tm, tn), lambda i, j: (i, j)), pl.BlockSpec((tm, tn), lambda i, j: (i, j))],
        out_shape=[jax.ShapeDtypeStruct((m, n), F32), jax.ShapeDtypeStruct((m, n), BF16)],
        scratch_shapes=[pltpu.VMEM((2, tm, LANES), F32)],
        compiler_params=_cparams(("parallel", "arbitrary"), vmem),
        name="out_proj",
    )(a1, a2, w, gain(g1), gain(g2), res)


def _ffn_up_kernel(a_ref, wg_ref, wu_ref, g_ref, o_ref, r_ref):
    @pl.when(pl.program_id(1) == 0)
    def _():
        r_ref[...] = _row_scale(a_ref)

    a = a_ref[...]
    r = _lanes(r_ref[...], o_ref.shape[1])
    gate = r * _dot(a, _gained_weight(wg_ref[...], g_ref))
    up = r * _dot(a, _gained_weight(wu_ref[...], g_ref))
    o_ref[...] = (jax.nn.silu(gate) * up).astype(o_ref.dtype)


def _ffn_up(a, w_gate, w_up, layer, g, tm=1024, tn=2 * LANES):
    m, k = a.shape
    n = w_gate.shape[2]
    gain = jnp.broadcast_to(g.reshape(-1, 1), (k, LANES))
    vmem = 2 * tm * k * 2 + 2 * k * tn * (2 * 4 + 4 + 2) + 2 * k * LANES * 4 + 2 * tm * tn * 2 + 5 * tm * tn * 4 + (4 << 20)
    w_spec = pl.BlockSpec((None, k, tn), lambda i, j: (layer, 0, j))
    return pl.pallas_call(
        _ffn_up_kernel,
        grid=(m // tm, n // tn),
        in_specs=[pl.BlockSpec((tm, k), lambda i, j: (i, 0)), w_spec, w_spec,
                  pl.BlockSpec((k, LANES), lambda i, j: (0, 0))],
        out_specs=pl.BlockSpec((tm, tn), lambda i, j: (i, j)),
        out_shape=jax.ShapeDtypeStruct((m, n), BF16),
        scratch_shapes=[pltpu.VMEM((tm, LANES), F32)],
        compiler_params=_cparams(("parallel", "arbitrary"), vmem),
        name="ffn_gate_up",
    )(a, w_gate, w_up, gain)


def _mm_scale_nt_kernel(a_ref, bt_ref, cs_ref, o_ref):
    o_ref[...] = (_dot_nt(a_ref[...], bt_ref[...].astype(BF16)) * cs_ref[...]).astype(o_ref.dtype)


def _mm_sigmoid_kernel(a_ref, b_ref, o_ref):
    o_ref[...] = jax.nn.sigmoid(_dot(a_ref[...], b_ref[...].astype(BF16)))


def _mm_acc_res_kernel(a_ref, b_ref, r_ref, o_ref, acc_ref):
    k = pl.program_id(2)

    @pl.when(k == 0)
    def _():
        acc_ref[...] = r_ref[...]

    acc_ref[...] += _dot(a_ref[...], b_ref[...].astype(BF16))

    @pl.when(k == pl.num_programs(2) - 1)
    def _():
        o_ref[...] = acc_ref[...]


def _matmul_fullk(kernel_fn, a, bs, layer, n, extras, out_dtype, tm, tn, name, transposed=False):
    m, k = a.shape
    in_specs = [pl.BlockSpec((tm, k), lambda i, j: (i, 0))]
    if transposed:
        in_specs += [pl.BlockSpec((None, tn, k), lambda i, j: (layer, j, 0)) for _ in bs]
    else:
        in_specs += [pl.BlockSpec((None, k, tn), lambda i, j: (layer, 0, j)) for _ in bs]
    w_bytes = bs[0].dtype.itemsize
    vmem = 2 * tm * k * 2 + len(bs) * k * tn * (2 * w_bytes + 2) + 2 * tm * tn * 4 + (2 + len(bs)) * tm * tn * 4
    args = [a, *bs]
    for arr, kind in extras:
        if kind == "col":
            in_specs.append(pl.BlockSpec((1, tn), lambda i, j: (0, j)))
        else:
            in_specs.append(pl.BlockSpec((tm, tn), lambda i, j: (i, j)))
            vmem += 2 * tm * tn * 4
        args.append(arr)
    return pl.pallas_call(
        kernel_fn,
        grid=(m // tm, n // tn),
        in_specs=in_specs,
        out_specs=pl.BlockSpec((tm, tn), lambda i, j: (i, j)),
        out_shape=jax.ShapeDtypeStruct((m, n), out_dtype),
        compiler_params=_cparams(("parallel", "parallel"), vmem + (4 << 20)),
        name=name,
    )(*args)


def _matmul_acc_res(a, b, layer, res, tm, tn, tk):
    m, k = a.shape
    n = b.shape[2]
    vmem = 2 * tm * tk * 2 + tk * tn * (2 * b.dtype.itemsize + 2) + 5 * tm * tn * 4 + tm * tn * 4
    return pl.pallas_call(
        _mm_acc_res_kernel,
        grid=(m // tm, n // tn, k // tk),
        in_specs=[pl.BlockSpec((tm, tk), lambda i, j, kk: (i, kk)),
                  pl.BlockSpec((None, tk, tn), lambda i, j, kk: (layer, kk, j)),
                  pl.BlockSpec((tm, tn), lambda i, j, kk: (i, j))],
        out_specs=pl.BlockSpec((tm, tn), lambda i, j, kk: (i, j)),
        out_shape=jax.ShapeDtypeStruct((m, n), F32),
        scratch_shapes=[pltpu.VMEM((tm, tn), F32)],
        compiler_params=_cparams(("parallel", "parallel", "arbitrary"), vmem + (4 << 20)),
        name="matmul_acc_res",
    )(a, b, res)


def _sb_kernel(q_ref, k_ref, v_ref, u_ref, o_ref, carry_ref, acc_ref, *, hg):
    step = pl.program_id(1)
    u = u_ref[...]
    heads = [slice(h * HEAD_DIM, (h + 1) * HEAD_DIM) for h in range(hg)]

    def logs(z):
        log1p_e = jnp.log(1.0 + jnp.exp(-jnp.abs(z)))
        log_sig = jnp.minimum(z, 0.0) - log1p_e
        return log_sig, log_sig - z

    def tile_sums(log_not, n):
        hi, lo = _split_bf16(log_not)
        return [jnp.concatenate([hi[:, t * TK:(t + 1) * TK], lo[:, t * TK:(t + 1) * TK]], axis=1) for t in range(n)]

    def window(sub):
        qt = step * SB_QT + sub
        rows = slice(sub * TQ, (sub + 1) * TQ)
        base = jnp.maximum(qt - (SB_WIN - 1), 0)
        row0 = pl.multiple_of(base * TK, TK)
        wk = SB_WIN * TK
        key_pos = base * TK + lax.broadcasted_iota(jnp.int32, (TQ, wk), 1)
        visible = key_pos < qt * TQ + lax.broadcasted_iota(jnp.int32, (TQ, wk), 0)
        zs = [jnp.where(visible, _dot_nt(q_ref[rows, cs], k_ref[pl.ds(row0, wk), cs]), NEG_INF) for cs in heads]
        lg = [logs(z) for z in zs]
        lhs = [part for (_, log_not) in lg for part in tile_sums(log_not, SB_WIN)]
        sums = _by_row_halves(_dot, jnp.concatenate(lhs, axis=0), u)
        worst = None
        for h, cs in enumerate(heads):
            carry = None
            shifted = [None] * SB_WIN
            for t in reversed(range(SB_WIN)):
                blk = sums[(h * SB_WIN + t) * TQ:(h * SB_WIN + t + 1) * TQ]
                tail, total = blk[:, :TK], blk[:, TK:]
                shifted[t] = tail if carry is None else tail + carry
                carry = total if carry is None else carry + total
            a = jnp.exp(lg[h][0] + jnp.concatenate(shifted, axis=1))
            acc_ref[sub * hg + h] = _dot(a.astype(BF16), v_ref[pl.ds(row0, wk), cs])
            carry_ref[sub * hg + h] = carry
            worst = carry if worst is None else jnp.maximum(worst, carry)
        return base, jnp.max(worst)

    def far_sweep(sub, base, worst):
        rows = slice(sub * TQ, (sub + 1) * TQ)

        def tile(kb):
            r0 = pl.multiple_of(kb * TK, TK)
            lg1 = [logs(_dot_nt(q_ref[rows, cs], k_ref[pl.ds(r0, TK), cs])) for cs in heads]
            sums1 = _dot(jnp.concatenate([tile_sums(log_not, 1)[0] for (_, log_not) in lg1], axis=0), u)
            far = None
            for h, cs in enumerate(heads):
                blk = sums1[h * TQ:(h + 1) * TQ]
                carry = carry_ref[sub * hg + h]
                a = jnp.exp(lg1[h][0] + blk[:, :TK] + carry)
                acc_ref[sub * hg + h] += _dot(a.astype(BF16), v_ref[pl.ds(r0, TK), cs])
                carry = carry + blk[:, TK:]
                carry_ref[sub * hg + h] = carry
                far = carry if far is None else jnp.maximum(far, carry)
            return jnp.max(far)

        def cond(c):
            kb, far = c
            return jnp.logical_and(kb >= 0, far > SB_DEAD_LOG)

        def body(c):
            kb, _ = c
            return kb - 1, tile(kb)

        lax.while_loop(cond, body, (base - 1, worst))

    starts = [window(sub) for sub in range(SB_QT)]
    for sub, (base, worst) in enumerate(starts):
        far_sweep(sub, base, worst)
    for sub in range(SB_QT):
        for h, cs in enumerate(heads):
            o_ref[sub * TQ:(sub + 1) * TQ, cs] = acc_ref[sub * hg + h].astype(o_ref.dtype)


def _sb_attention(proj, hg=4):
    s = proj.shape[0]
    w = hg * HEAD_DIM
    nblk = SB_W // w
    tri = np.arange(TK)[:, None] > np.arange(TK)[None, :]
    half = np.concatenate([tri, np.ones((TK, TK), bool)], axis=1)
    u = jnp.asarray(np.concatenate([half, half], axis=0), dtype=BF16)
    tq = SB_QT * TQ
    vmem = 2 * 2 * s * w * 2 + 2 * tq * w * 2 + 2 * tq * w * 4 + 2 * SB_QT * hg * TQ * TK * 4 + (16 << 20)
    return pl.pallas_call(
        functools.partial(_sb_kernel, hg=hg),
        grid=(nblk, s // tq),
        in_specs=[pl.BlockSpec((tq, w), lambda g, i: (i, g)),
                  pl.BlockSpec((s, w), lambda g, i: (0, nblk + g)),
                  pl.BlockSpec((s, w), lambda g, i: (0, 2 * nblk + g)),
                  pl.BlockSpec((2 * TK, 2 * TK), lambda g, i: (0, 0))],
        out_specs=pl.BlockSpec((tq, w), lambda g, i: (i, g)),
        out_shape=jax.ShapeDtypeStruct((s, SB_W), BF16),
        scratch_shapes=[pltpu.VMEM((SB_QT * hg, TQ, TK), F32), pltpu.VMEM((SB_QT * hg, TQ, HEAD_DIM), F32)],
        compiler_params=_cparams(("parallel", "arbitrary"), vmem),
        name="sb_attention",
    )(proj, proj, proj, u)


def _compress_kernel(r_ref, pos_ref, w1_ref, w2_ref, o_ref):
    r = r_ref[0]
    half = r.shape[1]
    first = _dot(r, w1_ref[:half, :])
    second = _dot(r, w1_ref[half:, :])
    second = pltpu.roll(second, second.shape[0] - 1, 0)
    pos_term = _dot(pos_ref[...].astype(BF16), w1_ref[...])[0:1, :]
    hidden = jax.nn.gelu(first + second + pos_term)
    o_ref[0] = _dot(hidden.astype(BF16), w2_ref[...]).astype(o_ref.dtype)


def _compress(r, pos, w1, w2):
    g, n, half = r.shape
    pos_rows = jnp.zeros((SUBLANES, 2 * half), F32).at[0].set(pos.reshape(-1))
    return pl.pallas_call(
        _compress_kernel,
        grid=(g,),
        in_specs=[pl.BlockSpec((1, n, half), lambda i: (i, 0, 0)),
                  pl.BlockSpec((SUBLANES, 2 * half), lambda i: (0, 0)),
                  pl.BlockSpec((2 * half, CMP_HIDDEN), lambda i: (0, 0)),
                  pl.BlockSpec((CMP_HIDDEN, HEAD_DIM), lambda i: (0, 0))],
        out_specs=pl.BlockSpec((1, n, HEAD_DIM), lambda i: (i, 0, 0)),
        out_shape=jax.ShapeDtypeStruct((g, n, HEAD_DIM), BF16),
        compiler_params=_cparams(("parallel",), 24 << 20),
        name="nsa_compress",
    )(r, pos_rows, w1.astype(BF16), w2.astype(BF16))


def _stack_heads(q_ref, qs_ref, group=0):
    for h in range(NSA_HPG):
        r0 = (group * NSA_HPG + h) * TQ
        c0 = (group * NSA_HPG + h) * HEAD_DIM
        qs_ref[r0:r0 + TQ, :] = q_ref[:, c0:c0 + HEAD_DIM]


def _nsa_cmp_kernel(slopes_ref, q_ref, kc_ref, vc_ref, msel_ref, oc_ref, sel_ref, flag_ref, qs_ref, *, n_top):
    i = pl.program_id(0)
    n_cmp = kc_ref.shape[1]
    n_slc = msel_ref.shape[1]
    t_rel = lax.broadcasted_iota(jnp.int32, (TQ, n_cmp), 0)
    end_rel = lax.broadcasted_iota(jnp.int32, (TQ, n_cmp), 1) * STRIDE_CMP + (L_CMP - 1) - i * TQ
    bias = jnp.where(end_rel <= t_rel, 0.0, NEG_INF)
    end_row = end_rel[0:1, :].astype(F32)
    t_col = i * TQ + lax.broadcasted_iota(jnp.int32, (TQ, 1), 0)
    row_valid = jnp.where(t_col >= L_CMP - 1, 1.0, 0.0)
    blk = lax.broadcasted_iota(jnp.int32, (n_slc, TQ), 0)
    cur = lax.shift_right_logical(i * TQ + lax.broadcasted_iota(jnp.int32, (n_slc, TQ), 1), LOG2_L_SLC)
    valid = blk <= cur
    bonus = jnp.where(blk == 0, FORCE_BONUS, 0.0)
    bonus = jnp.where(blk == cur, FORCE_BONUS, bonus)
    bonus = jnp.where(blk == cur - 1, FORCE_BONUS, bonus)
    blk_f = blk.astype(F32)
    msel = msel_ref[...]
    rows = NSA_HPG * TQ
    for g in range(NSA_KV):
        _stack_heads(q_ref, qs_ref, g)
        z = _by_row_halves(_dot_nt, qs_ref[g * rows:(g + 1) * rows, :], kc_ref[g])
        p_sum = jnp.zeros((TQ, n_cmp), F32)
        ps = []
        for h in range(NSA_HPG):
            s = z[h * TQ:(h + 1) * TQ] + (bias + slopes_ref[g * NSA_HPG + h] * end_row)
            m = jnp.max(s, axis=-1, keepdims=True)
            e = jnp.exp(s - m)
            p = e * (row_valid / jnp.sum(e, axis=-1, keepdims=True))
            ps.append(p.astype(BF16))
            p_sum = p_sum + p
        oc = _by_row_halves(_dot, jnp.concatenate(ps, axis=0), vc_ref[g])
        for h in range(NSA_HPG):
            c0 = (g * NSA_HPG + h) * HEAD_DIM
            oc_ref[:, c0:c0 + HEAD_DIM] = oc[h * TQ:(h + 1) * TQ]
        hi, lo = _split_bf16(p_sum)
        imp = _dot(hi, msel) + _dot(lo, msel)
        score = jnp.where(valid, imp.T + bonus, NEG_INF)
        taken = jnp.zeros((n_slc, TQ), F32)
        for _ in range(n_top):
            best = jnp.max(score, axis=0, keepdims=True)
            first = jnp.min(jnp.where(score == best, blk_f, float(n_slc)), axis=0, keepdims=True)
            hit = blk_f == first
            taken = jnp.where(hit, 1.0, taken)
            score = jnp.where(hit, -jnp.inf, score)
        chosen_q = jnp.where(valid, taken, 0.0).T
        sel_ref[g, 0] = chosen_q.astype(sel_ref.dtype)
        flag_ref[g, 0] = jnp.max(chosen_q, axis=0, keepdims=True)


def _select_matrix(n_cmp_pad, n_slc):
    rs = L_SLC // STRIDE_CMP
    rc = L_CMP // STRIDE_CMP
    m = np.zeros((n_cmp_pad, n_slc), np.float32)
    for j in range(n_slc):
        for a in range(rs):
            for b in range(rc):
                src = rs * j - a - b
                if 0 <= src < n_cmp_pad - 1:
                    m[src, j] += 1.0
    return jnp.asarray(m, dtype=BF16)


def _nsa_cmp(proj, k_cmp, v_cmp, slopes):
    s = proj.shape[0]
    n_cmp = k_cmp.shape[1]
    n_slc = s // L_SLC
    nq = s // TQ
    msel = _select_matrix(n_cmp, n_slc)
    qblk = (3 * SB_W) // NSA_W
    return pl.pallas_call(
        functools.partial(_nsa_cmp_kernel, n_top=min(N_TOPK, n_slc)),
        grid=(nq,),
        in_specs=[pl.BlockSpec(memory_space=pltpu.SMEM),
                  pl.BlockSpec((TQ, NSA_W), lambda i: (i, qblk)),
                  pl.BlockSpec((NSA_KV, n_cmp, HEAD_DIM), lambda i: (0, 0, 0)),
                  pl.BlockSpec((NSA_KV, n_cmp, HEAD_DIM), lambda i: (0, 0, 0)),
                  pl.BlockSpec((n_cmp, n_slc), lambda i: (0, 0))],
        out_specs=[pl.BlockSpec((TQ, NSA_W), lambda i: (i, 0)),
                   pl.BlockSpec((NSA_KV, 1, TQ, n_slc), lambda i: (0, i, 0, 0)),
                   pl.BlockSpec((NSA_KV, 1, 1, n_slc), lambda i: (0, i, 0, 0))],
        out_shape=[jax.ShapeDtypeStruct((s, NSA_W), F32),
                   jax.ShapeDtypeStruct((NSA_KV, nq, TQ, n_slc), BF16),
                   jax.ShapeDtypeStruct((NSA_KV, nq, 1, n_slc), F32)],
        scratch_shapes=[pltpu.VMEM((NSA_HEADS * TQ, HEAD_DIM), BF16)],
        compiler_params=_cparams(("parallel",), 40 << 20),
        name="nsa_cmp_select",
    )(slopes, proj, k_cmp, v_cmp, msel)


def _nsa_main_kernel(tiles_ref, count_ref, slopes_ref, q_ref, ks_ref, vs_ref, kw_ref, vw_ref, sel_ref, oc_ref,
                     gate_ref, o_ref, qs_ref, acc_s, m_s):
    g = pl.program_id(0)
    i = pl.program_id(1)
    n_kt = pl.num_programs(1) * TQ // NSA_TK
    _stack_heads(q_ref, qs_ref)
    slopes = [slopes_ref[g * NSA_HPG + h] for h in range(NSA_HPG)]
    t_rel = lax.broadcasted_iota(jnp.int32, (TQ, NSA_TK), 0)
    lane = lax.broadcasted_iota(jnp.int32, (TQ, NSA_TK), 1)
    diag = lax.div(i * TQ, NSA_TK)

    def scores(k_ref, v_ref, pieces, biases):
        z = jnp.concatenate([_by_row_halves(_dot_nt, qs_ref[...], k_ref[pl.ds(r0, n), :]) for r0, n in pieces],
                            axis=1)
        v_aug = jnp.concatenate(
            [jnp.concatenate([v_ref[pl.ds(r0, n), :], jnp.ones((n, HEAD_DIM), BF16)], axis=1) for r0, n in pieces],
            axis=0)
        key_rel = jnp.concatenate(
            [(r0 - i * TQ + lax.broadcasted_iota(jnp.int32, (1, n), 1)).astype(F32) for r0, n in pieces], axis=1)
        bias = jnp.concatenate(biases, axis=1)
        return [z[h * TQ:(h + 1) * TQ] + (bias + slopes[h] * key_rel) for h in range(NSA_HPG)], v_aug

    def attend(k_ref, v_ref, pieces, biases, acc, m_ref, first):
        s_all, v_aug = scores(k_ref, v_ref, pieces, biases)
        width = s_all[0].shape[1]
        ps, alphas = [], []
        for h, s in enumerate(s_all):
            rows = slice(h * TQ, (h + 1) * TQ)
            m_tile = jnp.max(s, axis=-1, keepdims=True)
            if first:
                m_new = jnp.broadcast_to(m_tile, (TQ, LANES))
            else:
                m_old = m_ref[rows]
                m_new = jnp.maximum(m_old, m_tile)
                alphas.append(jnp.exp(m_old - m_new))
            m_ref[rows] = m_new
            ps.append(jnp.exp(s - jnp.concatenate([m_new] * (width // LANES), axis=1)).astype(BF16))
        pv = _by_row_halves(_dot, jnp.concatenate(ps, axis=0), v_aug)
        for h in range(NSA_HPG):
            rows = slice(h * TQ, (h + 1) * TQ)
            if first:
                acc[rows] = pv[rows]
            else:
                acc[rows] = acc[rows] * jnp.concatenate([alphas[h], alphas[h]], axis=1) + pv[rows]

    def tile_rows(kt):
        return pl.multiple_of(kt * NSA_TK, NSA_TK), NSA_TK

    def select_bias(kt, causal):
        n_slc = sel_ref.shape[3]
        blk = lax.broadcasted_iota(jnp.int32, (n_slc, NSA_TK), 0)
        key_blk = kt * (NSA_TK // L_SLC) + lax.shift_right_logical(
            lax.broadcasted_iota(jnp.int32, (n_slc, NSA_TK), 1), LOG2_L_SLC)
        expand = jnp.where(blk == key_blk, 1.0, 0.0).astype(BF16)
        picked = _dot(sel_ref[0, 0], expand)
        if causal:
            picked = jnp.where(kt * NSA_TK - i * TQ + lane <= t_rel, picked, 0.0)
        return jnp.where(picked > 0.5, 0.0, NEG_INF)

    attend(ks_ref, vs_ref, [tile_rows(diag)], [select_bias(diag, True)], acc_s, m_s, True)
    step = g * pl.num_programs(1) + i
    n_far = count_ref[step]

    def pair_body(p, _):
        ta = tiles_ref[step * n_kt + 2 * p]
        tb = tiles_ref[step * n_kt + 2 * p + 1]
        attend(ks_ref, vs_ref, [tile_rows(ta), tile_rows(tb)], [select_bias(ta, False), select_bias(tb, False)],
               acc_s, m_s, False)
        return 0

    lax.fori_loop(0, lax.shift_right_logical(n_far, 1), pair_body, 0)

    @pl.when(lax.rem(n_far, 2) == 1)
    def _():
        tl = tiles_ref[step * n_kt + n_far - 1]
        attend(ks_ref, vs_ref, [tile_rows(tl)], [select_bias(tl, False)], acc_s, m_s, False)

    wn = WINDOW + TQ
    w0 = pl.multiple_of(jnp.maximum(i * TQ - WINDOW, 0), TQ)
    dist = i * TQ + lax.broadcasted_iota(jnp.int32, (TQ, wn), 0) - (w0 + lax.broadcasted_iota(jnp.int32, (TQ, wn), 1))
    w_bias = jnp.where(dist >= 0, jnp.where(dist < WINDOW, 0.0, NEG_INF), NEG_INF)
    s_win, v_win = scores(kw_ref, vw_ref, [(w0, wn)], [w_bias])
    p_win = [jnp.exp(s - jnp.max(s, axis=-1, keepdims=True)).astype(BF16) for s in s_win]
    pv_win = _by_row_halves(_dot, jnp.concatenate(p_win, axis=0), v_win)

    gates = gate_ref[...]
    for h in range(NSA_HPG):
        cs = slice(h * HEAD_DIM, (h + 1) * HEAD_DIM)
        a_s = acc_s[h * TQ:(h + 1) * TQ]
        a_w = pv_win[h * TQ:(h + 1) * TQ]
        o_sel = a_s[:, :HEAD_DIM] / a_s[:, HEAD_DIM:HEAD_DIM + 1]
        o_win = a_w[:, :HEAD_DIM] / a_w[:, HEAD_DIM:HEAD_DIM + 1]
        o_ref[:, cs] = (gates[:, 3 * h:3 * h + 1] * oc_ref[:, cs] + gates[:, 3 * h + 1:3 * h + 2] * o_sel
                        + gates[:, 3 * h + 2:3 * h + 3] * o_win).astype(o_ref.dtype)


def _far_tile_lists(tile_any, s):
    nq = s // TQ
    n_kt = s // NSA_TK
    hit = tile_any.reshape(NSA_KV, nq, n_kt, NSA_TK // L_SLC).max(axis=-1) > 0.5
    kt = jnp.arange(n_kt, dtype=jnp.int32)
    diag = (jnp.arange(nq, dtype=jnp.int32) * TQ) // NSA_TK
    far = jnp.logical_and(hit, kt[None, None, :] < diag[None, :, None])
    tiles = jnp.sort(jnp.where(far, kt[None, None, :], n_kt), axis=-1)
    tiles = jnp.minimum(tiles, n_kt - 1)
    return tiles.reshape(-1), jnp.sum(far, axis=-1, dtype=jnp.int32).reshape(-1)


def _nsa_main(proj, sel, tile_any, o_c, gates, slopes):
    s = proj.shape[0]
    nq = s // TQ
    n_slc = s // L_SLC
    qblk = (3 * SB_W) // GROUP_W
    tiles, counts = _far_tile_lists(tile_any, s)
    kv_spec = lambda cb: pl.BlockSpec((s, HEAD_DIM), lambda g, i, t, c: (0, cb + g))
    rows = NSA_HPG * TQ
    grid_spec = pltpu.PrefetchScalarGridSpec(
        num_scalar_prefetch=2,
        grid=(NSA_KV, nq),
        in_specs=[pl.BlockSpec(memory_space=pltpu.SMEM),
                  pl.BlockSpec((TQ, GROUP_W), lambda g, i, t, c: (i, qblk + g)),
                  kv_spec(CB_KS), kv_spec(CB_VS), kv_spec(CB_KW), kv_spec(CB_VW),
                  pl.BlockSpec((1, 1, TQ, n_slc), lambda g, i, t, c: (g, i, 0, 0)),
                  pl.BlockSpec((TQ, GROUP_W), lambda g, i, t, c: (i, g)),
                  pl.BlockSpec((TQ, LANES), lambda g, i, t, c: (i, g))],
        out_specs=pl.BlockSpec((TQ, GROUP_W), lambda g, i, t, c: (i, g)),
        scratch_shapes=[pltpu.VMEM((rows, HEAD_DIM), BF16),
                        pltpu.VMEM((rows, 2 * HEAD_DIM), F32), pltpu.VMEM((rows, LANES), F32)],
    )
    vmem = 2 * 4 * s * HEAD_DIM * 2 + (24 << 20)
    return pl.pallas_call(
        _nsa_main_kernel,
        grid_spec=grid_spec,
        out_shape=jax.ShapeDtypeStruct((s, NSA_W), BF16),
        compiler_params=_cparams(("parallel", "arbitrary"), vmem),
        name="nsa_select_window",
    )(tiles, counts, slopes, proj, proj, proj, proj, proj, sel, o_c, gates)


def kernel(x, attn_norm, w_in, pos_cmp_k, pos_cmp_v, w_cmp_k1, w_cmp_k2, w_cmp_v1, w_cmp_v2, norm_sb, norm_nsa,
           w_out, ffn_norm, w_gate, w_up, w_down, final_norm):
    batch, s, d_model = x.shape
    assert batch == 1 and s % 1024 == 0 and w_in.shape[2] == MAIN_COLS + GATE_COLS
    depth = w_in.shape[0]
    d_ff = w_gate.shape[2]
    assert d_ff % (2 * LANES) == 0 and (d_ff // 2) % LANES == 0
    scale = HEAD_DIM ** -0.5
    head_idx = jnp.arange(1, NSA_HEADS + 1, dtype=F32)
    slopes = 2.0 ** (-8.0 * head_idx / NSA_HEADS)
    col_scale = jnp.concatenate([jnp.full((SB_W,), scale, F32), jnp.ones((2 * SB_W,), F32),
                                 jnp.full((NSA_W,), scale, F32), jnp.ones((MAIN_COLS - 3 * SB_W - NSA_W,), F32)])
    col_scale = col_scale.reshape(1, MAIN_COLS)
    xs = x[0]
    w_in_t = jnp.swapaxes(w_in, 1, 2)
    for l in range(depth):
        w_g = w_in[l][:, MAIN_COLS:].reshape(d_model, NSA_KV, 3 * NSA_HPG)
        w_g = jnp.pad(w_g, ((0, 0), (0, 0), (0, LANES - 3 * NSA_HPG))).reshape(1, d_model, NSA_KV * LANES)

        h = _rmsnorm(xs, attn_norm[l], BF16)
        proj = _matmul_fullk(_mm_scale_nt_kernel, h, [w_in_t], l, MAIN_COLS, [(col_scale, "col")], BF16, 1024, 512,
                             "in_proj", transposed=True)
        gates = _matmul_fullk(_mm_sigmoid_kernel, h, [w_g], 0, NSA_KV * LANES, [], F32, 1024, NSA_KV * LANES,
                              "gate_proj")

        o_sb = _sb_attention(proj)

        def blocks_view(cb):
            cols = proj[:, cb * HEAD_DIM:(cb + NSA_KV) * HEAD_DIM]
            r = cols.reshape(s // STRIDE_CMP, STRIDE_CMP, NSA_KV, HEAD_DIM).transpose(2, 0, 1, 3)
            return r.reshape(NSA_KV, s // STRIDE_CMP, STRIDE_CMP * HEAD_DIM)

        k_cmp = _compress(blocks_view(CB_KC), pos_cmp_k[l], w_cmp_k1[l], w_cmp_k2[l])
        v_cmp = _compress(blocks_view(CB_VC), pos_cmp_v[l], w_cmp_v1[l], w_cmp_v2[l])
        o_c, sel, tile_any = _nsa_cmp(proj, k_cmp, v_cmp, slopes)
        o_nsa = _nsa_main(proj, sel, tile_any, o_c, gates, slopes)

        xs, xs_bf16 = _out_proj(o_sb, o_nsa, w_out, l, norm_sb[l], norm_nsa[l], xs)
        act = _ffn_up(xs_bf16, w_gate, w_up, l, ffn_norm[l])
        xs = _matmul_acc_res(act, w_down.astype(BF16), l, xs, 1024, 512, d_ff // 2)
    return _rmsnorm(xs, final_norm, F32)[None]
```

```python
import functools

import numpy as np
import jax
import jax.numpy as jnp
from jax import lax
from jax.experimental import pallas as pl
from jax.experimental.pallas import tpu as pltpu

HEAD_DIM = 128
SB_HEADS = 16
NSA_HEADS = 16
NSA_KV = 2
NSA_HPG = NSA_HEADS // NSA_KV
L_CMP = 32
STRIDE_CMP = 16
CMP_HIDDEN = 256
L_SLC = 64
LOG2_L_SLC = 6
N_TOPK = 16
WINDOW = 512
EPS = 1e-6
NEG_INF = -1e30
FORCE_BONUS = 1e6

LANES = 128
SUBLANES = 8
TQ = 128
TK = 128
SB_WIN = 3
SB_QT = 2
NSA_TK = 256
VMEM_CAP = 56 * 1024 * 1024
SB_DEAD_LOG = -105.0

F32 = jnp.float32
BF16 = jnp.bfloat16

SB_W = SB_HEADS * HEAD_DIM
NSA_W = NSA_HEADS * HEAD_DIM
GROUP_W = NSA_HPG * HEAD_DIM
MAIN_COLS = 3 * SB_W + NSA_W + 3 * 2 * NSA_KV * HEAD_DIM
GATE_COLS = 3 * NSA_HEADS
CB_KC, CB_VC, CB_KS, CB_VS, CB_KW, CB_VW = 64, 66, 68, 70, 72, 74


def _cparams(sem, vmem_bytes):
    return pltpu.CompilerParams(dimension_semantics=sem, vmem_limit_bytes=int(min(vmem_bytes, VMEM_CAP)))


def _dot(a, b):
    return jnp.dot(a, b, preferred_element_type=F32)


def _dot_nt(a, b):
    return lax.dot_general(a, b, (((1,), (1,)), ((), ())), preferred_element_type=F32)


def _by_row_halves(dot_fn, a, b):
    half = a.shape[0] // 2
    return jnp.concatenate([dot_fn(a[:half], b), dot_fn(a[half:], b)], axis=0)


def _split_bf16(x):
    hi = x.astype(BF16)
    lo = (x - hi.astype(F32)).astype(BF16)
    return hi, lo


def _rms_kernel(x_ref, g_ref, o_ref):
    x = x_ref[...]
    ms = jnp.mean(x * x, axis=-1, keepdims=True)
    o_ref[...] = (x * lax.rsqrt(ms + EPS) * g_ref[...]).astype(o_ref.dtype)


def _rmsnorm(x, g, out_dtype, tm=256):
    m, d = x.shape
    return pl.pallas_call(
        _rms_kernel,
        grid=(m // tm,),
        in_specs=[pl.BlockSpec((tm, d), lambda i: (i, 0)), pl.BlockSpec((1, d), lambda i: (0, 0))],
        out_specs=pl.BlockSpec((tm, d), lambda i: (i, 0)),
        out_shape=jax.ShapeDtypeStruct((m, d), out_dtype),
        compiler_params=_cparams(("parallel",), 6 * tm * d * 4),
        name="rmsnorm",
    )(x, g.reshape(1, d))


def _row_scale(a_ref):
    a = a_ref[...].astype(F32)
    ms = jnp.mean(a * a, axis=-1, keepdims=True)
    return jnp.broadcast_to(lax.rsqrt(ms + EPS), (a.shape[0], LANES))


def _gained_weight(w, g_ref):
    return (w * jnp.concatenate([g_ref[...]] * (w.shape[1] // LANES), axis=1)).astype(BF16)


def _lanes(x, n):
    return jnp.concatenate([x] * (n // LANES), axis=1)


def _out_proj_kernel(a1_ref, a2_ref, w_ref, g1_ref, g2_ref, res_ref, o_ref, ob_ref, r_ref):
    @pl.when(pl.program_id(1) == 0)
    def _():
        r_ref[0] = _row_scale(a1_ref)
        r_ref[1] = _row_scale(a2_ref)

    k1 = a1_ref.shape[1]
    tn = o_ref.shape[1]
    y1 = _dot(a1_ref[...], _gained_weight(w_ref[:k1, :], g1_ref))
    y2 = _dot(a2_ref[...], _gained_weight(w_ref[k1:, :], g2_ref))
    out = res_ref[...] + (_lanes(r_ref[0], tn) * y1 + _lanes(r_ref[1], tn) * y2)
    o_ref[...] = out
    ob_ref[...] = out.astype(ob_ref.dtype)


def _out_proj(a1, a2, w, layer, g1, g2, res, tm=1024, tn=512):
    m, k1 = a1.shape
    k2 = a2.shape[1]
    n = w.shape[2]
    gain = lambda g: jnp.broadcast_to(g.reshape(-1, 1), (g.shape[0], LANES))
    vmem = (2 * tm * (k1 + k2) * 2 + (k1 + k2) * tn * (2 * 4 + 4 + 2) + 2 * (k1 + k2) * LANES * 4
            + 2 * tm * tn * (4 + 4 + 2) + 3 * tm * tn * 4 + (4 << 20))
    return pl.pallas_call(
        _out_proj_kernel,
        grid=(m // tm, n // tn),
        in_specs=[pl.BlockSpec((tm, k1), lambda i, j: (i, 0)), pl.BlockSpec((tm, k2), lambda i, j: (i, 0)),
                  pl.BlockSpec((None, k1 + k2, tn), lambda i, j: (layer, 0, j)),
                  pl.BlockSpec((k1, LANES), lambda i, j: (0, 0)), pl.BlockSpec((k2, LANES), lambda i, j: (0, 0)),
                  pl.BlockSpec((tm, tn), lambda i, j: (i, j))],
        out_specs=[pl.BlockSpec((tm, tn), lambda i, j: (i, j)), pl.BlockSpec((tm, tn), lambda i, j: (i, j))],
        out_shape=[jax.ShapeDtypeStruct((m, n), F32), jax.ShapeDtypeStruct((m, n), BF16)],
        scratch_shapes=[pltpu.VMEM((2, tm, LANES), F32)],
        compiler_params=_cparams(("parallel", "arbitrary"), vmem),
        name="out_proj",
    )(a1, a2, w, gain(g1), gain(g2), res)


def _ffn_up_kernel(a_ref, wg_ref, wu_ref, g_ref, wd_ref, o_ref, wd_bf16_ref, r_ref):
    @pl.when(pl.program_id(1) == 0)
    def _():
        r_ref[...] = _row_scale(a_ref)

    a = a_ref[...]
    r = _lanes(r_ref[...], o_ref.shape[1])
    gate = r * _dot(a, _gained_weight(wg_ref[...], g_ref))
    up = r * _dot(a, _gained_weight(wu_ref[...], g_ref))
    o_ref[...] = (jax.nn.silu(gate) * up).astype(o_ref.dtype)
    wd_bf16_ref[...] = wd_ref[...].astype(wd_bf16_ref.dtype)


def _ffn_up(a, w_gate, w_up, w_down, layer, g, tm=1024, tn=2 * LANES):
    m, k = a.shape
    n = w_gate.shape[2]
    d_out = w_down.shape[2]
    nj = n // tn
    slab = n // ((m // tm) * nj)
    assert slab * (m // tm) * nj == n and slab % (2 * SUBLANES) == 0
    gain = jnp.broadcast_to(g.reshape(-1, 1), (k, LANES))
    vmem = (2 * tm * k * 2 + 2 * k * tn * (2 * 4 + 4 + 2) + 2 * k * LANES * 4 + 2 * tm * tn * 2 + 5 * tm * tn * 4
            + 2 * slab * d_out * 6 + (4 << 20))
    w_spec = pl.BlockSpec((None, k, tn), lambda i, j: (layer, 0, j))
    return pl.pallas_call(
        _ffn_up_kernel,
        grid=(m // tm, nj),
        in_specs=[pl.BlockSpec((tm, k), lambda i, j: (i, 0)), w_spec, w_spec,
                  pl.BlockSpec((k, LANES), lambda i, j: (0, 0)),
                  pl.BlockSpec((None, slab, d_out), lambda i, j: (layer, i * nj + j, 0))],
        out_specs=[pl.BlockSpec((tm, tn), lambda i, j: (i, j)),
                   pl.BlockSpec((slab, d_out), lambda i, j: (i * nj + j, 0))],
        out_shape=[jax.ShapeDtypeStruct((m, n), BF16), jax.ShapeDtypeStruct((n, d_out), BF16)],
        scratch_shapes=[pltpu.VMEM((tm, LANES), F32)],
        compiler_params=_cparams(("parallel", "arbitrary"), vmem),
        name="ffn_gate_up",
    )(a, w_gate, w_up, gain, w_down)


def _mm_scale_nt_kernel(a_ref, bt_ref, cs_ref, o_ref):
    o_ref[...] = (_dot_nt(a_ref[...], bt_ref[...].astype(BF16)) * cs_ref[...]).astype(o_ref.dtype)


def _mm_sigmoid_kernel(a_ref, b_ref, o_ref):
    o_ref[...] = jax.nn.sigmoid(_dot(a_ref[...], b_ref[...].astype(BF16)))


def _ffn_down_kernel(a_ref, b_ref, res_ref, g_ref, o_ref, acc_ref, *, normalize):
    kk = pl.program_id(1)
    j = pl.program_id(2)
    last_k = pl.num_programs(1) - 1
    tn = b_ref.shape[1]
    part = _dot(a_ref[...], b_ref[...])

    @pl.when(kk == 0)
    def _():
        acc_ref[j] = res_ref[...] + part

    @pl.when(jnp.logical_and(kk > 0, kk < last_k))
    def _():
        acc_ref[j] += part

    @pl.when(kk == last_k)
    def _():
        o_ref[:, pl.ds(pl.multiple_of(j * tn, tn), tn)] = acc_ref[j] + part

    if normalize:
        @pl.when(jnp.logical_and(kk == last_k, j == pl.num_programs(2) - 1))
        def _():
            rows = 8 * SUBLANES
            for r0 in range(0, o_ref.shape[0], rows):
                x = o_ref[r0:r0 + rows, :]
                ms = jnp.mean(x * x, axis=-1, keepdims=True)
                o_ref[r0:r0 + rows, :] = x * lax.rsqrt(ms + EPS) * g_ref[...]


def _ffn_down(a, b, res, gain, tm=512, tn=512):
    m, k = a.shape
    n = b.shape[1]
    tk = k // 2
    assert tk % LANES == 0 and k // tk >= 2
    normalize = gain is not None
    g = (gain if normalize else jnp.ones((n,), F32)).reshape(1, n)
    vmem = 2 * tm * tk * 2 + 2 * tk * tn * 2 + 2 * tm * tn * 4 + 3 * tm * n * 4 + 3 * tm * tn * 4 + (4 << 20)
    return pl.pallas_call(
        functools.partial(_ffn_down_kernel, normalize=normalize),
        grid=(m // tm, k // tk, n // tn),
        in_specs=[pl.BlockSpec((tm, tk), lambda i, kk, j: (i, kk)),
                  pl.BlockSpec((tk, tn), lambda i, kk, j: (kk, j)),
                  pl.BlockSpec((tm, tn), lambda i, kk, j: (i, jnp.where(kk == 0, j, 0))),
                  pl.BlockSpec((1, n), lambda i, kk, j: (0, 0))],
        out_specs=pl.BlockSpec((tm, n), lambda i, kk, j: (i, 0)),
        out_shape=jax.ShapeDtypeStruct((m, n), F32),
        scratch_shapes=[pltpu.VMEM((n // tn, tm, tn), F32)],
        compiler_params=_cparams(("parallel", "arbitrary", "arbitrary"), vmem),
        name="ffn_down",
    )(a, b, res, g)


def _matmul_fullk(kernel_fn, a, bs, layer, n, extras, out_dtype, tm, tn, name, transposed=False):
    m, k = a.shape
    in_specs = [pl.BlockSpec((tm, k), lambda i, j: (i, 0))]
    if transposed:
        in_specs += [pl.BlockSpec((None, tn, k), lambda i, j: (layer, j, 0)) for _ in bs]
    else:
        in_specs += [pl.BlockSpec((None, k, tn), lambda i, j: (layer, 0, j)) for _ in bs]
    w_bytes = bs[0].dtype.itemsize
    vmem = 2 * tm * k * 2 + len(bs) * k * tn * (2 * w_bytes + 2) + 2 * tm * tn * 4 + (2 + len(bs)) * tm * tn * 4
    args = [a, *bs]
    for arr, kind in extras:
        if kind == "col":
            in_specs.append(pl.BlockSpec((1, tn), lambda i, j: (0, j)))
        else:
            in_specs.append(pl.BlockSpec((tm, tn), lambda i, j: (i, j)))
            vmem += 2 * tm * tn * 4
        args.append(arr)
    return pl.pallas_call(
        kernel_fn,
        grid=(m // tm, n // tn),
        in_specs=in_specs,
        out_specs=pl.BlockSpec((tm, tn), lambda i, j: (i, j)),
        out_shape=jax.ShapeDtypeStruct((m, n), out_dtype),
        compiler_params=_cparams(("parallel", "parallel"), vmem + (4 << 20)),
        name=name,
    )(*args)


def _sb_kernel(q_ref, k_ref, v_ref, u_ref, o_ref, carry_ref, acc_ref, *, hg):
    step = pl.program_id(1)
    u = u_ref[...]
    heads = [slice(h * HEAD_DIM, (h + 1) * HEAD_DIM) for h in range(hg)]

    def logs(z):
        log1p_e = jnp.log(1.0 + jnp.exp(-jnp.abs(z)))
        log_sig = jnp.minimum(z, 0.0) - log1p_e
        return log_sig, log_sig - z

    def tile_sums(log_not, n):
        hi, lo = _split_bf16(log_not)
        return [jnp.concatenate([hi[:, t * TK:(t + 1) * TK], lo[:, t * TK:(t + 1) * TK]], axis=1) for t in range(n)]

    def window(sub):
        qt = step * SB_QT + sub
        rows = slice(sub * TQ, (sub + 1) * TQ)
        base = jnp.maximum(qt - (SB_WIN - 1), 0)
        row0 = pl.multiple_of(base * TK, TK)
        wk = SB_WIN * TK
        key_pos = base * TK + lax.broadcasted_iota(jnp.int32, (TQ, wk), 1)
        visible = key_pos < qt * TQ + lax.broadcasted_iota(jnp.int32, (TQ, wk), 0)
        zs = [jnp.where(visible, _dot_nt(q_ref[rows, cs], k_ref[pl.ds(row0, wk), cs]), NEG_INF) for cs in heads]
        lg = [logs(z) for z in zs]
        lhs = [part for (_, log_not) in lg for part in tile_sums(log_not, SB_WIN)]
        sums = _by_row_halves(_dot, jnp.concatenate(lhs, axis=0), u)
        worst = None
        for h, cs in enumerate(heads):
            carry = None
            shifted = [None] * SB_WIN
            for t in reversed(range(SB_WIN)):
                blk = sums[(h * SB_WIN + t) * TQ:(h * SB_WIN + t + 1) * TQ]
                tail, total = blk[:, :TK], blk[:, TK:]
                shifted[t] = tail if carry is None else tail + carry
                carry = total if carry is None else carry + total
            a = jnp.exp(lg[h][0] + jnp.concatenate(shifted, axis=1))
            acc_ref[sub * hg + h] = _dot(a.astype(BF16), v_ref[pl.ds(row0, wk), cs])
            carry_ref[sub * hg + h] = carry
            worst = carry if worst is None else jnp.maximum(worst, carry)
        return base, jnp.max(worst)

    def far_sweep(sub, base, worst):
        rows = slice(sub * TQ, (sub + 1) * TQ)

        def tile(kb):
            r0 = pl.multiple_of(kb * TK, TK)
            lg1 = [logs(_dot_nt(q_ref[rows, cs], k_ref[pl.ds(r0, TK), cs])) for cs in heads]
            sums1 = _dot(jnp.concatenate([tile_sums(log_not, 1)[0] for (_, log_not) in lg1], axis=0), u)
            far = None
            for h, cs in enumerate(heads):
                blk = sums1[h * TQ:(h + 1) * TQ]
                carry = carry_ref[sub * hg + h]
                a = jnp.exp(lg1[h][0] + blk[:, :TK] + carry)
                acc_ref[sub * hg + h] += _dot(a.astype(BF16), v_ref[pl.ds(r0, TK), cs])
                carry = carry + blk[:, TK:]
                carry_ref[sub * hg + h] = carry
                far = carry if far is None else jnp.maximum(far, carry)
            return jnp.max(far)

        def cond(c):
            kb, far = c
            return jnp.logical_and(kb >= 0, far > SB_DEAD_LOG)

        def body(c):
            kb, _ = c
            return kb - 1, tile(kb)

        lax.while_loop(cond, body, (base - 1, worst))

    starts = [window(sub) for sub in range(SB_QT)]
    for sub, (base, worst) in enumerate(starts):
        far_sweep(sub, base, worst)
    for sub in range(SB_QT):
        for h, cs in enumerate(heads):
            o_ref[sub * TQ:(sub + 1) * TQ, cs] = acc_ref[sub * hg + h].astype(o_ref.dtype)


def _sb_attention(proj, hg=4):
    s = proj.shape[0]
    w = hg * HEAD_DIM
    nblk = SB_W // w
    tri = np.arange(TK)[:, None] > np.arange(TK)[None, :]
    half = np.concatenate([tri, np.ones((TK, TK), bool)], axis=1)
    u = jnp.asarray(np.concatenate([half, half], axis=0), dtype=BF16)
    tq = SB_QT * TQ
    vmem = 2 * 2 * s * w * 2 + 2 * tq * w * 2 + 2 * tq * w * 4 + 2 * SB_QT * hg * TQ * TK * 4 + (16 << 20)
    return pl.pallas_call(
        functools.partial(_sb_kernel, hg=hg),
        grid=(nblk, s // tq),
        in_specs=[pl.BlockSpec((tq, w), lambda g, i: (i, g)),
                  pl.BlockSpec((s, w), lambda g, i: (0, nblk + g)),
                  pl.BlockSpec((s, w), lambda g, i: (0, 2 * nblk + g)),
                  pl.BlockSpec((2 * TK, 2 * TK), lambda g, i: (0, 0))],
        out_specs=pl.BlockSpec((tq, w), lambda g, i: (i, g)),
        out_shape=jax.ShapeDtypeStruct((s, SB_W), BF16),
        scratch_shapes=[pltpu.VMEM((SB_QT * hg, TQ, TK), F32), pltpu.VMEM((SB_QT * hg, TQ, HEAD_DIM), F32)],
        compiler_params=_cparams(("parallel", "arbitrary"), vmem),
        name="sb_attention",
    )(proj, proj, proj, u)


def _compress_kernel(r_ref, pos_ref, w1_ref, w2_ref, o_ref):
    r = r_ref[0]
    half = r.shape[1]
    first = _dot(r, w1_ref[:half, :])
    second = _dot(r, w1_ref[half:, :])
    second = pltpu.roll(second, second.shape[0] - 1, 0)
    pos_term = _dot(pos_ref[...].astype(BF16), w1_ref[...])[0:1, :]
    hidden = jax.nn.gelu(first + second + pos_term)
    o_ref[0] = _dot(hidden.astype(BF16), w2_ref[...]).astype(o_ref.dtype)


def _compress(r, pos, w1, w2):
    g, n, half = r.shape
    pos_rows = jnp.zeros((SUBLANES, 2 * half), F32).at[0].set(pos.reshape(-1))
    return pl.pallas_call(
        _compress_kernel,
        grid=(g,),
        in_specs=[pl.BlockSpec((1, n, half), lambda i: (i, 0, 0)),
                  pl.BlockSpec((SUBLANES, 2 * half), lambda i: (0, 0)),
                  pl.BlockSpec((2 * half, CMP_HIDDEN), lambda i: (0, 0)),
                  pl.BlockSpec((CMP_HIDDEN, HEAD_DIM), lambda i: (0, 0))],
        out_specs=pl.BlockSpec((1, n, HEAD_DIM), lambda i: (i, 0, 0)),
        out_shape=jax.ShapeDtypeStruct((g, n, HEAD_DIM), BF16),
        compiler_params=_cparams(("parallel",), 24 << 20),
        name="nsa_compress",
    )(r, pos_rows, w1.astype(BF16), w2.astype(BF16))


def _stack_heads(q_ref, qs_ref, group=0):
    for h in range(NSA_HPG):
        r0 = (group * NSA_HPG + h) * TQ
        c0 = (group * NSA_HPG + h) * HEAD_DIM
        qs_ref[r0:r0 + TQ, :] = q_ref[:, c0:c0 + HEAD_DIM]


def _nsa_cmp_kernel(slopes_ref, q_ref, kc_ref, vc_ref, msel_ref, oc_ref, sel_ref, flag_ref, qs_ref, *, n_top):
    i = pl.program_id(0)
    n_cmp = kc_ref.shape[1]
    n_slc = msel_ref.shape[1]
    t_rel = lax.broadcasted_iota(jnp.int32, (TQ, n_cmp), 0)
    end_rel = lax.broadcasted_iota(jnp.int32, (TQ, n_cmp), 1) * STRIDE_CMP + (L_CMP - 1) - i * TQ
    bias = jnp.where(end_rel <= t_rel, 0.0, NEG_INF)
    end_row = end_rel[0:1, :].astype(F32)
    t_col = i * TQ + lax.broadcasted_iota(jnp.int32, (TQ, 1), 0)
    row_valid = jnp.where(t_col >= L_CMP - 1, 1.0, 0.0)
    blk = lax.broadcasted_iota(jnp.int32, (n_slc, TQ), 0)
    cur = lax.shift_right_logical(i * TQ + lax.broadcasted_iota(jnp.int32, (n_slc, TQ), 1), LOG2_L_SLC)
    valid = blk <= cur
    bonus = jnp.where(blk == 0, FORCE_BONUS, 0.0)
    bonus = jnp.where(blk == cur, FORCE_BONUS, bonus)
    bonus = jnp.where(blk == cur - 1, FORCE_BONUS, bonus)
    blk_f = blk.astype(F32)
    msel = msel_ref[...]
    rows = NSA_HPG * TQ
    for g in range(NSA_KV):
        _stack_heads(q_ref, qs_ref, g)
        z = _by_row_halves(_dot_nt, qs_ref[g * rows:(g + 1) * rows, :], kc_ref[g])
        p_sum = jnp.zeros((TQ, n_cmp), F32)
        ps = []
        for h in range(NSA_HPG):
            s = z[h * TQ:(h + 1) * TQ] + (bias + slopes_ref[g * NSA_HPG + h] * end_row)
            m = jnp.max(s, axis=-1, keepdims=True)
            e = jnp.exp(s - m)
            p = e * (row_valid / jnp.sum(e, axis=-1, keepdims=True))
            ps.append(p.astype(BF16))
            p_sum = p_sum + p
        oc = _by_row_halves(_dot, jnp.concatenate(ps, axis=0), vc_ref[g])
        for h in range(NSA_HPG):
            c0 = (g * NSA_HPG + h) * HEAD_DIM
            oc_ref[:, c0:c0 + HEAD_DIM] = oc[h * TQ:(h + 1) * TQ]
        hi, lo = _split_bf16(p_sum)
        imp = _dot(hi, msel) + _dot(lo, msel)
        score = jnp.where(valid, imp.T + bonus, NEG_INF)
        taken = jnp.zeros((n_slc, TQ), F32)
        for _ in range(n_top):
            best = jnp.max(score, axis=0, keepdims=True)
            first = jnp.min(jnp.where(score == best, blk_f, float(n_slc)), axis=0, keepdims=True)
            hit = blk_f == first
            taken = jnp.where(hit, 1.0, taken)
            score = jnp.where(hit, -jnp.inf, score)
        chosen_q = jnp.where(valid, taken, 0.0).T
        sel_ref[g, 0] = chosen_q.astype(sel_ref.dtype)
        flag_ref[g, 0] = jnp.max(chosen_q, axis=0, keepdims=True)


def _select_matrix(n_cmp_pad, n_slc):
    rs = L_SLC // STRIDE_CMP
    rc = L_CMP // STRIDE_CMP
    m = np.zeros((n_cmp_pad, n_slc), np.float32)
    for j in range(n_slc):
        for a in range(rs):
            for b in range(rc):
                src = rs * j - a - b
                if 0 <= src < n_cmp_pad - 1:
                    m[src, j] += 1.0
    return jnp.asarray(m, dtype=BF16)


def _nsa_cmp(proj, k_cmp, v_cmp, slopes):
    s = proj.shape[0]
    n_cmp = k_cmp.shape[1]
    n_slc = s // L_SLC
    nq = s // TQ
    msel = _select_matrix(n_cmp, n_slc)
    qblk = (3 * SB_W) // NSA_W
    return pl.pallas_call(
        functools.partial(_nsa_cmp_kernel, n_top=min(N_TOPK, n_slc)),
        grid=(nq,),
        in_specs=[pl.BlockSpec(memory_space=pltpu.SMEM),
                  pl.BlockSpec((TQ, NSA_W), lambda i: (i, qblk)),
                  pl.BlockSpec((NSA_KV, n_cmp, HEAD_DIM), lambda i: (0, 0, 0)),
                  pl.BlockSpec((NSA_KV, n_cmp, HEAD_DIM), lambda i: (0, 0, 0)),
                  pl.BlockSpec((n_cmp, n_slc), lambda i: (0, 0))],
        out_specs=[pl.BlockSpec((TQ, NSA_W), lambda i: (i, 0)),
                   pl.BlockSpec((NSA_KV, 1, TQ, n_slc), lambda i: (0, i, 0, 0)),
                   pl.BlockSpec((NSA_KV, 1, 1, n_slc), lambda i: (0, i, 0, 0))],
        out_shape=[jax.ShapeDtypeStruct((s, NSA_W), F32),
                   jax.ShapeDtypeStruct((NSA_KV, nq, TQ, n_slc), BF16),
                   jax.ShapeDtypeStruct((NSA_KV, nq, 1, n_slc), F32)],
        scratch_shapes=[pltpu.VMEM((NSA_HEADS * TQ, HEAD_DIM), BF16)],
        compiler_params=_cparams(("parallel",), 40 << 20),
        name="nsa_cmp_select",
    )(slopes, proj, k_cmp, v_cmp, msel)


def _nsa_main_kernel(tiles_ref, count_ref, slopes_ref, q_ref, ks_ref, vs_ref, kw_ref, vw_ref, sel_ref, oc_ref,
                     gate_ref, o_ref, qs_ref, acc_s, m_s):
    g = pl.program_id(0)
    i = pl.program_id(1)
    n_kt = pl.num_programs(1) * TQ // NSA_TK
    _stack_heads(q_ref, qs_ref)
    slopes = [slopes_ref[g * NSA_HPG + h] for h in range(NSA_HPG)]
    t_rel = lax.broadcasted_iota(jnp.int32, (TQ, NSA_TK), 0)
    lane = lax.broadcasted_iota(jnp.int32, (TQ, NSA_TK), 1)
    diag = lax.div(i * TQ, NSA_TK)

    def scores(k_ref, v_ref, pieces, biases):
        z = jnp.concatenate([_by_row_halves(_dot_nt, qs_ref[...], k_ref[pl.ds(r0, n), :]) for r0, n in pieces],
                            axis=1)
        v_aug = jnp.concatenate(
            [jnp.concatenate([v_ref[pl.ds(r0, n), :], jnp.ones((n, HEAD_DIM), BF16)], axis=1) for r0, n in pieces],
            axis=0)
        key_rel = jnp.concatenate(
            [(r0 - i * TQ + lax.broadcasted_iota(jnp.int32, (1, n), 1)).astype(F32) for r0, n in pieces], axis=1)
        bias = jnp.concatenate(biases, axis=1)
        return [z[h * TQ:(h + 1) * TQ] + (bias + slopes[h] * key_rel) for h in range(NSA_HPG)], v_aug

    def attend(k_ref, v_ref, pieces, biases, acc, m_ref, first):
        s_all, v_aug = scores(k_ref, v_ref, pieces, biases)
        width = s_all[0].shape[1]
        ps, alphas = [], []
        for h, s in enumerate(s_all):
            rows = slice(h * TQ, (h + 1) * TQ)
            m_tile = jnp.max(s, axis=-1, keepdims=True)
            if first:
                m_new = jnp.broadcast_to(m_tile, (TQ, LANES))
            else:
                m_old = m_ref[rows]
                m_new = jnp.maximum(m_old, m_tile)
                alphas.append(jnp.exp(m_old - m_new))
            m_ref[rows] = m_new
            ps.append(jnp.exp(s - jnp.concatenate([m_new] * (width // LANES), axis=1)).astype(BF16))
        pv = _by_row_halves(_dot, jnp.concatenate(ps, axis=0), v_aug)
        for h in range(NSA_HPG):
            rows = slice(h * TQ, (h + 1) * TQ)
            if first:
                acc[rows] = pv[rows]
            else:
                acc[rows] = acc[rows] * jnp.concatenate([alphas[h], alphas[h]], axis=1) + pv[rows]

    def tile_rows(kt):
        return pl.multiple_of(kt * NSA_TK, NSA_TK), NSA_TK

    def select_bias(kt, causal):
        n_slc = sel_ref.shape[3]
        blk = lax.broadcasted_iota(jnp.int32, (n_slc, NSA_TK), 0)
        key_blk = kt * (NSA_TK // L_SLC) + lax.shift_right_logical(
            lax.broadcasted_iota(jnp.int32, (n_slc, NSA_TK), 1), LOG2_L_SLC)
        expand = jnp.where(blk == key_blk, 1.0, 0.0).astype(BF16)
        picked = _dot(sel_ref[0, 0], expand)
        if causal:
            picked = jnp.where(kt * NSA_TK - i * TQ + lane <= t_rel, picked, 0.0)
        return jnp.where(picked > 0.5, 0.0, NEG_INF)

    attend(ks_ref, vs_ref, [tile_rows(diag)], [select_bias(diag, True)], acc_s, m_s, True)
    step = g * pl.num_programs(1) + i
    n_far = count_ref[step]

    def pair_body(p, _):
        ta = tiles_ref[step * n_kt + 2 * p]
        tb = tiles_ref[step * n_kt + 2 * p + 1]
        attend(ks_ref, vs_ref, [tile_rows(ta), tile_rows(tb)], [select_bias(ta, False), select_bias(tb, False)],
               acc_s, m_s, False)
        return 0

    lax.fori_loop(0, lax.shift_right_logical(n_far, 1), pair_body, 0)

    @pl.when(lax.rem(n_far, 2) == 1)
    def _():
        tl = tiles_ref[step * n_kt + n_far - 1]
        attend(ks_ref, vs_ref, [tile_rows(tl)], [select_bias(tl, False)], acc_s, m_s, False)

    wn = WINDOW + TQ
    w0 = pl.multiple_of(jnp.maximum(i * TQ - WINDOW, 0), TQ)
    dist = i * TQ + lax.broadcasted_iota(jnp.int32, (TQ, wn), 0) - (w0 + lax.broadcasted_iota(jnp.int32, (TQ, wn), 1))
    w_bias = jnp.where(dist >= 0, jnp.where(dist < WINDOW, 0.0, NEG_INF), NEG_INF)
    s_win, v_win = scores(kw_ref, vw_ref, [(w0, wn)], [w_bias])
    p_win = [jnp.exp(s - jnp.max(s, axis=-1, keepdims=True)).astype(BF16) for s in s_win]
    pv_win = _by_row_halves(_dot, jnp.concatenate(p_win, axis=0), v_win)

    gates = gate_ref[...]
    for h in range(NSA_HPG):
        cs = slice(h * HEAD_DIM, (h + 1) * HEAD_DIM)
        a_s = acc_s[h * TQ:(h + 1) * TQ]
        a_w = pv_win[h * TQ:(h + 1) * TQ]
        o_sel = a_s[:, :HEAD_DIM] / a_s[:, HEAD_DIM:HEAD_DIM + 1]
        o_win = a_w[:, :HEAD_DIM] / a_w[:, HEAD_DIM:HEAD_DIM + 1]
        o_ref[:, cs] = (gates[:, 3 * h:3 * h + 1] * oc_ref[:, cs] + gates[:, 3 * h + 1:3 * h + 2] * o_sel
                        + gates[:, 3 * h + 2:3 * h + 3] * o_win).astype(o_ref.dtype)


def _far_tile_lists(tile_any, s):
    nq = s // TQ
    n_kt = s // NSA_TK
    hit = tile_any.reshape(NSA_KV, nq, n_kt, NSA_TK // L_SLC).max(axis=-1) > 0.5
    kt = jnp.arange(n_kt, dtype=jnp.int32)
    diag = (jnp.arange(nq, dtype=jnp.int32) * TQ) // NSA_TK
    far = jnp.logical_and(hit, kt[None, None, :] < diag[None, :, None])
    tiles = jnp.sort(jnp.where(far, kt[None, None, :], n_kt), axis=-1)
    tiles = jnp.minimum(tiles, n_kt - 1)
    return tiles.reshape(-1), jnp.sum(far, axis=-1, dtype=jnp.int32).reshape(-1)


def _nsa_main(proj, sel, tile_any, o_c, gates, slopes):
    s = proj.shape[0]
    nq = s // TQ
    n_slc = s // L_SLC
    qblk = (3 * SB_W) // GROUP_W
    tiles, counts = _far_tile_lists(tile_any, s)
    kv_spec = lambda cb: pl.BlockSpec((s, HEAD_DIM), lambda g, i, t, c: (0, cb + g))
    rows = NSA_HPG * TQ
    grid_spec = pltpu.PrefetchScalarGridSpec(
        num_scalar_prefetch=2,
        grid=(NSA_KV, nq),
        in_specs=[pl.BlockSpec(memory_space=pltpu.SMEM),
                  pl.BlockSpec((TQ, GROUP_W), lambda g, i, t, c: (i, qblk + g)),
                  kv_spec(CB_KS), kv_spec(CB_VS), kv_spec(CB_KW), kv_spec(CB_VW),
                  pl.BlockSpec((1, 1, TQ, n_slc), lambda g, i, t, c: (g, i, 0, 0)),
                  pl.BlockSpec((TQ, GROUP_W), lambda g, i, t, c: (i, g)),
                  pl.BlockSpec((TQ, LANES), lambda g, i, t, c: (i, g))],
        out_specs=pl.BlockSpec((TQ, GROUP_W), lambda g, i, t, c: (i, g)),
        scratch_shapes=[pltpu.VMEM((rows, HEAD_DIM), BF16),
                        pltpu.VMEM((rows, 2 * HEAD_DIM), F32), pltpu.VMEM((rows, LANES), F32)],
    )
    vmem = 2 * 4 * s * HEAD_DIM * 2 + (24 << 20)
    return pl.pallas_call(
        _nsa_main_kernel,
        grid_spec=grid_spec,
        out_shape=jax.ShapeDtypeStruct((s, NSA_W), BF16),
        compiler_params=_cparams(("parallel", "arbitrary"), vmem),
        name="nsa_select_window",
    )(tiles, counts, slopes, proj, proj, proj, proj, proj, sel, o_c, gates)


def kernel(x, attn_norm, w_in, pos_cmp_k, pos_cmp_v, w_cmp_k1, w_cmp_k2, w_cmp_v1, w_cmp_v2, norm_sb, norm_nsa,
           w_out, ffn_norm, w_gate, w_up, w_down, final_norm):
    batch, s, d_model = x.shape
    assert batch == 1 and s % 1024 == 0 and w_in.shape[2] == MAIN_COLS + GATE_COLS
    depth = w_in.shape[0]
    d_ff = w_gate.shape[2]
    assert d_ff % (2 * LANES) == 0 and (d_ff // 2) % LANES == 0
    scale = HEAD_DIM ** -0.5
    head_idx = jnp.arange(1, NSA_HEADS + 1, dtype=F32)
    slopes = 2.0 ** (-8.0 * head_idx / NSA_HEADS)
    col_scale = jnp.concatenate([jnp.full((SB_W,), scale, F32), jnp.ones((2 * SB_W,), F32),
                                 jnp.full((NSA_W,), scale, F32), jnp.ones((MAIN_COLS - 3 * SB_W - NSA_W,), F32)])
    col_scale = col_scale.reshape(1, MAIN_COLS)
    xs = x[0]
    w_in_t = jnp.swapaxes(w_in, 1, 2)
    for l in range(depth):
        w_g = w_in[l][:, MAIN_COLS:].reshape(d_model, NSA_KV, 3 * NSA_HPG)
        w_g = jnp.pad(w_g, ((0, 0), (0, 0), (0, LANES - 3 * NSA_HPG))).reshape(1, d_model, NSA_KV * LANES)

        h = _rmsnorm(xs, attn_norm[l], BF16)
        proj = _matmul_fullk(_mm_scale_nt_kernel, h, [w_in_t], l, MAIN_COLS, [(col_scale, "col")], BF16, 1024, 512,
                             "in_proj", transposed=True)
        gates = _matmul_fullk(_mm_sigmoid_kernel, h, [w_g], 0, NSA_KV * LANES, [], F32, 1024, NSA_KV * LANES,
                              "gate_proj")

        o_sb = _sb_attention(proj)

        def blocks_view(cb):
            cols = proj[:, cb * HEAD_DIM:(cb + NSA_KV) * HEAD_DIM]
            r = cols.reshape(s // STRIDE_CMP, STRIDE_CMP, NSA_KV, HEAD_DIM).transpose(2, 0, 1, 3)
            return r.reshape(NSA_KV, s // STRIDE_CMP, STRIDE_CMP * HEAD_DIM)

        k_cmp = _compress(blocks_view(CB_KC), pos_cmp_k[l], w_cmp_k1[l], w_cmp_k2[l])
        v_cmp = _compress(blocks_view(CB_VC), pos_cmp_v[l], w_cmp_v1[l], w_cmp_v2[l])
        o_c, sel, tile_any = _nsa_cmp(proj, k_cmp, v_cmp, slopes)
        o_nsa = _nsa_main(proj, sel, tile_any, o_c, gates, slopes)

        xs, xs_bf16 = _out_proj(o_sb, o_nsa, w_out, l, norm_sb[l], norm_nsa[l], xs)
        act, w_down_bf16 = _ffn_up(xs_bf16, w_gate, w_up, w_down, l, ffn_norm[l])
        xs = _ffn_down(act, w_down_bf16, xs, final_norm if l == depth - 1 else None)
    return xs[None]
```

```python
import functools

import numpy as np
import jax
import jax.numpy as jnp
from jax import lax
from jax.experimental import pallas as pl
from jax.experimental.pallas import tpu as pltpu

HEAD_DIM = 128
SB_HEADS = 16
NSA_HEADS = 16
NSA_KV = 2
NSA_HPG = NSA_HEADS // NSA_KV
L_CMP = 32
STRIDE_CMP = 16
CMP_HIDDEN = 256
L_SLC = 64
LOG2_L_SLC = 6
N_TOPK = 16
WINDOW = 512
EPS = 1e-6
NEG_INF = -1e30
FORCE_BONUS = 1e6

LANES = 128
SUBLANES = 8
TQ = 128
TK = 128
SB_WIN = 3
SB_QT = 4
NSA_TK = 256
VMEM_CAP = 56 * 1024 * 1024
SB_DEAD_LOG = -105.0

F32 = jnp.float32
BF16 = jnp.bfloat16

SB_W = SB_HEADS * HEAD_DIM
NSA_W = NSA_HEADS * HEAD_DIM
GROUP_W = NSA_HPG * HEAD_DIM
MAIN_COLS = 3 * SB_W + NSA_W + 3 * 2 * NSA_KV * HEAD_DIM
GATE_COLS = 3 * NSA_HEADS
CB_KC, CB_VC, CB_KS, CB_VS, CB_KW, CB_VW = 64, 66, 68, 70, 72, 74


def _cparams(sem, vmem_bytes):
    return pltpu.CompilerParams(dimension_semantics=sem, vmem_limit_bytes=int(min(vmem_bytes, VMEM_CAP)))


def _dot(a, b):
    return jnp.dot(a, b, preferred_element_type=F32)


def _dot_nt(a, b):
    return lax.dot_general(a, b, (((1,), (1,)), ((), ())), preferred_element_type=F32)


def _by_row_halves(dot_fn, a, b):
    half = a.shape[0] // 2
    return jnp.concatenate([dot_fn(a[:half], b), dot_fn(a[half:], b)], axis=0)


def _split_bf16(x):
    hi = x.astype(BF16)
    lo = (x - hi.astype(F32)).astype(BF16)
    return hi, lo


def _rms_kernel(x_ref, g_ref, o_ref):
    x = x_ref[...]
    ms = jnp.mean(x * x, axis=-1, keepdims=True)
    o_ref[...] = (x * lax.rsqrt(ms + EPS) * g_ref[...]).astype(o_ref.dtype)


def _rmsnorm(x, g, out_dtype, tm=256):
    m, d = x.shape
    return pl.pallas_call(
        _rms_kernel,
        grid=(m // tm,),
        in_specs=[pl.BlockSpec((tm, d), lambda i: (i, 0)), pl.BlockSpec((1, d), lambda i: (0, 0))],
        out_specs=pl.BlockSpec((tm, d), lambda i: (i, 0)),
        out_shape=jax.ShapeDtypeStruct((m, d), out_dtype),
        compiler_params=_cparams(("parallel",), 6 * tm * d * 4),
        name="rmsnorm",
    )(x, g.reshape(1, d))


def _row_scale(a_ref):
    a = a_ref[...].astype(F32)
    ms = jnp.mean(a * a, axis=-1, keepdims=True)
    return jnp.broadcast_to(lax.rsqrt(ms + EPS), (a.shape[0], LANES))


def _gained_weight(w, g_ref):
    return (w * jnp.concatenate([g_ref[...]] * (w.shape[1] // LANES), axis=1)).astype(BF16)


def _lanes(x, n):
    return jnp.concatenate([x] * (n // LANES), axis=1)


def _out_proj_kernel(a1_ref, a2_ref, w_ref, g1_ref, g2_ref, res_ref, o_ref, ob_ref, r_ref):
    @pl.when(pl.program_id(1) == 0)
    def _():
        r_ref[0] = _row_scale(a1_ref)
        r_ref[1] = _row_scale(a2_ref)

    k1 = a1_ref.shape[1]
    tn = o_ref.shape[1]
    y1 = _dot(a1_ref[...], _gained_weight(w_ref[:k1, :], g1_ref))
    y2 = _dot(a2_ref[...], _gained_weight(w_ref[k1:, :], g2_ref))
    out = res_ref[...] + (_lanes(r_ref[0], tn) * y1 + _lanes(r_ref[1], tn) * y2)
    o_ref[...] = out
    ob_ref[...] = out.astype(ob_ref.dtype)


def _out_proj(a1, a2, w, layer, g1, g2, res, tm=1024, tn=512):
    m, k1 = a1.shape
    k2 = a2.shape[1]
    n = w.shape[2]
    gain = lambda g: jnp.broadcast_to(g.reshape(-1, 1), (g.shape[0], LANES))
    vmem = (2 * tm * (k1 + k2) * 2 + (k1 + k2) * tn * (2 * 4 + 4 + 2) + 2 * (k1 + k2) * LANES * 4
            + 2 * tm * tn * (4 + 4 + 2) + 3 * tm * tn * 4 + (4 << 20))
    return pl.pallas_call(
        _out_proj_kernel,
        grid=(m // tm, n // tn),
        in_specs=[pl.BlockSpec((tm, k1), lambda i, j: (i, 0)), pl.BlockSpec((tm, k2), lambda i, j: (i, 0)),
                  pl.BlockSpec((None, k1 + k2, tn), lambda i, j: (layer, 0, j)),
                  pl.BlockSpec((k1, LANES), lambda i, j: (0, 0)), pl.BlockSpec((k2, LANES), lambda i, j: (0, 0)),
                  pl.BlockSpec((tm, tn), lambda i, j: (i, j))],
        out_specs=[pl.BlockSpec((tm, tn), lambda i, j: (i, j)), pl.BlockSpec((tm, tn), lambda i, j: (i, j))],
        out_shape=[jax.ShapeDtypeStruct((m, n), F32), jax.ShapeDtypeStruct((m, n), BF16)],
        scratch_shapes=[pltpu.VMEM((2, tm, LANES), F32)],
        compiler_params=_cparams(("parallel", "arbitrary"), vmem),
        name="out_proj",
    )(a1, a2, w, gain(g1), gain(g2), res)


def _ffn_up_kernel(a_ref, wg_ref, wu_ref, g_ref, wd_ref, o_ref, wd_bf16_ref, r_ref):
    @pl.when(pl.program_id(1) == 0)
    def _():
        r_ref[...] = _row_scale(a_ref)

    a = a_ref[...]
    r = _lanes(r_ref[...], o_ref.shape[1])
    gate = r * _dot(a, _gained_weight(wg_ref[...], g_ref))
    up = r * _dot(a, _gained_weight(wu_ref[...], g_ref))
    o_ref[...] = (jax.nn.silu(gate) * up).astype(o_ref.dtype)
    wd_bf16_ref[...] = wd_ref[...].astype(wd_bf16_ref.dtype)


def _ffn_up(a, w_gate, w_up, w_down, layer, g, tm=1024, tn=2 * LANES):
    m, k = a.shape
    n = w_gate.shape[2]
    d_out = w_down.shape[2]
    nj = n // tn
    slab = n // ((m // tm) * nj)
    assert slab * (m // tm) * nj == n and slab % (2 * SUBLANES) == 0
    gain = jnp.broadcast_to(g.reshape(-1, 1), (k, LANES))
    vmem = (2 * tm * k * 2 + 2 * k * tn * (2 * 4 + 4 + 2) + 2 * k * LANES * 4 + 2 * tm * tn * 2 + 5 * tm * tn * 4
            + 2 * slab * d_out * 6 + (4 << 20))
    w_spec = pl.BlockSpec((None, k, tn), lambda i, j: (layer, 0, j))
    return pl.pallas_call(
        _ffn_up_kernel,
        grid=(m // tm, nj),
        in_specs=[pl.BlockSpec((tm, k), lambda i, j: (i, 0)), w_spec, w_spec,
                  pl.BlockSpec((k, LANES), lambda i, j: (0, 0)),
                  pl.BlockSpec((None, slab, d_out), lambda i, j: (layer, i * nj + j, 0))],
        out_specs=[pl.BlockSpec((tm, tn), lambda i, j: (i, j)),
                   pl.BlockSpec((slab, d_out), lambda i, j: (i * nj + j, 0))],
        out_shape=[jax.ShapeDtypeStruct((m, n), BF16), jax.ShapeDtypeStruct((n, d_out), BF16)],
        scratch_shapes=[pltpu.VMEM((tm, LANES), F32)],
        compiler_params=_cparams(("parallel", "arbitrary"), vmem),
        name="ffn_gate_up",
    )(a, w_gate, w_up, gain, w_down)


def _mm_scale_nt_kernel(a_ref, bt_ref, cs_ref, o_ref):
    o_ref[...] = (_dot_nt(a_ref[...], bt_ref[...].astype(BF16)) * cs_ref[...]).astype(o_ref.dtype)


def _mm_sigmoid_kernel(a_ref, b_ref, o_ref):
    o_ref[...] = jax.nn.sigmoid(_dot(a_ref[...], b_ref[...].astype(BF16)))


def _ffn_down_kernel(a_ref, b_ref, res_ref, o_ref, acc_ref):
    kk = pl.program_id(2)

    @pl.when(kk == 0)
    def _():
        acc_ref[...] = res_ref[...]

    acc_ref[...] += _dot(a_ref[...], b_ref[...])

    @pl.when(kk == pl.num_programs(2) - 1)
    def _():
        o_ref[...] = acc_ref[...]


def _ffn_down(a, b, res, tm=1024, tn=512):
    m, k = a.shape
    n = b.shape[1]
    tk = k // 2
    assert tk % LANES == 0
    vmem = 2 * tm * tk * 2 + 2 * tk * tn * 2 + 6 * tm * tn * 4 + (4 << 20)
    return pl.pallas_call(
        _ffn_down_kernel,
        grid=(m // tm, n // tn, k // tk),
        in_specs=[pl.BlockSpec((tm, tk), lambda i, j, kk: (i, kk)),
                  pl.BlockSpec((tk, tn), lambda i, j, kk: (kk, j)),
                  pl.BlockSpec((tm, tn), lambda i, j, kk: (i, j))],
        out_specs=pl.BlockSpec((tm, tn), lambda i, j, kk: (i, j)),
        out_shape=jax.ShapeDtypeStruct((m, n), F32),
        scratch_shapes=[pltpu.VMEM((tm, tn), F32)],
        compiler_params=_cparams(("parallel", "parallel", "arbitrary"), vmem),
        name="ffn_down",
    )(a, b, res)


def _matmul_fullk(kernel_fn, a, bs, layer, n, extras, out_dtype, tm, tn, name, transposed=False):
    m, k = a.shape
    in_specs = [pl.BlockSpec((tm, k), lambda i, j: (i, 0))]
    if transposed:
        in_specs += [pl.BlockSpec((None, tn, k), lambda i, j: (layer, j, 0)) for _ in bs]
    else:
        in_specs += [pl.BlockSpec((None, k, tn), lambda i, j: (layer, 0, j)) for _ in bs]
    w_bytes = bs[0].dtype.itemsize
    vmem = 2 * tm * k * 2 + len(bs) * k * tn * (2 * w_bytes + 2) + 2 * tm * tn * 4 + (2 + len(bs)) * tm * tn * 4
    args = [a, *bs]
    for arr, kind in extras:
        if kind == "col":
            in_specs.append(pl.BlockSpec((1, tn), lambda i, j: (0, j)))
        else:
            in_specs.append(pl.BlockSpec((tm, tn), lambda i, j: (i, j)))
            vmem += 2 * tm * tn * 4
        args.append(arr)
    return pl.pallas_call(
        kernel_fn,
        grid=(m // tm, n // tn),
        in_specs=in_specs,
        out_specs=pl.BlockSpec((tm, tn), lambda i, j: (i, j)),
        out_shape=jax.ShapeDtypeStruct((m, n), out_dtype),
        compiler_params=_cparams(("parallel", "parallel"), vmem + (4 << 20)),
        name=name,
    )(*args)


def _sb_kernel(q_ref, k_ref, v_ref, u_ref, o_ref, carry_ref, acc_ref, *, hg):
    step = pl.program_id(1)
    u = u_ref[...]
    heads = [slice(h * HEAD_DIM, (h + 1) * HEAD_DIM) for h in range(hg)]

    def logs(z):
        log1p_e = jnp.log(1.0 + jnp.exp(-jnp.abs(z)))
        log_sig = jnp.minimum(z, 0.0) - log1p_e
        return log_sig, log_sig - z

    def tile_sums(log_not, n):
        hi, lo = _split_bf16(log_not)
        return [jnp.concatenate([hi[:, t * TK:(t + 1) * TK], lo[:, t * TK:(t + 1) * TK]], axis=1) for t in range(n)]

    def window(sub):
        qt = step * SB_QT + sub
        rows = slice(sub * TQ, (sub + 1) * TQ)
        base = jnp.maximum(qt - (SB_WIN - 1), 0)
        row0 = pl.multiple_of(base * TK, TK)
        wk = SB_WIN * TK
        key_pos = base * TK + lax.broadcasted_iota(jnp.int32, (TQ, wk), 1)
        visible = key_pos < qt * TQ + lax.broadcasted_iota(jnp.int32, (TQ, wk), 0)
        zs = [jnp.where(visible, _dot_nt(q_ref[rows, cs], k_ref[pl.ds(row0, wk), cs]), NEG_INF) for cs in heads]
        lg = [logs(z) for z in zs]
        lhs = [part for (_, log_not) in lg for part in tile_sums(log_not, SB_WIN)]
        sums = _by_row_halves(_dot, jnp.concatenate(lhs, axis=0), u)
        worst = None
        for h, cs in enumerate(heads):
            carry = None
            shifted = [None] * SB_WIN
            for t in reversed(range(SB_WIN)):
                blk = sums[(h * SB_WIN + t) * TQ:(h * SB_WIN + t + 1) * TQ]
                tail, total = blk[:, :TK], blk[:, TK:]
                shifted[t] = tail if carry is None else tail + carry
                carry = total if carry is None else carry + total
            a = jnp.exp(lg[h][0] + jnp.concatenate(shifted, axis=1))
            acc_ref[sub * hg + h] = _dot(a.astype(BF16), v_ref[pl.ds(row0, wk), cs])
            carry_ref[sub * hg + h] = carry
            worst = carry if worst is None else jnp.maximum(worst, carry)
        return base, jnp.max(worst)

    def far_sweep(sub, base, worst):
        rows = slice(sub * TQ, (sub + 1) * TQ)

        def tile(kb):
            r0 = pl.multiple_of(kb * TK, TK)
            lg1 = [logs(_dot_nt(q_ref[rows, cs], k_ref[pl.ds(r0, TK), cs])) for cs in heads]
            sums1 = _dot(jnp.concatenate([tile_sums(log_not, 1)[0] for (_, log_not) in lg1], axis=0), u)
            far = None
            for h, cs in enumerate(heads):
                blk = sums1[h * TQ:(h + 1) * TQ]
                carry = carry_ref[sub * hg + h]
                a = jnp.exp(lg1[h][0] + blk[:, :TK] + carry)
                acc_ref[sub * hg + h] += _dot(a.astype(BF16), v_ref[pl.ds(r0, TK), cs])
                carry = carry + blk[:, TK:]
                carry_ref[sub * hg + h] = carry
                far = carry if far is None else jnp.maximum(far, carry)
            return jnp.max(far)

        def cond(c):
            kb, far = c
            return jnp.logical_and(kb >= 0, far > SB_DEAD_LOG)

        def body(c):
            kb, _ = c
            return kb - 1, tile(kb)

        lax.while_loop(cond, body, (base - 1, worst))

    starts = [window(sub) for sub in range(SB_QT)]
    for sub, (base, worst) in enumerate(starts):
        far_sweep(sub, base, worst)
    for sub in range(SB_QT):
        for h, cs in enumerate(heads):
            o_ref[sub * TQ:(sub + 1) * TQ, cs] = acc_ref[sub * hg + h].astype(o_ref.dtype)


def _sb_attention(proj, hg=4):
    s = proj.shape[0]
    w = hg * HEAD_DIM
    nblk = SB_W // w
    tri = np.arange(TK)[:, None] > np.arange(TK)[None, :]
    half = np.concatenate([tri, np.ones((TK, TK), bool)], axis=1)
    u = jnp.asarray(np.concatenate([half, half], axis=0), dtype=BF16)
    tq = SB_QT * TQ
    vmem = 2 * 2 * s * w * 2 + 2 * tq * w * 2 + 2 * tq * w * 4 + 2 * SB_QT * hg * TQ * TK * 4 + (16 << 20)
    return pl.pallas_call(
        functools.partial(_sb_kernel, hg=hg),
        grid=(nblk, s // tq),
        in_specs=[pl.BlockSpec((tq, w), lambda g, i: (i, g)),
                  pl.BlockSpec((s, w), lambda g, i: (0, nblk + g)),
                  pl.BlockSpec((s, w), lambda g, i: (0, 2 * nblk + g)),
                  pl.BlockSpec((2 * TK, 2 * TK), lambda g, i: (0, 0))],
        out_specs=pl.BlockSpec((tq, w), lambda g, i: (i, g)),
        out_shape=jax.ShapeDtypeStruct((s, SB_W), BF16),
        scratch_shapes=[pltpu.VMEM((SB_QT * hg, TQ, TK), F32), pltpu.VMEM((SB_QT * hg, TQ, HEAD_DIM), F32)],
        compiler_params=_cparams(("parallel", "arbitrary"), vmem),
        name="sb_attention",
    )(proj, proj, proj, u)


def _compress_kernel(r_ref, pos_ref, w1_ref, w2_ref, o_ref):
    r = r_ref[0]
    half = r.shape[1]
    first = _dot(r, w1_ref[:half, :])
    second = _dot(r, w1_ref[half:, :])
    second = pltpu.roll(second, second.shape[0] - 1, 0)
    pos_term = _dot(pos_ref[...].astype(BF16), w1_ref[...])[0:1, :]
    hidden = jax.nn.gelu(first + second + pos_term)
    o_ref[0] = _dot(hidden.astype(BF16), w2_ref[...]).astype(o_ref.dtype)


def _compress(r, pos, w1, w2):
    g, n, half = r.shape
    pos_rows = jnp.zeros((SUBLANES, 2 * half), F32).at[0].set(pos.reshape(-1))
    return pl.pallas_call(
        _compress_kernel,
        grid=(g,),
        in_specs=[pl.BlockSpec((1, n, half), lambda i: (i, 0, 0)),
                  pl.BlockSpec((SUBLANES, 2 * half), lambda i: (0, 0)),
                  pl.BlockSpec((2 * half, CMP_HIDDEN), lambda i: (0, 0)),
                  pl.BlockSpec((CMP_HIDDEN, HEAD_DIM), lambda i: (0, 0))],
        out_specs=pl.BlockSpec((1, n, HEAD_DIM), lambda i: (i, 0, 0)),
        out_shape=jax.ShapeDtypeStruct((g, n, HEAD_DIM), BF16),
        compiler_params=_cparams(("parallel",), 24 << 20),
        name="nsa_compress",
    )(r, pos_rows, w1.astype(BF16), w2.astype(BF16))


def _stack_heads(q_ref, qs_ref, group=0):
    for h in range(NSA_HPG):
        r0 = (group * NSA_HPG + h) * TQ
        c0 = (group * NSA_HPG + h) * HEAD_DIM
        qs_ref[r0:r0 + TQ, :] = q_ref[:, c0:c0 + HEAD_DIM]


def _nsa_cmp_kernel(slopes_ref, q_ref, kc_ref, vc_ref, msel_ref, oc_ref, sel_ref, flag_ref, qs_ref, *, n_top):
    i = pl.program_id(0)
    n_cmp = kc_ref.shape[1]
    n_slc = msel_ref.shape[1]
    t_rel = lax.broadcasted_iota(jnp.int32, (TQ, n_cmp), 0)
    end_rel = lax.broadcasted_iota(jnp.int32, (TQ, n_cmp), 1) * STRIDE_CMP + (L_CMP - 1) - i * TQ
    bias = jnp.where(end_rel <= t_rel, 0.0, NEG_INF)
    end_row = end_rel[0:1, :].astype(F32)
    t_col = i * TQ + lax.broadcasted_iota(jnp.int32, (TQ, 1), 0)
    row_valid = jnp.where(t_col >= L_CMP - 1, 1.0, 0.0)
    blk = lax.broadcasted_iota(jnp.int32, (n_slc, TQ), 0)
    cur = lax.shift_right_logical(i * TQ + lax.broadcasted_iota(jnp.int32, (n_slc, TQ), 1), LOG2_L_SLC)
    valid = blk <= cur
    bonus = jnp.where(blk == 0, FORCE_BONUS, 0.0)
    bonus = jnp.where(blk == cur, FORCE_BONUS, bonus)
    bonus = jnp.where(blk == cur - 1, FORCE_BONUS, bonus)
    blk_f = blk.astype(F32)
    msel = msel_ref[...]
    rows = NSA_HPG * TQ
    for g in range(NSA_KV):
        _stack_heads(q_ref, qs_ref, g)
        z = _by_row_halves(_dot_nt, qs_ref[g * rows:(g + 1) * rows, :], kc_ref[g])
        p_sum = jnp.zeros((TQ, n_cmp), F32)
        ps = []
        for h in range(NSA_HPG):
            s = z[h * TQ:(h + 1) * TQ] + (bias + slopes_ref[g * NSA_HPG + h] * end_row)
            m = jnp.max(s, axis=-1, keepdims=True)
            e = jnp.exp(s - m)
            p = e * (row_valid / jnp.sum(e, axis=-1, keepdims=True))
            ps.append(p.astype(BF16))
            p_sum = p_sum + p
        oc = _by_row_halves(_dot, jnp.concatenate(ps, axis=0), vc_ref[g])
        for h in range(NSA_HPG):
            c0 = (g * NSA_HPG + h) * HEAD_DIM
            oc_ref[:, c0:c0 + HEAD_DIM] = oc[h * TQ:(h + 1) * TQ]
        hi, lo = _split_bf16(p_sum)
        imp = _dot(hi, msel) + _dot(lo, msel)
        score = jnp.where(valid, imp.T + bonus, NEG_INF)
        taken = jnp.zeros((n_slc, TQ), F32)
        for _ in range(n_top):
            best = jnp.max(score, axis=0, keepdims=True)
            first = jnp.min(jnp.where(score == best, blk_f, float(n_slc)), axis=0, keepdims=True)
            hit = blk_f == first
            taken = jnp.where(hit, 1.0, taken)
            score = jnp.where(hit, -jnp.inf, score)
        chosen_q = jnp.where(valid, taken, 0.0).T
        sel_ref[g, 0] = chosen_q.astype(sel_ref.dtype)
        flag_ref[g, 0] = jnp.max(chosen_q, axis=0, keepdims=True)


def _select_matrix(n_cmp_pad, n_slc):
    rs = L_SLC // STRIDE_CMP
    rc = L_CMP // STRIDE_CMP
    m = np.zeros((n_cmp_pad, n_slc), np.float32)
    for j in range(n_slc):
        for a in range(rs):
            for b in range(rc):
                src = rs * j - a - b
                if 0 <= src < n_cmp_pad - 1:
                    m[src, j] += 1.0
    return jnp.asarray(m, dtype=BF16)


def _nsa_cmp(proj, k_cmp, v_cmp, slopes):
    s = proj.shape[0]
    n_cmp = k_cmp.shape[1]
    n_slc = s // L_SLC
    nq = s // TQ
    msel = _select_matrix(n_cmp, n_slc)
    qblk = (3 * SB_W) // NSA_W
    return pl.pallas_call(
        functools.partial(_nsa_cmp_kernel, n_top=min(N_TOPK, n_slc)),
        grid=(nq,),
        in_specs=[pl.BlockSpec(memory_space=pltpu.SMEM),
                  pl.BlockSpec((TQ, NSA_W), lambda i: (i, qblk)),
                  pl.BlockSpec((NSA_KV, n_cmp, HEAD_DIM), lambda i: (0, 0, 0)),
                  pl.BlockSpec((NSA_KV, n_cmp, HEAD_DIM), lambda i: (0, 0, 0)),
                  pl.BlockSpec((n_cmp, n_slc), lambda i: (0, 0))],
        out_specs=[pl.BlockSpec((TQ, NSA_W), lambda i: (i, 0)),
                   pl.BlockSpec((NSA_KV, 1, TQ, n_slc), lambda i: (0, i, 0, 0)),
                   pl.BlockSpec((NSA_KV, 1, 1, n_slc), lambda i: (0, i, 0, 0))],
        out_shape=[jax.ShapeDtypeStruct((s, NSA_W), F32),
                   jax.ShapeDtypeStruct((NSA_KV, nq, TQ, n_slc), BF16),
                   jax.ShapeDtypeStruct((NSA_KV, nq, 1, n_slc), F32)],
        scratch_shapes=[pltpu.VMEM((NSA_HEADS * TQ, HEAD_DIM), BF16)],
        compiler_params=_cparams(("parallel",), 40 << 20),
        name="nsa_cmp_select",
    )(slopes, proj, k_cmp, v_cmp, msel)


def _nsa_main_kernel(tiles_ref, count_ref, slopes_ref, q_ref, ks_ref, vs_ref, kw_ref, vw_ref, sel_ref, oc_ref,
                     gate_ref, o_ref, qs_ref, acc_s, m_s):
    g = pl.program_id(0)
    i = pl.program_id(1)
    n_kt = pl.num_programs(1) * TQ // NSA_TK
    _stack_heads(q_ref, qs_ref)
    slopes = [slopes_ref[g * NSA_HPG + h] for h in range(NSA_HPG)]
    t_rel = lax.broadcasted_iota(jnp.int32, (TQ, NSA_TK), 0)
    lane = lax.broadcasted_iota(jnp.int32, (TQ, NSA_TK), 1)
    diag = lax.div(i * TQ, NSA_TK)

    def scores(k_ref, v_ref, pieces, biases):
        z = jnp.concatenate([_by_row_halves(_dot_nt, qs_ref[...], k_ref[pl.ds(r0, n), :]) for r0, n in pieces],
                            axis=1)
        v_aug = jnp.concatenate(
            [jnp.concatenate([v_ref[pl.ds(r0, n), :], jnp.ones((n, HEAD_DIM), BF16)], axis=1) for r0, n in pieces],
            axis=0)
        key_rel = jnp.concatenate(
            [(r0 - i * TQ + lax.broadcasted_iota(jnp.int32, (1, n), 1)).astype(F32) for r0, n in pieces], axis=1)
        bias = jnp.concatenate(biases, axis=1)
        return [z[h * TQ:(h + 1) * TQ] + (bias + slopes[h] * key_rel) for h in range(NSA_HPG)], v_aug

    def attend(k_ref, v_ref, pieces, biases, acc, m_ref, first):
        s_all, v_aug = scores(k_ref, v_ref, pieces, biases)
        width = s_all[0].shape[1]
        ps, alphas = [], []
        for h, s in enumerate(s_all):
            rows = slice(h * TQ, (h + 1) * TQ)
            m_tile = jnp.max(s, axis=-1, keepdims=True)
            if first:
                m_new = jnp.broadcast_to(m_tile, (TQ, LANES))
            else:
                m_old = m_ref[rows]
                m_new = jnp.maximum(m_old, m_tile)
                alphas.append(jnp.exp(m_old - m_new))
            m_ref[rows] = m_new
            ps.append(jnp.exp(s - jnp.concatenate([m_new] * (width // LANES), axis=1)).astype(BF16))
        pv = _by_row_halves(_dot, jnp.concatenate(ps, axis=0), v_aug)
        for h in range(NSA_HPG):
            rows = slice(h * TQ, (h + 1) * TQ)
            if first:
                acc[rows] = pv[rows]
            else:
                acc[rows] = acc[rows] * jnp.concatenate([alphas[h], alphas[h]], axis=1) + pv[rows]

    def tile_rows(kt):
        return pl.multiple_of(kt * NSA_TK, NSA_TK), NSA_TK

    def select_bias(kt, causal):
        n_slc = sel_ref.shape[3]
        blk = lax.broadcasted_iota(jnp.int32, (n_slc, NSA_TK), 0)
        key_blk = kt * (NSA_TK // L_SLC) + lax.shift_right_logical(
            lax.broadcasted_iota(jnp.int32, (n_slc, NSA_TK), 1), LOG2_L_SLC)
        expand = jnp.where(blk == key_blk, 1.0, 0.0).astype(BF16)
        picked = _dot(sel_ref[0, 0], expand)
        if causal:
            picked = jnp.where(kt * NSA_TK - i * TQ + lane <= t_rel, picked, 0.0)
        return jnp.where(picked > 0.5, 0.0, NEG_INF)

    attend(ks_ref, vs_ref, [tile_rows(diag)], [select_bias(diag, True)], acc_s, m_s, True)
    step = g * pl.num_programs(1) + i
    n_far = count_ref[step]

    def pair_body(p, _):
        ta = tiles_ref[step * n_kt + 2 * p]
        tb = tiles_ref[step * n_kt + 2 * p + 1]
        attend(ks_ref, vs_ref, [tile_rows(ta), tile_rows(tb)], [select_bias(ta, False), select_bias(tb, False)],
               acc_s, m_s, False)
        return 0

    lax.fori_loop(0, lax.shift_right_logical(n_far, 1), pair_body, 0)

    @pl.when(lax.rem(n_far, 2) == 1)
    def _():
        tl = tiles_ref[step * n_kt + n_far - 1]
        attend(ks_ref, vs_ref, [tile_rows(tl)], [select_bias(tl, False)], acc_s, m_s, False)

    wn = WINDOW + TQ
    w0 = pl.multiple_of(jnp.maximum(i * TQ - WINDOW, 0), TQ)
    dist = i * TQ + lax.broadcasted_iota(jnp.int32, (TQ, wn), 0) - (w0 + lax.broadcasted_iota(jnp.int32, (TQ, wn), 1))
    w_bias = jnp.where(dist >= 0, jnp.where(dist < WINDOW, 0.0, NEG_INF), NEG_INF)
    s_win, v_win = scores(kw_ref, vw_ref, [(w0, wn)], [w_bias])
    p_win = [jnp.exp(s - jnp.max(s, axis=-1, keepdims=True)).astype(BF16) for s in s_win]
    pv_win = _by_row_halves(_dot, jnp.concatenate(p_win, axis=0), v_win)

    gates = gate_ref[...]
    for h in range(NSA_HPG):
        cs = slice(h * HEAD_DIM, (h + 1) * HEAD_DIM)
        a_s = acc_s[h * TQ:(h + 1) * TQ]
        a_w = pv_win[h * TQ:(h + 1) * TQ]
        o_sel = a_s[:, :HEAD_DIM] / a_s[:, HEAD_DIM:HEAD_DIM + 1]
        o_win = a_w[:, :HEAD_DIM] / a_w[:, HEAD_DIM:HEAD_DIM + 1]
        o_ref[:, cs] = (gates[:, 3 * h:3 * h + 1] * oc_ref[:, cs] + gates[:, 3 * h + 1:3 * h + 2] * o_sel
                        + gates[:, 3 * h + 2:3 * h + 3] * o_win).astype(o_ref.dtype)


def _far_tile_lists(tile_any, s):
    nq = s // TQ
    n_kt = s // NSA_TK
    hit = tile_any.reshape(NSA_KV, nq, n_kt, NSA_TK // L_SLC).max(axis=-1) > 0.5
    kt = jnp.arange(n_kt, dtype=jnp.int32)
    diag = (jnp.arange(nq, dtype=jnp.int32) * TQ) // NSA_TK
    far = jnp.logical_and(hit, kt[None, None, :] < diag[None, :, None])
    tiles = jnp.sort(jnp.where(far, kt[None, None, :], n_kt), axis=-1)
    tiles = jnp.minimum(tiles, n_kt - 1)
    return tiles.reshape(-1), jnp.sum(far, axis=-1, dtype=jnp.int32).reshape(-1)


def _nsa_main(proj, sel, tile_any, o_c, gates, slopes):
    s = proj.shape[0]
    nq = s // TQ
    n_slc = s // L_SLC
    qblk = (3 * SB_W) // GROUP_W
    tiles, counts = _far_tile_lists(tile_any, s)
    kv_spec = lambda cb: pl.BlockSpec((s, HEAD_DIM), lambda g, i, t, c: (0, cb + g))
    rows = NSA_HPG * TQ
    grid_spec = pltpu.PrefetchScalarGridSpec(
        num_scalar_prefetch=2,
        grid=(NSA_KV, nq),
        in_specs=[pl.BlockSpec(memory_space=pltpu.SMEM),
                  pl.BlockSpec((TQ, GROUP_W), lambda g, i, t, c: (i, qblk + g)),
                  kv_spec(CB_KS), kv_spec(CB_VS), kv_spec(CB_KW), kv_spec(CB_VW),
                  pl.BlockSpec((1, 1, TQ, n_slc), lambda g, i, t, c: (g, i, 0, 0)),
                  pl.BlockSpec((TQ, GROUP_W), lambda g, i, t, c: (i, g)),
                  pl.BlockSpec((TQ, LANES), lambda g, i, t, c: (i, g))],
        out_specs=pl.BlockSpec((TQ, GROUP_W), lambda g, i, t, c: (i, g)),
        scratch_shapes=[pltpu.VMEM((rows, HEAD_DIM), BF16),
                        pltpu.VMEM((rows, 2 * HEAD_DIM), F32), pltpu.VMEM((rows, LANES), F32)],
    )
    vmem = 2 * 4 * s * HEAD_DIM * 2 + (24 << 20)
    return pl.pallas_call(
        _nsa_main_kernel,
        grid_spec=grid_spec,
        out_shape=jax.ShapeDtypeStruct((s, NSA_W), BF16),
        compiler_params=_cparams(("parallel", "arbitrary"), vmem),
        name="nsa_select_window",
    )(tiles, counts, slopes, proj, proj, proj, proj, proj, sel, o_c, gates)


def kernel(x, attn_norm, w_in, pos_cmp_k, pos_cmp_v, w_cmp_k1, w_cmp_k2, w_cmp_v1, w_cmp_v2, norm_sb, norm_nsa,
           w_out, ffn_norm, w_gate, w_up, w_down, final_norm):
    batch, s, d_model = x.shape
    assert batch == 1 and s % 1024 == 0 and w_in.shape[2] == MAIN_COLS + GATE_COLS
    depth = w_in.shape[0]
    d_ff = w_gate.shape[2]
    assert d_ff % (2 * LANES) == 0 and (d_ff // 2) % LANES == 0
    scale = HEAD_DIM ** -0.5
    head_idx = jnp.arange(1, NSA_HEADS + 1, dtype=F32)
    slopes = 2.0 ** (-8.0 * head_idx / NSA_HEADS)
    col_scale = jnp.concatenate([jnp.full((SB_W,), scale, F32), jnp.ones((2 * SB_W,), F32),
                                 jnp.full((NSA_W,), scale, F32), jnp.ones((MAIN_COLS - 3 * SB_W - NSA_W,), F32)])
    col_scale = col_scale.reshape(1, MAIN_COLS)
    xs = x[0]
    w_in_t = jnp.swapaxes(w_in, 1, 2)
    for l in range(depth):
        w_g = w_in[l][:, MAIN_COLS:].reshape(d_model, NSA_KV, 3 * NSA_HPG)
        w_g = jnp.pad(w_g, ((0, 0), (0, 0), (0, LANES - 3 * NSA_HPG))).reshape(1, d_model, NSA_KV * LANES)

        h = _rmsnorm(xs, attn_norm[l], BF16)
        proj = _matmul_fullk(_mm_scale_nt_kernel, h, [w_in_t], l, MAIN_COLS, [(col_scale, "col")], BF16, 1024, 512,
                             "in_proj", transposed=True)
        gates = _matmul_fullk(_mm_sigmoid_kernel, h, [w_g], 0, NSA_KV * LANES, [], F32, 1024, NSA_KV * LANES,
                              "gate_proj")

        o_sb = _sb_attention(proj)

        def blocks_view(cb):
            cols = proj[:, cb * HEAD_DIM:(cb + NSA_KV) * HEAD_DIM]
            r = cols.reshape(s // STRIDE_CMP, STRIDE_CMP, NSA_KV, HEAD_DIM).transpose(2, 0, 1, 3)
            return r.reshape(NSA_KV, s // STRIDE_CMP, STRIDE_CMP * HEAD_DIM)

        k_cmp = _compress(blocks_view(CB_KC), pos_cmp_k[l], w_cmp_k1[l], w_cmp_k2[l])
        v_cmp = _compress(blocks_view(CB_VC), pos_cmp_v[l], w_cmp_v1[l], w_cmp_v2[l])
        o_c, sel, tile_any = _nsa_cmp(proj, k_cmp, v_cmp, slopes)
        o_nsa = _nsa_main(proj, sel, tile_any, o_c, gates, slopes)

        xs, xs_bf16 = _out_proj(o_sb, o_nsa, w_out, l, norm_sb[l], norm_nsa[l], xs)
        act, w_down_bf16 = _ffn_up(xs_bf16, w_gate, w_up, w_down, l, ffn_norm[l])
        xs = _ffn_down(act, w_down_bf16, xs)
    return _rmsnorm(xs, final_norm, F32)[None]
```

```python
import functools

import numpy as np
import jax
import jax.numpy as jnp
from jax import lax
from jax.experimental import pallas as pl
from jax.experimental.pallas import tpu as pltpu

HEAD_DIM = 128
SB_HEADS = 16
NSA_HEADS = 16
NSA_KV = 2
NSA_HPG = NSA_HEADS // NSA_KV
L_CMP = 32
STRIDE_CMP = 16
CMP_HIDDEN = 256
L_SLC = 64
LOG2_L_SLC = 6
N_TOPK = 16
WINDOW = 512
EPS = 1e-6
NEG_INF = -1e30
FORCE_BONUS = 1e6

LANES = 128
SUBLANES = 8
TQ = 128
TK = 128
SB_WIN = 3
SB_QT = 8
NSA_TK = 256
VMEM_CAP = 56 * 1024 * 1024
SB_DEAD_LOG = -105.0

F32 = jnp.float32
BF16 = jnp.bfloat16

SB_W = SB_HEADS * HEAD_DIM
NSA_W = NSA_HEADS * HEAD_DIM
GROUP_W = NSA_HPG * HEAD_DIM
MAIN_COLS = 3 * SB_W + NSA_W + 3 * 2 * NSA_KV * HEAD_DIM
GATE_COLS = 3 * NSA_HEADS
CB_KC, CB_VC, CB_KS, CB_VS, CB_KW, CB_VW = 64, 66, 68, 70, 72, 74


def _cparams(sem, vmem_bytes):
    return pltpu.CompilerParams(dimension_semantics=sem, vmem_limit_bytes=int(min(vmem_bytes, VMEM_CAP)))


def _dot(a, b):
    return jnp.dot(a, b, preferred_element_type=F32)


def _dot_nt(a, b):
    return lax.dot_general(a, b, (((1,), (1,)), ((), ())), preferred_element_type=F32)


def _by_row_halves(dot_fn, a, b):
    half = a.shape[0] // 2
    return jnp.concatenate([dot_fn(a[:half], b), dot_fn(a[half:], b)], axis=0)


def _split_bf16(x):
    hi = x.astype(BF16)
    lo = (x - hi.astype(F32)).astype(BF16)
    return hi, lo


def _rms_kernel(x_ref, g_ref, o_ref):
    x = x_ref[...]
    ms = jnp.mean(x * x, axis=-1, keepdims=True)
    o_ref[...] = (x * lax.rsqrt(ms + EPS) * g_ref[...]).astype(o_ref.dtype)


def _rmsnorm(x, g, out_dtype, tm=256):
    m, d = x.shape
    return pl.pallas_call(
        _rms_kernel,
        grid=(m // tm,),
        in_specs=[pl.BlockSpec((tm, d), lambda i: (i, 0)), pl.BlockSpec((1, d), lambda i: (0, 0))],
        out_specs=pl.BlockSpec((tm, d), lambda i: (i, 0)),
        out_shape=jax.ShapeDtypeStruct((m, d), out_dtype),
        compiler_params=_cparams(("parallel",), 6 * tm * d * 4),
        name="rmsnorm",
    )(x, g.reshape(1, d))


def _row_scale(a_ref):
    a = a_ref[...].astype(F32)
    ms = jnp.mean(a * a, axis=-1, keepdims=True)
    return jnp.broadcast_to(lax.rsqrt(ms + EPS), (a.shape[0], LANES))


def _gained_weight(w, g_ref):
    return (w * jnp.concatenate([g_ref[...]] * (w.shape[1] // LANES), axis=1)).astype(BF16)


def _lanes(x, n):
    return jnp.concatenate([x] * (n // LANES), axis=1)


def _out_proj_kernel(a1_ref, a2_ref, w_ref, g1_ref, g2_ref, res_ref, o_ref, ob_ref, r_ref):
    @pl.when(pl.program_id(1) == 0)
    def _():
        r_ref[0] = _row_scale(a1_ref)
        r_ref[1] = _row_scale(a2_ref)

    k1 = a1_ref.shape[1]
    tn = o_ref.shape[1]
    y1 = _dot(a1_ref[...], _gained_weight(w_ref[:k1, :], g1_ref))
    y2 = _dot(a2_ref[...], _gained_weight(w_ref[k1:, :], g2_ref))
    out = res_ref[...] + (_lanes(r_ref[0], tn) * y1 + _lanes(r_ref[1], tn) * y2)
    o_ref[...] = out
    ob_ref[...] = out.astype(ob_ref.dtype)


def _out_proj(a1, a2, w, layer, g1, g2, res, tm=1024, tn=512):
    m, k1 = a1.shape
    k2 = a2.shape[1]
    n = w.shape[2]
    gain = lambda g: jnp.broadcast_to(g.reshape(-1, 1), (g.shape[0], LANES))
    vmem = (2 * tm * (k1 + k2) * 2 + (k1 + k2) * tn * (2 * 4 + 4 + 2) + 2 * (k1 + k2) * LANES * 4
            + 2 * tm * tn * (4 + 4 + 2) + 3 * tm * tn * 4 + (4 << 20))
    return pl.pallas_call(
        _out_proj_kernel,
        grid=(m // tm, n // tn),
        in_specs=[pl.BlockSpec((tm, k1), lambda i, j: (i, 0)), pl.BlockSpec((tm, k2), lambda i, j: (i, 0)),
                  pl.BlockSpec((None, k1 + k2, tn), lambda i, j: (layer, 0, j)),
                  pl.BlockSpec((k1, LANES), lambda i, j: (0, 0)), pl.BlockSpec((k2, LANES), lambda i, j: (0, 0)),
                  pl.BlockSpec((tm, tn), lambda i, j: (i, j))],
        out_specs=[pl.BlockSpec((tm, tn), lambda i, j: (i, j)), pl.BlockSpec((tm, tn), lambda i, j: (i, j))],
        out_shape=[jax.ShapeDtypeStruct((m, n), F32), jax.ShapeDtypeStruct((m, n), BF16)],
        scratch_shapes=[pltpu.VMEM((2, tm, LANES), F32)],
        compiler_params=_cparams(("parallel", "arbitrary"), vmem),
        name="out_proj",
    )(a1, a2, w, gain(g1), gain(g2), res)


def _ffn_up_kernel(a_ref, wg_ref, wu_ref, g_ref, wd_ref, o_ref, wd_bf16_ref, r_ref):
    @pl.when(pl.program_id(1) == 0)
    def _():
        r_ref[...] = _row_scale(a_ref)

    a = a_ref[...]
    r = _lanes(r_ref[...], o_ref.shape[1])
    gate = r * _dot(a, _gained_weight(wg_ref[...], g_ref))
    up = r * _dot(a, _gained_weight(wu_ref[...], g_ref))
    o_ref[...] = (jax.nn.silu(gate) * up).astype(o_ref.dtype)
    wd_bf16_ref[...] = wd_ref[...].astype(wd_bf16_ref.dtype)


def _ffn_up(a, w_gate, w_up, w_down, layer, g, tm=1024, tn=2 * LANES):
    m, k = a.shape
    n = w_gate.shape[2]
    d_out = w_down.shape[2]
    nj = n // tn
    slab = n // ((m // tm) * nj)
    assert slab * (m // tm) * nj == n and slab % (2 * SUBLANES) == 0
    gain = jnp.broadcast_to(g.reshape(-1, 1), (k, LANES))
    vmem = (2 * tm * k * 2 + 2 * k * tn * (2 * 4 + 4 + 2) + 2 * k * LANES * 4 + 2 * tm * tn * 2 + 5 * tm * tn * 4
            + 2 * slab * d_out * 6 + (4 << 20))
    w_spec = pl.BlockSpec((None, k, tn), lambda i, j: (layer, 0, j))
    return pl.pallas_call(
        _ffn_up_kernel,
        grid=(m // tm, nj),
        in_specs=[pl.BlockSpec((tm, k), lambda i, j: (i, 0)), w_spec, w_spec,
                  pl.BlockSpec((k, LANES), lambda i, j: (0, 0)),
                  pl.BlockSpec((None, slab, d_out), lambda i, j: (layer, i * nj + j, 0))],
        out_specs=[pl.BlockSpec((tm, tn), lambda i, j: (i, j)),
                   pl.BlockSpec((slab, d_out), lambda i, j: (i * nj + j, 0))],
        out_shape=[jax.ShapeDtypeStruct((m, n), BF16), jax.ShapeDtypeStruct((n, d_out), BF16)],
        scratch_shapes=[pltpu.VMEM((tm, LANES), F32)],
        compiler_params=_cparams(("parallel", "arbitrary"), vmem),
        name="ffn_gate_up",
    )(a, w_gate, w_up, gain, w_down)


def _mm_scale_nt_kernel(a_ref, bt_ref, cs_ref, o_ref):
    o_ref[...] = (_dot_nt(a_ref[...], bt_ref[...].astype(BF16)) * cs_ref[...]).astype(o_ref.dtype)


def _mm_sigmoid_kernel(a_ref, b_ref, o_ref):
    o_ref[...] = jax.nn.sigmoid(_dot(a_ref[...], b_ref[...].astype(BF16)))


def _ffn_down_kernel(a_ref, b_ref, res_ref, o_ref, acc_ref):
    kk = pl.program_id(2)

    @pl.when(kk == 0)
    def _():
        acc_ref[...] = res_ref[...]

    acc_ref[...] += _dot(a_ref[...], b_ref[...])

    @pl.when(kk == pl.num_programs(2) - 1)
    def _():
        o_ref[...] = acc_ref[...]


def _ffn_down(a, b, res, tm=1024, tn=512):
    m, k = a.shape
    n = b.shape[1]
    tk = k // 2
    assert tk % LANES == 0
    vmem = 2 * tm * tk * 2 + 2 * tk * tn * 2 + 6 * tm * tn * 4 + (4 << 20)
    return pl.pallas_call(
        _ffn_down_kernel,
        grid=(m // tm, n // tn, k // tk),
        in_specs=[pl.BlockSpec((tm, tk), lambda i, j, kk: (i, kk)),
                  pl.BlockSpec((tk, tn), lambda i, j, kk: (kk, j)),
                  pl.BlockSpec((tm, tn), lambda i, j, kk: (i, j))],
        out_specs=pl.BlockSpec((tm, tn), lambda i, j, kk: (i, j)),
        out_shape=jax.ShapeDtypeStruct((m, n), F32),
        scratch_shapes=[pltpu.VMEM((tm, tn), F32)],
        compiler_params=_cparams(("parallel", "parallel", "arbitrary"), vmem),
        name="ffn_down",
    )(a, b, res)


def _matmul_fullk(kernel_fn, a, bs, layer, n, extras, out_dtype, tm, tn, name, transposed=False):
    m, k = a.shape
    in_specs = [pl.BlockSpec((tm, k), lambda i, j: (i, 0))]
    if transposed:
        in_specs += [pl.BlockSpec((None, tn, k), lambda i, j: (layer, j, 0)) for _ in bs]
    else:
        in_specs += [pl.BlockSpec((None, k, tn), lambda i, j: (layer, 0, j)) for _ in bs]
    w_bytes = bs[0].dtype.itemsize
    vmem = 2 * tm * k * 2 + len(bs) * k * tn * (2 * w_bytes + 2) + 2 * tm * tn * 4 + (2 + len(bs)) * tm * tn * 4
    args = [a, *bs]
    for arr in extras:
        in_specs.append(pl.BlockSpec((1, tn), lambda i, j: (0, j)))
        args.append(arr)
    return pl.pallas_call(
        kernel_fn,
        grid=(m // tm, n // tn),
        in_specs=in_specs,
        out_specs=pl.BlockSpec((tm, tn), lambda i, j: (i, j)),
        out_shape=jax.ShapeDtypeStruct((m, n), out_dtype),
        compiler_params=_cparams(("parallel", "parallel"), vmem + (4 << 20)),
        name=name,
    )(*args)


def _sb_kernel(q_ref, k_ref, v_ref, u_ref, o_ref, carry_ref, acc_ref, *, hg):
    step = pl.program_id(1)
    u = u_ref[...]
    heads = [slice(h * HEAD_DIM, (h + 1) * HEAD_DIM) for h in range(hg)]

    def logs(z):
        log1p_e = jnp.log(1.0 + jnp.exp(-jnp.abs(z)))
        log_sig = jnp.minimum(z, 0.0) - log1p_e
        return log_sig, log_sig - z

    def tile_sums(log_not, n):
        hi, lo = _split_bf16(log_not)
        return [jnp.concatenate([hi[:, t * TK:(t + 1) * TK], lo[:, t * TK:(t + 1) * TK]], axis=1) for t in range(n)]

    def window(sub):
        qt = step * SB_QT + sub
        rows = slice(sub * TQ, (sub + 1) * TQ)
        base = jnp.maximum(qt - (SB_WIN - 1), 0)
        row0 = pl.multiple_of(base * TK, TK)
        wk = SB_WIN * TK
        key_pos = base * TK + lax.broadcasted_iota(jnp.int32, (TQ, wk), 1)
        visible = key_pos < qt * TQ + lax.broadcasted_iota(jnp.int32, (TQ, wk), 0)
        zs = [jnp.where(visible, _dot_nt(q_ref[rows, cs], k_ref[pl.ds(row0, wk), cs]), NEG_INF) for cs in heads]
        lg = [logs(z) for z in zs]
        lhs = [part for (_, log_not) in lg for part in tile_sums(log_not, SB_WIN)]
        sums = _by_row_halves(_dot, jnp.concatenate(lhs, axis=0), u)
        worst = None
        for h, cs in enumerate(heads):
            carry = None
            shifted = [None] * SB_WIN
            for t in reversed(range(SB_WIN)):
                blk = sums[(h * SB_WIN + t) * TQ:(h * SB_WIN + t + 1) * TQ]
                tail, total = blk[:, :TK], blk[:, TK:]
                shifted[t] = tail if carry is None else tail + carry
                carry = total if carry is None else carry + total
            a = jnp.exp(lg[h][0] + jnp.concatenate(shifted, axis=1))
            acc_ref[sub * hg + h] = _dot(a.astype(BF16), v_ref[pl.ds(row0, wk), cs])
            carry_ref[sub * hg + h] = carry
            worst = carry if worst is None else jnp.maximum(worst, carry)
        return base, jnp.max(worst)

    def far_sweep(sub, base, worst):
        rows = slice(sub * TQ, (sub + 1) * TQ)

        def tile(kb):
            r0 = pl.multiple_of(kb * TK, TK)
            lg1 = [logs(_dot_nt(q_ref[rows, cs], k_ref[pl.ds(r0, TK), cs])) for cs in heads]
            sums1 = _dot(jnp.concatenate([tile_sums(log_not, 1)[0] for (_, log_not) in lg1], axis=0), u)
            far = None
            for h, cs in enumerate(heads):
                blk = sums1[h * TQ:(h + 1) * TQ]
                carry = carry_ref[sub * hg + h]
                a = jnp.exp(lg1[h][0] + blk[:, :TK] + carry)
                acc_ref[sub * hg + h] += _dot(a.astype(BF16), v_ref[pl.ds(r0, TK), cs])
                carry = carry + blk[:, TK:]
                carry_ref[sub * hg + h] = carry
                far = carry if far is None else jnp.maximum(far, carry)
            return jnp.max(far)

        def cond(c):
            kb, far = c
            return jnp.logical_and(kb >= 0, far > SB_DEAD_LOG)

        def body(c):
            kb, _ = c
            return kb - 1, tile(kb)

        lax.while_loop(cond, body, (base - 1, worst))

    starts = [window(sub) for sub in range(SB_QT)]
    for sub, (base, worst) in enumerate(starts):
        far_sweep(sub, base, worst)
    for sub in range(SB_QT):
        for h, cs in enumerate(heads):
            o_ref[sub * TQ:(sub + 1) * TQ, cs] = acc_ref[sub * hg + h].astype(o_ref.dtype)


def _sb_attention(proj, hg=4):
    s = proj.shape[0]
    w = hg * HEAD_DIM
    nblk = SB_W // w
    tri = np.arange(TK)[:, None] > np.arange(TK)[None, :]
    half = np.concatenate([tri, np.ones((TK, TK), bool)], axis=1)
    u = jnp.asarray(np.concatenate([half, half], axis=0), dtype=BF16)
    tq = SB_QT * TQ
    vmem = 2 * 2 * s * w * 2 + 2 * tq * w * 2 + 2 * tq * w * 4 + 2 * SB_QT * hg * TQ * TK * 4 + (16 << 20)
    return pl.pallas_call(
        functools.partial(_sb_kernel, hg=hg),
        grid=(nblk, s // tq),
        in_specs=[pl.BlockSpec((tq, w), lambda g, i: (i, g)),
                  pl.BlockSpec((s, w), lambda g, i: (0, nblk + g)),
                  pl.BlockSpec((s, w), lambda g, i: (0, 2 * nblk + g)),
                  pl.BlockSpec((2 * TK, 2 * TK), lambda g, i: (0, 0))],
        out_specs=pl.BlockSpec((tq, w), lambda g, i: (i, g)),
        out_shape=jax.ShapeDtypeStruct((s, SB_W), BF16),
        scratch_shapes=[pltpu.VMEM((SB_QT * hg, TQ, TK), F32), pltpu.VMEM((SB_QT * hg, TQ, HEAD_DIM), F32)],
        compiler_params=_cparams(("parallel", "arbitrary"), vmem),
        name="sb_attention",
    )(proj, proj, proj, u)


def _compress_kernel(r_ref, pos_ref, w1_ref, w2_ref, o_ref):
    r = r_ref[0]
    half = r.shape[1]
    first = _dot(r, w1_ref[:half, :])
    second = _dot(r, w1_ref[half:, :])
    second = pltpu.roll(second, second.shape[0] - 1, 0)
    pos_term = _dot(pos_ref[...].astype(BF16), w1_ref[...])[0:1, :]
    hidden = jax.nn.gelu(first + second + pos_term)
    o_ref[0] = _dot(hidden.astype(BF16), w2_ref[...]).astype(o_ref.dtype)


def _compress(r, pos, w1, w2):
    g, n, half = r.shape
    pos_rows = jnp.zeros((SUBLANES, 2 * half), F32).at[0].set(pos.reshape(-1))
    return pl.pallas_call(
        _compress_kernel,
        grid=(g,),
        in_specs=[pl.BlockSpec((1, n, half), lambda i: (i, 0, 0)),
                  pl.BlockSpec((SUBLANES, 2 * half), lambda i: (0, 0)),
                  pl.BlockSpec((2 * half, CMP_HIDDEN), lambda i: (0, 0)),
                  pl.BlockSpec((CMP_HIDDEN, HEAD_DIM), lambda i: (0, 0))],
        out_specs=pl.BlockSpec((1, n, HEAD_DIM), lambda i: (i, 0, 0)),
        out_shape=jax.ShapeDtypeStruct((g, n, HEAD_DIM), BF16),
        compiler_params=_cparams(("parallel",), 24 << 20),
        name="nsa_compress",
    )(r, pos_rows, w1.astype(BF16), w2.astype(BF16))


def _stack_heads(q_ref, qs_ref, group=0):
    for h in range(NSA_HPG):
        r0 = (group * NSA_HPG + h) * TQ
        c0 = (group * NSA_HPG + h) * HEAD_DIM
        qs_ref[r0:r0 + TQ, :] = q_ref[:, c0:c0 + HEAD_DIM]


def _nsa_cmp_kernel(slopes_ref, q_ref, kc_ref, vc_ref, msel_ref, oc_ref, sel_ref, flag_ref, qs_ref, *, n_top):
    i = pl.program_id(0)
    n_cmp = kc_ref.shape[1]
    n_slc = msel_ref.shape[1]
    t_rel = lax.broadcasted_iota(jnp.int32, (TQ, n_cmp), 0)
    end_rel = lax.broadcasted_iota(jnp.int32, (TQ, n_cmp), 1) * STRIDE_CMP + (L_CMP - 1) - i * TQ
    bias = jnp.where(end_rel <= t_rel, 0.0, NEG_INF)
    end_row = end_rel[0:1, :].astype(F32)
    t_col = i * TQ + lax.broadcasted_iota(jnp.int32, (TQ, 1), 0)
    row_valid = jnp.where(t_col >= L_CMP - 1, 1.0, 0.0)
    blk = lax.broadcasted_iota(jnp.int32, (n_slc, TQ), 0)
    cur = lax.shift_right_logical(i * TQ + lax.broadcasted_iota(jnp.int32, (n_slc, TQ), 1), LOG2_L_SLC)
    valid = blk <= cur
    bonus = jnp.where(blk == 0, FORCE_BONUS, 0.0)
    bonus = jnp.where(blk == cur, FORCE_BONUS, bonus)
    bonus = jnp.where(blk == cur - 1, FORCE_BONUS, bonus)
    blk_f = blk.astype(F32)
    msel = msel_ref[...]
    rows = NSA_HPG * TQ
    for g in range(NSA_KV):
        _stack_heads(q_ref, qs_ref, g)
        z = _by_row_halves(_dot_nt, qs_ref[g * rows:(g + 1) * rows, :], kc_ref[g])
        p_sum = jnp.zeros((TQ, n_cmp), F32)
        ps = []
        for h in range(NSA_HPG):
            s = z[h * TQ:(h + 1) * TQ] + (bias + slopes_ref[g * NSA_HPG + h] * end_row)
            m = jnp.max(s, axis=-1, keepdims=True)
            e = jnp.exp(s - m)
            p = e * (row_valid / jnp.sum(e, axis=-1, keepdims=True))
            ps.append(p.astype(BF16))
            p_sum = p_sum + p
        oc = _by_row_halves(_dot, jnp.concatenate(ps, axis=0), vc_ref[g])
        for h in range(NSA_HPG):
            c0 = (g * NSA_HPG + h) * HEAD_DIM
            oc_ref[:, c0:c0 + HEAD_DIM] = oc[h * TQ:(h + 1) * TQ]
        hi, lo = _split_bf16(p_sum)
        imp = _dot(hi, msel) + _dot(lo, msel)
        score = jnp.where(valid, imp.T + bonus, NEG_INF)
        taken = jnp.zeros((n_slc, TQ), F32)
        for _ in range(n_top):
            best = jnp.max(score, axis=0, keepdims=True)
            first = jnp.min(jnp.where(score == best, blk_f, float(n_slc)), axis=0, keepdims=True)
            hit = blk_f == first
            taken = jnp.where(hit, 1.0, taken)
            score = jnp.where(hit, -jnp.inf, score)
        chosen_q = jnp.where(valid, taken, 0.0).T
        sel_ref[g, 0] = chosen_q.astype(sel_ref.dtype)
        flag_ref[g, 0] = jnp.max(chosen_q, axis=0, keepdims=True)


def _select_matrix(n_cmp_pad, n_slc):
    rs = L_SLC // STRIDE_CMP
    rc = L_CMP // STRIDE_CMP
    m = np.zeros((n_cmp_pad, n_slc), np.float32)
    for j in range(n_slc):
        for a in range(rs):
            for b in range(rc):
                src = rs * j - a - b
                if 0 <= src < n_cmp_pad - 1:
                    m[src, j] += 1.0
    return jnp.asarray(m, dtype=BF16)


def _nsa_cmp(proj, k_cmp, v_cmp, slopes):
    s = proj.shape[0]
    n_cmp = k_cmp.shape[1]
    n_slc = s // L_SLC
    nq = s // TQ
    msel = _select_matrix(n_cmp, n_slc)
    qblk = (3 * SB_W) // NSA_W
    return pl.pallas_call(
        functools.partial(_nsa_cmp_kernel, n_top=min(N_TOPK, n_slc)),
        grid=(nq,),
        in_specs=[pl.BlockSpec(memory_space=pltpu.SMEM),
                  pl.BlockSpec((TQ, NSA_W), lambda i: (i, qblk)),
                  pl.BlockSpec((NSA_KV, n_cmp, HEAD_DIM), lambda i: (0, 0, 0)),
                  pl.BlockSpec((NSA_KV, n_cmp, HEAD_DIM), lambda i: (0, 0, 0)),
                  pl.BlockSpec((n_cmp, n_slc), lambda i: (0, 0))],
        out_specs=[pl.BlockSpec((TQ, NSA_W), lambda i: (i, 0)),
                   pl.BlockSpec((NSA_KV, 1, TQ, n_slc), lambda i: (0, i, 0, 0)),
                   pl.BlockSpec((NSA_KV, 1, 1, n_slc), lambda i: (0, i, 0, 0))],
        out_shape=[jax.ShapeDtypeStruct((s, NSA_W), F32),
                   jax.ShapeDtypeStruct((NSA_KV, nq, TQ, n_slc), BF16),
                   jax.ShapeDtypeStruct((NSA_KV, nq, 1, n_slc), F32)],
        scratch_shapes=[pltpu.VMEM((NSA_HEADS * TQ, HEAD_DIM), BF16)],
        compiler_params=_cparams(("parallel",), 40 << 20),
        name="nsa_cmp_select",
    )(slopes, proj, k_cmp, v_cmp, msel)


def _nsa_main_kernel(tiles_ref, count_ref, slopes_ref, q_ref, ks_ref, vs_ref, kw_ref, vw_ref, sel_ref, oc_ref,
                     gate_ref, o_ref, qa_ref, acc_s, m_s):
    g = pl.program_id(0)
    i = pl.program_id(1)
    n_kt = pl.num_programs(1) * TQ // NSA_TK
    n_slc = sel_ref.shape[3]
    not_picked = jnp.where(sel_ref[0, 0].astype(F32) > 0.5, 0.0, NEG_INF).astype(BF16)
    for h in range(NSA_HPG):
        qa_ref[h * TQ:(h + 1) * TQ, :HEAD_DIM] = q_ref[:, h * HEAD_DIM:(h + 1) * HEAD_DIM]
        qa_ref[h * TQ:(h + 1) * TQ, HEAD_DIM:] = not_picked
    slopes = [slopes_ref[g * NSA_HPG + h] for h in range(NSA_HPG)]
    diag = lax.div(i * TQ, NSA_TK)

    def scores(k_ref, v_ref, pieces, bias, select):
        zs = []
        for r0, n in pieces:
            k = k_ref[pl.ds(r0, n), :]
            if select:
                blk_of_key = lax.shift_right_logical(r0 + lax.broadcasted_iota(jnp.int32, (n, n_slc), 0), LOG2_L_SLC)
                one_hot = jnp.where(lax.broadcasted_iota(jnp.int32, (n, n_slc), 1) == blk_of_key, 1.0, 0.0)
                zs.append(_by_row_halves(_dot_nt, qa_ref[...], jnp.concatenate([k, one_hot.astype(BF16)], axis=1)))
            else:
                zs.append(_by_row_halves(_dot_nt, qa_ref[:, :HEAD_DIM], k))
        z = jnp.concatenate(zs, axis=1)
        v_aug = jnp.concatenate(
            [jnp.concatenate([v_ref[pl.ds(r0, n), :], jnp.ones((n, HEAD_DIM), BF16)], axis=1) for r0, n in pieces],
            axis=0)
        key_rel = jnp.concatenate(
            [(r0 - i * TQ + lax.broadcasted_iota(jnp.int32, (1, n), 1)).astype(F32) for r0, n in pieces], axis=1)
        if bias is None:
            return [z[h * TQ:(h + 1) * TQ] + slopes[h] * key_rel for h in range(NSA_HPG)], v_aug
        return [z[h * TQ:(h + 1) * TQ] + (bias + slopes[h] * key_rel) for h in range(NSA_HPG)], v_aug

    def attend(pieces, bias, first):
        s_all, v_aug = scores(ks_ref, vs_ref, pieces, bias, True)
        width = s_all[0].shape[1]
        ps, alphas = [], []
        for h, s in enumerate(s_all):
            rows = slice(h * TQ, (h + 1) * TQ)
            m_tile = jnp.max(s, axis=-1, keepdims=True)
            if first:
                m_new = jnp.broadcast_to(m_tile, (TQ, LANES))
            else:
                m_old = m_s[rows]
                m_new = jnp.maximum(m_old, m_tile)
                alphas.append(jnp.exp(m_old - m_new))
            m_s[rows] = m_new
            ps.append(jnp.exp(s - jnp.concatenate([m_new] * (width // LANES), axis=1)).astype(BF16))
        pv = _by_row_halves(_dot, jnp.concatenate(ps, axis=0), v_aug)
        for h in range(NSA_HPG):
            rows = slice(h * TQ, (h + 1) * TQ)
            if first:
                acc_s[rows] = pv[rows]
            else:
                acc_s[rows] = acc_s[rows] * jnp.concatenate([alphas[h], alphas[h]], axis=1) + pv[rows]

    def tile_rows(kt):
        return pl.multiple_of(kt * NSA_TK, NSA_TK), NSA_TK

    near = jnp.maximum(diag - 1, 0)
    near0 = pl.multiple_of(near * NSA_TK, NSA_TK)
    nw = 2 * NSA_TK
    ahead = (near0 + lax.broadcasted_iota(jnp.int32, (TQ, nw), 1)) - (i * TQ + lax.broadcasted_iota(jnp.int32, (TQ, nw), 0))
    attend([(near0, nw)], jnp.where(ahead <= 0, 0.0, NEG_INF), True)
    step = g * pl.num_programs(1) + i
    n_far = count_ref[step]

    def pair_body(p, _):
        ta = tiles_ref[step * n_kt + 2 * p]
        tb = tiles_ref[step * n_kt + 2 * p + 1]
        attend([tile_rows(ta), tile_rows(tb)], None, False)
        return 0

    lax.fori_loop(0, lax.shift_right_logical(n_far, 1), pair_body, 0)

    @pl.when(lax.rem(n_far, 2) == 1)
    def _():
        attend([tile_rows(tiles_ref[step * n_kt + n_far - 1])], None, False)

    wn = WINDOW + TQ
    w0 = pl.multiple_of(jnp.maximum(i * TQ - WINDOW, 0), TQ)
    dist = i * TQ + lax.broadcasted_iota(jnp.int32, (TQ, wn), 0) - (w0 + lax.broadcasted_iota(jnp.int32, (TQ, wn), 1))
    w_bias = jnp.where(dist >= 0, jnp.where(dist < WINDOW, 0.0, NEG_INF), NEG_INF)
    s_win, v_win = scores(kw_ref, vw_ref, [(w0, wn)], w_bias, False)
    p_win = [jnp.exp(s - jnp.max(s, axis=-1, keepdims=True)).astype(BF16) for s in s_win]
    pv_win = _by_row_halves(_dot, jnp.concatenate(p_win, axis=0), v_win)

    gates = gate_ref[...]
    for h in range(NSA_HPG):
        cs = slice(h * HEAD_DIM, (h + 1) * HEAD_DIM)
        a_s = acc_s[h * TQ:(h + 1) * TQ]
        a_w = pv_win[h * TQ:(h + 1) * TQ]
        o_sel = a_s[:, :HEAD_DIM] / a_s[:, HEAD_DIM:HEAD_DIM + 1]
        o_win = a_w[:, :HEAD_DIM] / a_w[:, HEAD_DIM:HEAD_DIM + 1]
        o_ref[:, cs] = (gates[:, 3 * h:3 * h + 1] * oc_ref[:, cs] + gates[:, 3 * h + 1:3 * h + 2] * o_sel
                        + gates[:, 3 * h + 2:3 * h + 3] * o_win).astype(o_ref.dtype)


def _far_tile_lists(tile_any, s):
    nq = s // TQ
    n_kt = s // NSA_TK
    hit = tile_any.reshape(NSA_KV, nq, n_kt, NSA_TK // L_SLC).max(axis=-1) > 0.5
    kt = jnp.arange(n_kt, dtype=jnp.int32)
    diag = (jnp.arange(nq, dtype=jnp.int32) * TQ) // NSA_TK
    far = jnp.logical_and(hit, kt[None, None, :] < diag[None, :, None] - 1)
    tiles = jnp.sort(jnp.where(far, kt[None, None, :], n_kt), axis=-1)
    tiles = jnp.minimum(tiles, n_kt - 1)
    return tiles.reshape(-1), jnp.sum(far, axis=-1, dtype=jnp.int32).reshape(-1)


def _nsa_main(proj, sel, tile_any, o_c, gates, slopes):
    s = proj.shape[0]
    nq = s // TQ
    n_slc = s // L_SLC
    qblk = (3 * SB_W) // GROUP_W
    tiles, counts = _far_tile_lists(tile_any, s)
    kv_spec = lambda cb: pl.BlockSpec((s, HEAD_DIM), lambda g, i, t, c: (0, cb + g))
    rows = NSA_HPG * TQ
    grid_spec = pltpu.PrefetchScalarGridSpec(
        num_scalar_prefetch=2,
        grid=(NSA_KV, nq),
        in_specs=[pl.BlockSpec(memory_space=pltpu.SMEM),
                  pl.BlockSpec((TQ, GROUP_W), lambda g, i, t, c: (i, qblk + g)),
                  kv_spec(CB_KS), kv_spec(CB_VS), kv_spec(CB_KW), kv_spec(CB_VW),
                  pl.BlockSpec((1, 1, TQ, n_slc), lambda g, i, t, c: (g, i, 0, 0)),
                  pl.BlockSpec((TQ, GROUP_W), lambda g, i, t, c: (i, g)),
                  pl.BlockSpec((TQ, LANES), lambda g, i, t, c: (i, g))],
        out_specs=pl.BlockSpec((TQ, GROUP_W), lambda g, i, t, c: (i, g)),
        scratch_shapes=[pltpu.VMEM((rows, HEAD_DIM + n_slc), BF16),
                        pltpu.VMEM((rows, 2 * HEAD_DIM), F32), pltpu.VMEM((rows, LANES), F32)],
    )
    vmem = 2 * 4 * s * HEAD_DIM * 2 + (24 << 20)
    return pl.pallas_call(
        _nsa_main_kernel,
        grid_spec=grid_spec,
        out_shape=jax.ShapeDtypeStruct((s, NSA_W), BF16),
        compiler_params=_cparams(("parallel", "arbitrary"), vmem),
        name="nsa_select_window",
    )(tiles, counts, slopes, proj, proj, proj, proj, proj, sel, o_c, gates)


def kernel(x, attn_norm, w_in, pos_cmp_k, pos_cmp_v, w_cmp_k1, w_cmp_k2, w_cmp_v1, w_cmp_v2, norm_sb, norm_nsa,
           w_out, ffn_norm, w_gate, w_up, w_down, final_norm):
    batch, s, d_model = x.shape
    assert batch == 1 and s % 1024 == 0 and w_in.shape[2] == MAIN_COLS + GATE_COLS
    depth = w_in.shape[0]
    d_ff = w_gate.shape[2]
    assert d_ff % (2 * LANES) == 0 and (d_ff // 2) % LANES == 0
    scale = HEAD_DIM ** -0.5
    head_idx = jnp.arange(1, NSA_HEADS + 1, dtype=F32)
    slopes = 2.0 ** (-8.0 * head_idx / NSA_HEADS)
    col_scale = jnp.concatenate([jnp.full((SB_W,), scale, F32), jnp.ones((2 * SB_W,), F32),
                                 jnp.full((NSA_W,), scale, F32), jnp.ones((MAIN_COLS - 3 * SB_W - NSA_W,), F32)])
    col_scale = col_scale.reshape(1, MAIN_COLS)
    xs = x[0]
    w_in_t = jnp.swapaxes(w_in, 1, 2)
    for l in range(depth):
        w_g = w_in[l][:, MAIN_COLS:].reshape(d_model, NSA_KV, 3 * NSA_HPG)
        w_g = jnp.pad(w_g, ((0, 0), (0, 0), (0, LANES - 3 * NSA_HPG))).reshape(1, d_model, NSA_KV * LANES)

        h = _rmsnorm(xs, attn_norm[l], BF16)
        proj = _matmul_fullk(_mm_scale_nt_kernel, h, [w_in_t], l, MAIN_COLS, [col_scale], BF16, 1024, 512,
                             "in_proj", transposed=True)
        gates = _matmul_fullk(_mm_sigmoid_kernel, h, [w_g], 0, NSA_KV * LANES, [], F32, 1024, NSA_KV * LANES,
                              "gate_proj")

        o_sb = _sb_attention(proj)

        def blocks_view(cb):
            cols = proj[:, cb * HEAD_DIM:(cb + NSA_KV) * HEAD_DIM]
            r = cols.reshape(s // STRIDE_CMP, STRIDE_CMP, NSA_KV, HEAD_DIM).transpose(2, 0, 1, 3)
            return r.reshape(NSA_KV, s // STRIDE_CMP, STRIDE_CMP * HEAD_DIM)

        k_cmp = _compress(blocks_view(CB_KC), pos_cmp_k[l], w_cmp_k1[l], w_cmp_k2[l])
        v_cmp = _compress(blocks_view(CB_VC), pos_cmp_v[l], w_cmp_v1[l], w_cmp_v2[l])
        o_c, sel, tile_any = _nsa_cmp(proj, k_cmp, v_cmp, slopes)
        o_nsa = _nsa_main(proj, sel, tile_any, o_c, gates, slopes)

        xs, xs_bf16 = _out_proj(o_sb, o_nsa, w_out, l, norm_sb[l], norm_nsa[l], xs)
        act, w_down_bf16 = _ffn_up(xs_bf16, w_gate, w_up, w_down, l, ffn_norm[l])
        xs = _ffn_down(act, w_down_bf16, xs)
    return _rmsnorm(xs, final_norm, F32)[None]
```

```python
import functools

import numpy as np
import jax
import jax.numpy as jnp
from jax import lax
from jax.experimental import pallas as pl
from jax.experimental.pallas import tpu as pltpu

HEAD_DIM = 128
SB_HEADS = 16
NSA_HEADS = 16
NSA_KV = 2
NSA_HPG = NSA_HEADS // NSA_KV
L_CMP = 32
STRIDE_CMP = 16
CMP_HIDDEN = 256
L_SLC = 64
LOG2_L_SLC = 6
N_TOPK = 16
WINDOW = 512
EPS = 1e-6
NEG_INF = -1e30
FORCE_BONUS = 1e6

LANES = 128
SUBLANES = 8
TQ = 128
TK = 128
SB_WIN = 3
SB_QT = 8
NSA_TK = 256
VMEM_CAP = 56 * 1024 * 1024
SB_DEAD_LOG = -105.0

F32 = jnp.float32
BF16 = jnp.bfloat16

SB_W = SB_HEADS * HEAD_DIM
NSA_W = NSA_HEADS * HEAD_DIM
GROUP_W = NSA_HPG * HEAD_DIM
MAIN_COLS = 3 * SB_W + NSA_W + 3 * 2 * NSA_KV * HEAD_DIM
GATE_COLS = 3 * NSA_HEADS
CB_KC, CB_VC, CB_KS, CB_VS, CB_KW, CB_VW = 64, 66, 68, 70, 72, 74


def _cparams(sem, vmem_bytes):
    return pltpu.CompilerParams(dimension_semantics=sem, vmem_limit_bytes=int(min(vmem_bytes, VMEM_CAP)))


def _dot(a, b):
    return jnp.dot(a, b, preferred_element_type=F32)


def _dot_nt(a, b):
    return lax.dot_general(a, b, (((1,), (1,)), ((), ())), preferred_element_type=F32)


def _by_row_halves(dot_fn, a, b):
    half = a.shape[0] // 2
    return jnp.concatenate([dot_fn(a[:half], b), dot_fn(a[half:], b)], axis=0)


def _split_bf16(x):
    hi = x.astype(BF16)
    lo = (x - hi.astype(F32)).astype(BF16)
    return hi, lo


def _rms_kernel(x_ref, g_ref, o_ref):
    x = x_ref[...]
    ms = jnp.mean(x * x, axis=-1, keepdims=True)
    o_ref[...] = (x * lax.rsqrt(ms + EPS) * g_ref[...]).astype(o_ref.dtype)


def _rmsnorm(x, g, out_dtype, tm=256):
    m, d = x.shape
    return pl.pallas_call(
        _rms_kernel,
        grid=(m // tm,),
        in_specs=[pl.BlockSpec((tm, d), lambda i: (i, 0)), pl.BlockSpec((1, d), lambda i: (0, 0))],
        out_specs=pl.BlockSpec((tm, d), lambda i: (i, 0)),
        out_shape=jax.ShapeDtypeStruct((m, d), out_dtype),
        compiler_params=_cparams(("parallel",), 6 * tm * d * 4),
        name="rmsnorm",
    )(x, g.reshape(1, d))


def _row_scale(a_ref):
    a = a_ref[...].astype(F32)
    ms = jnp.mean(a * a, axis=-1, keepdims=True)
    return jnp.broadcast_to(lax.rsqrt(ms + EPS), (a.shape[0], LANES))


def _lanes(x, n):
    return jnp.concatenate([x] * (n // LANES), axis=1)


def _gained_slab(w_ref, g_ref, o_ref):
    o_ref[...] = (w_ref[...] * _lanes(g_ref[...], w_ref.shape[1])).astype(o_ref.dtype)


def _slab_specs(w, layer, n_steps, step_of):
    rows, cols = w.shape[1], w.shape[2]
    slab = rows // n_steps
    assert slab * n_steps == rows and slab % (2 * SUBLANES) == 0
    return (pl.BlockSpec((None, slab, cols), lambda *idx: (layer, step_of(*idx), 0)),
            pl.BlockSpec((slab, LANES), lambda *idx: (step_of(*idx), 0)),
            pl.BlockSpec((slab, cols), lambda *idx: (step_of(*idx), 0)),
            jax.ShapeDtypeStruct((rows, cols), BF16), slab * cols * 6 + slab * LANES * 4)


def _row_gains(g):
    return jnp.broadcast_to(g.reshape(-1, 1), (g.shape[0], LANES))


def _out_proj_kernel(a1_ref, a2_ref, w_ref, res_ref, o_ref, ob_ref, r_ref):
    @pl.when(pl.program_id(1) == 0)
    def _():
        r_ref[0] = _row_scale(a1_ref)
        r_ref[1] = _row_scale(a2_ref)

    k1 = a1_ref.shape[1]
    tn = o_ref.shape[1]
    y1 = _dot(a1_ref[...], w_ref[:k1, :])
    y2 = _dot(a2_ref[...], w_ref[k1:, :])
    out = res_ref[...] + (_lanes(r_ref[0], tn) * y1 + _lanes(r_ref[1], tn) * y2)
    o_ref[...] = out
    ob_ref[...] = out.astype(ob_ref.dtype)


def _out_proj(a1, a2, w, res, tm=1024, tn=512):
    m, k1 = a1.shape
    k2 = a2.shape[1]
    n = w.shape[1]
    vmem = 2 * tm * (k1 + k2) * 2 + 2 * (k1 + k2) * tn * 2 + 2 * tm * tn * (4 + 4 + 2) + 3 * tm * tn * 4 + (4 << 20)
    return pl.pallas_call(
        _out_proj_kernel,
        grid=(m // tm, n // tn),
        in_specs=[pl.BlockSpec((tm, k1), lambda i, j: (i, 0)), pl.BlockSpec((tm, k2), lambda i, j: (i, 0)),
                  pl.BlockSpec((k1 + k2, tn), lambda i, j: (0, j)),
                  pl.BlockSpec((tm, tn), lambda i, j: (i, j))],
        out_specs=[pl.BlockSpec((tm, tn), lambda i, j: (i, j)), pl.BlockSpec((tm, tn), lambda i, j: (i, j))],
        out_shape=[jax.ShapeDtypeStruct((m, n), F32), jax.ShapeDtypeStruct((m, n), BF16)],
        scratch_shapes=[pltpu.VMEM((2, tm, LANES), F32)],
        compiler_params=_cparams(("parallel", "arbitrary"), vmem),
        name="out_proj",
    )(a1, a2, w, res)


def _ffn_up_kernel(a_ref, wg_ref, wu_ref, wd_ref, o_ref, wd_bf16_ref, r_ref):
    @pl.when(pl.program_id(1) == 0)
    def _():
        r_ref[...] = _row_scale(a_ref)

    a = a_ref[...]
    r = _lanes(r_ref[...], o_ref.shape[1])
    gate = r * _dot(a, wg_ref[...])
    up = r * _dot(a, wu_ref[...])
    o_ref[...] = (jax.nn.silu(gate) * up).astype(o_ref.dtype)
    wd_bf16_ref[...] = wd_ref[...].astype(wd_bf16_ref.dtype)


def _ffn_up(a, w_gate, w_up, w_down, layer, tm=2048, tn=2 * LANES):
    m, k = a.shape
    n = w_gate.shape[1]
    d_out = w_down.shape[2]
    nj = n // tn
    slab = n // ((m // tm) * nj)
    assert slab * (m // tm) * nj == n and slab % (2 * SUBLANES) == 0
    vmem = 2 * tm * k * 2 + 2 * 2 * k * tn * 2 + 2 * tm * tn * 2 + 5 * tm * tn * 4 + 2 * slab * d_out * 6 + (4 << 20)
    w_spec = pl.BlockSpec((k, tn), lambda i, j: (0, j))
    return pl.pallas_call(
        _ffn_up_kernel,
        grid=(m // tm, nj),
        in_specs=[pl.BlockSpec((tm, k), lambda i, j: (i, 0)), w_spec, w_spec,
                  pl.BlockSpec((None, slab, d_out), lambda i, j: (layer, i * nj + j, 0))],
        out_specs=[pl.BlockSpec((tm, tn), lambda i, j: (i, j)),
                   pl.BlockSpec((slab, d_out), lambda i, j: (i * nj + j, 0))],
        out_shape=[jax.ShapeDtypeStruct((m, n), BF16), jax.ShapeDtypeStruct((n, d_out), BF16)],
        scratch_shapes=[pltpu.VMEM((tm, LANES), F32)],
        compiler_params=_cparams(("parallel", "arbitrary"), vmem),
        name="ffn_gate_up",
    )(a, w_gate, w_up, w_down)


IN_PROJ_SLABS = 16


def _in_proj_kernel(a_ref, bt_ref, cs_ref, wa_ref, wb_ref, wgain_ref, o_ref, wa_bf16_ref, wb_bf16_ref):
    o_ref[...] = (_dot_nt(a_ref[...], bt_ref[...].astype(BF16)) * cs_ref[...]).astype(o_ref.dtype)

    @pl.when(pl.program_id(1) < IN_PROJ_SLABS)
    def _():
        _gained_slab(wa_ref, wgain_ref, wa_bf16_ref)
        _gained_slab(wb_ref, wgain_ref, wb_bf16_ref)


def _in_proj(a, w_t, layer, n, col_scale, w_a, w_b, next_gains, tm=1024, tn=512):
    m, k = a.shape
    ni, nj = m // tm, n // tn
    assert nj >= IN_PROJ_SLABS
    w_spec, g_spec, wo_spec, wo_shape, slab_bytes = _slab_specs(
        w_a, layer, ni * IN_PROJ_SLABS, lambda i, j: i * IN_PROJ_SLABS + jnp.minimum(j, IN_PROJ_SLABS - 1))
    vmem = 2 * tm * k * 2 + tn * k * (2 * 4 + 2) + 2 * tm * tn * 2 + 2 * tm * tn * 4 + 4 * slab_bytes + (4 << 20)
    return pl.pallas_call(
        _in_proj_kernel,
        grid=(ni, nj),
        in_specs=[pl.BlockSpec((tm, k), lambda i, j: (i, 0)),
                  pl.BlockSpec((None, tn, k), lambda i, j: (layer, j, 0)),
                  pl.BlockSpec((1, tn), lambda i, j: (0, j)),
                  w_spec, w_spec, g_spec],
        out_specs=[pl.BlockSpec((tm, tn), lambda i, j: (i, j)), wo_spec, wo_spec],
        out_shape=[jax.ShapeDtypeStruct((m, n), BF16), wo_shape, wo_shape],
        compiler_params=_cparams(("parallel", "arbitrary"), vmem),
        name="in_proj",
    )(a, w_t, col_scale, w_a, w_b, _row_gains(next_gains))


def _mm_sigmoid_kernel(a_ref, b_ref, o_ref):
    o_ref[...] = jax.nn.sigmoid(_dot(a_ref[...], b_ref[...].astype(BF16)))


def _ffn_down_kernel(a_ref, b_ref, res_ref, o_ref, acc_ref):
    kk = pl.program_id(2)

    @pl.when(kk == 0)
    def _():
        acc_ref[...] = res_ref[...]

    acc_ref[...] += _dot(a_ref[...], b_ref[...])

    @pl.when(kk == pl.num_programs(2) - 1)
    def _():
        o_ref[...] = acc_ref[...]


def _ffn_down(a, b, res, tm=1024, tn=512):
    m, k = a.shape
    n = b.shape[1]
    tk = k // 2
    assert tk % LANES == 0
    vmem = 2 * tm * tk * 2 + 2 * tk * tn * 2 + 6 * tm * tn * 4 + (4 << 20)
    return pl.pallas_call(
        _ffn_down_kernel,
        grid=(m // tm, n // tn, k // tk),
        in_specs=[pl.BlockSpec((tm, tk), lambda i, j, kk: (i, kk)),
                  pl.BlockSpec((tk, tn), lambda i, j, kk: (kk, j)),
                  pl.BlockSpec((tm, tn), lambda i, j, kk: (i, j))],
        out_specs=pl.BlockSpec((tm, tn), lambda i, j, kk: (i, j)),
        out_shape=jax.ShapeDtypeStruct((m, n), F32),
        scratch_shapes=[pltpu.VMEM((tm, tn), F32)],
        compiler_params=_cparams(("parallel", "parallel", "arbitrary"), vmem),
        name="ffn_down",
    )(a, b, res)


def _gate_proj(a, w, tm=1024):
    m, k = a.shape
    n = w.shape[1]
    vmem = 2 * tm * k * 2 + k * n * (2 * 4 + 2) + 4 * tm * n * 4 + (4 << 20)
    return pl.pallas_call(
        _mm_sigmoid_kernel,
        grid=(m // tm,),
        in_specs=[pl.BlockSpec((tm, k), lambda i: (i, 0)), pl.BlockSpec((k, n), lambda i: (0, 0))],
        out_specs=pl.BlockSpec((tm, n), lambda i: (i, 0)),
        out_shape=jax.ShapeDtypeStruct((m, n), F32),
        compiler_params=_cparams(("parallel",), vmem),
        name="gate_proj",
    )(a, w)


def _sb_kernel(q_ref, k_ref, v_ref, u_ref, w_ref, wgain_ref, o_ref, w_bf16_ref, carry_ref, acc_ref, *, hg):
    _gained_slab(w_ref, wgain_ref, w_bf16_ref)
    step = pl.program_id(1)
    u = u_ref[...]
    heads = [slice(h * HEAD_DIM, (h + 1) * HEAD_DIM) for h in range(hg)]

    def logs(z):
        log1p_e = jnp.log(1.0 + jnp.exp(-jnp.abs(z)))
        log_sig = jnp.minimum(z, 0.0) - log1p_e
        return log_sig, log_sig - z

    def tile_sums(log_not, n):
        hi, lo = _split_bf16(log_not)
        return [jnp.concatenate([hi[:, t * TK:(t + 1) * TK], lo[:, t * TK:(t + 1) * TK]], axis=1) for t in range(n)]

    def window(sub):
        qt = step * SB_QT + sub
        rows = slice(sub * TQ, (sub + 1) * TQ)
        base = jnp.maximum(qt - (SB_WIN - 1), 0)
        row0 = pl.multiple_of(base * TK, TK)
        wk = SB_WIN * TK
        key_pos = base * TK + lax.broadcasted_iota(jnp.int32, (TQ, wk), 1)
        visible = key_pos < qt * TQ + lax.broadcasted_iota(jnp.int32, (TQ, wk), 0)
        zs = [jnp.where(visible, _dot_nt(q_ref[rows, cs], k_ref[pl.ds(row0, wk), cs]), NEG_INF) for cs in heads]
        lg = [logs(z) for z in zs]
        lhs = [part for (_, log_not) in lg for part in tile_sums(log_not, SB_WIN)]
        sums = _by_row_halves(_dot, jnp.concatenate(lhs, axis=0), u)
        worst = None
        for h, cs in enumerate(heads):
            carry = None
            shifted = [None] * SB_WIN
            for t in reversed(range(SB_WIN)):
                blk = sums[(h * SB_WIN + t) * TQ:(h * SB_WIN + t + 1) * TQ]
                tail, total = blk[:, :TK], blk[:, TK:]
                shifted[t] = tail if carry is None else tail + carry
                carry = total if carry is None else carry + total
            a = jnp.exp(lg[h][0] + jnp.concatenate(shifted, axis=1))
            acc_ref[sub * hg + h] = _dot(a.astype(BF16), v_ref[pl.ds(row0, wk), cs])
            carry_ref[sub * hg + h] = carry
            worst = carry if worst is None else jnp.maximum(worst, carry)
        return base, jnp.max(worst)

    def far_sweep(sub, base, worst):
        rows = slice(sub * TQ, (sub + 1) * TQ)

        def tile(kb):
            r0 = pl.multiple_of(kb * TK, TK)
            lg1 = [logs(_dot_nt(q_ref[rows, cs], k_ref[pl.ds(r0, TK), cs])) for cs in heads]
            sums1 = _dot(jnp.concatenate([tile_sums(log_not, 1)[0] for (_, log_not) in lg1], axis=0), u)
            far = None
            for h, cs in enumerate(heads):
                blk = sums1[h * TQ:(h + 1) * TQ]
                carry = carry_ref[sub * hg + h]
                a = jnp.exp(lg1[h][0] + blk[:, :TK] + carry)
                acc_ref[sub * hg + h] += _dot(a.astype(BF16), v_ref[pl.ds(r0, TK), cs])
                carry = carry + blk[:, TK:]
                carry_ref[sub * hg + h] = carry
                far = carry if far is None else jnp.maximum(far, carry)
            return jnp.max(far)

        def cond(c):
            kb, far = c
            return jnp.logical_and(kb >= 0, far > SB_DEAD_LOG)

        def body(c):
            kb, _ = c
            return kb - 1, tile(kb)

        lax.while_loop(cond, body, (base - 1, worst))

    starts = [window(sub) for sub in range(SB_QT)]
    for sub, (base, worst) in enumerate(starts):
        far_sweep(sub, base, worst)
    for sub in range(SB_QT):
        for h, cs in enumerate(heads):
            o_ref[sub * TQ:(sub + 1) * TQ, cs] = acc_ref[sub * hg + h].astype(o_ref.dtype)


def _sb_attention(proj, w_next, layer, next_gains, hg=4):
    s = proj.shape[0]
    w = hg * HEAD_DIM
    nblk = SB_W // w
    tri = np.arange(TK)[:, None] > np.arange(TK)[None, :]
    half = np.concatenate([tri, np.ones((TK, TK), bool)], axis=1)
    u = jnp.asarray(np.concatenate([half, half], axis=0), dtype=BF16)
    tq = SB_QT * TQ
    nq = s // tq
    w_spec, g_spec, wo_spec, wo_shape, slab_bytes = _slab_specs(w_next, layer, nblk * nq, lambda g, i: g * nq + i)
    vmem = (2 * 2 * s * w * 2 + 2 * tq * w * 2 + 2 * tq * w * 4 + 2 * SB_QT * hg * TQ * TK * 4 + 2 * slab_bytes
            + (16 << 20))
    return pl.pallas_call(
        functools.partial(_sb_kernel, hg=hg),
        grid=(nblk, nq),
        in_specs=[pl.BlockSpec((tq, w), lambda g, i: (i, g)),
                  pl.BlockSpec((s, w), lambda g, i: (0, nblk + g)),
                  pl.BlockSpec((s, w), lambda g, i: (0, 2 * nblk + g)),
                  pl.BlockSpec((2 * TK, 2 * TK), lambda g, i: (0, 0)),
                  w_spec, g_spec],
        out_specs=[pl.BlockSpec((tq, w), lambda g, i: (i, g)), wo_spec],
        out_shape=[jax.ShapeDtypeStruct((s, SB_W), BF16), wo_shape],
        scratch_shapes=[pltpu.VMEM((SB_QT * hg, TQ, TK), F32), pltpu.VMEM((SB_QT * hg, TQ, HEAD_DIM), F32)],
        compiler_params=_cparams(("parallel", "arbitrary"), vmem),
        name="sb_attention",
    )(proj, proj, proj, u, w_next, _row_gains(next_gains))


def _compress_kernel(r_ref, pos_ref, w1_ref, w2_ref, o_ref):
    r = r_ref[0]
    half = r.shape[1]
    first = _dot(r, w1_ref[:half, :])
    second = _dot(r, w1_ref[half:, :])
    second = pltpu.roll(second, second.shape[0] - 1, 0)
    pos_term = _dot(pos_ref[...].astype(BF16), w1_ref[...])[0:1, :]
    hidden = jax.nn.gelu(first + second + pos_term)
    o_ref[0] = _dot(hidden.astype(BF16), w2_ref[...]).astype(o_ref.dtype)


def _compress(r, pos, w1, w2):
    g, n, half = r.shape
    pos_rows = jnp.zeros((SUBLANES, 2 * half), F32).at[0].set(pos.reshape(-1))
    return pl.pallas_call(
        _compress_kernel,
        grid=(g,),
        in_specs=[pl.BlockSpec((1, n, half), lambda i: (i, 0, 0)),
                  pl.BlockSpec((SUBLANES, 2 * half), lambda i: (0, 0)),
                  pl.BlockSpec((2 * half, CMP_HIDDEN), lambda i: (0, 0)),
                  pl.BlockSpec((CMP_HIDDEN, HEAD_DIM), lambda i: (0, 0))],
        out_specs=pl.BlockSpec((1, n, HEAD_DIM), lambda i: (i, 0, 0)),
        out_shape=jax.ShapeDtypeStruct((g, n, HEAD_DIM), BF16),
        compiler_params=_cparams(("parallel",), 24 << 20),
        name="nsa_compress",
    )(r, pos_rows, w1.astype(BF16), w2.astype(BF16))


def _stack_heads(q_ref, qs_ref, group=0):
    for h in range(NSA_HPG):
        r0 = (group * NSA_HPG + h) * TQ
        c0 = (group * NSA_HPG + h) * HEAD_DIM
        qs_ref[r0:r0 + TQ, :] = q_ref[:, c0:c0 + HEAD_DIM]


def _nsa_cmp_kernel(slopes_ref, q_ref, kc_ref, vc_ref, msel_ref, oc_ref, sel_ref, flag_ref, qs_ref, *, n_top):
    i = pl.program_id(0)
    n_cmp = kc_ref.shape[1]
    n_slc = msel_ref.shape[1]
    t_rel = lax.broadcasted_iota(jnp.int32, (TQ, n_cmp), 0)
    end_rel = lax.broadcasted_iota(jnp.int32, (TQ, n_cmp), 1) * STRIDE_CMP + (L_CMP - 1) - i * TQ
    bias = jnp.where(end_rel <= t_rel, 0.0, NEG_INF)
    end_row = end_rel[0:1, :].astype(F32)
    t_col = i * TQ + lax.broadcasted_iota(jnp.int32, (TQ, 1), 0)
    row_valid = jnp.where(t_col >= L_CMP - 1, 1.0, 0.0)
    blk = lax.broadcasted_iota(jnp.int32, (n_slc, TQ), 0)
    cur = lax.shift_right_logical(i * TQ + lax.broadcasted_iota(jnp.int32, (n_slc, TQ), 1), LOG2_L_SLC)
    valid = blk <= cur
    bonus = jnp.where(blk == 0, FORCE_BONUS, 0.0)
    bonus = jnp.where(blk == cur, FORCE_BONUS, bonus)
    bonus = jnp.where(blk == cur - 1, FORCE_BONUS, bonus)
    blk_f = blk.astype(F32)
    msel = msel_ref[...]
    rows = NSA_HPG * TQ
    for g in range(NSA_KV):
        _stack_heads(q_ref, qs_ref, g)
        z = _by_row_halves(_dot_nt, qs_ref[g * rows:(g + 1) * rows, :], kc_ref[g])
        p_sum = jnp.zeros((TQ, n_cmp), F32)
        ps = []
        for h in range(NSA_HPG):
            s = z[h * TQ:(h + 1) * TQ] + (bias + slopes_ref[g * NSA_HPG + h] * end_row)
            m = jnp.max(s, axis=-1, keepdims=True)
            e = jnp.exp(s - m)
            p = e * (row_valid / jnp.sum(e, axis=-1, keepdims=True))
            ps.append(p.astype(BF16))
            p_sum = p_sum + p
        oc = _by_row_halves(_dot, jnp.concatenate(ps, axis=0), vc_ref[g])
        for h in range(NSA_HPG):
            c0 = (g * NSA_HPG + h) * HEAD_DIM
            oc_ref[:, c0:c0 + HEAD_DIM] = oc[h * TQ:(h + 1) * TQ]
        hi, lo = _split_bf16(p_sum)
        imp = _dot(hi, msel) + _dot(lo, msel)
        score = jnp.where(valid, imp.T + bonus, NEG_INF)
        taken = jnp.zeros((n_slc, TQ), F32)
        for _ in range(n_top):
            best = jnp.max(score, axis=0, keepdims=True)
            first = jnp.min(jnp.where(score == best, blk_f, float(n_slc)), axis=0, keepdims=True)
            hit = blk_f == first
            taken = jnp.where(hit, 1.0, taken)
            score = jnp.where(hit, -jnp.inf, score)
        chosen_q = jnp.where(valid, taken, 0.0).T
        sel_ref[g, 0] = chosen_q.astype(sel_ref.dtype)
        flag_ref[g, 0] = jnp.max(chosen_q, axis=0, keepdims=True)


def _select_matrix(n_cmp_pad, n_slc):
    rs = L_SLC // STRIDE_CMP
    rc = L_CMP // STRIDE_CMP
    m = np.zeros((n_cmp_pad, n_slc), np.float32)
    for j in range(n_slc):
        for a in range(rs):
            for b in range(rc):
                src = rs * j - a - b
                if 0 <= src < n_cmp_pad - 1:
                    m[src, j] += 1.0
    return jnp.asarray(m, dtype=BF16)


def _nsa_cmp(proj, k_cmp, v_cmp, slopes):
    s = proj.shape[0]
    n_cmp = k_cmp.shape[1]
    n_slc = s // L_SLC
    nq = s // TQ
    msel = _select_matrix(n_cmp, n_slc)
    qblk = (3 * SB_W) // NSA_W
    return pl.pallas_call(
        functools.partial(_nsa_cmp_kernel, n_top=min(N_TOPK, n_slc)),
        grid=(nq,),
        in_specs=[pl.BlockSpec(memory_space=pltpu.SMEM),
                  pl.BlockSpec((TQ, NSA_W), lambda i: (i, qblk)),
                  pl.BlockSpec((NSA_KV, n_cmp, HEAD_DIM), lambda i: (0, 0, 0)),
                  pl.BlockSpec((NSA_KV, n_cmp, HEAD_DIM), lambda i: (0, 0, 0)),
                  pl.BlockSpec((n_cmp, n_slc), lambda i: (0, 0))],
        out_specs=[pl.BlockSpec((TQ, NSA_W), lambda i: (i, 0)),
                   pl.BlockSpec((NSA_KV, 1, TQ, n_slc), lambda i: (0, i, 0, 0)),
                   pl.BlockSpec((NSA_KV, 1, 1, n_slc), lambda i: (0, i, 0, 0))],
        out_shape=[jax.ShapeDtypeStruct((s, NSA_W), F32),
                   jax.ShapeDtypeStruct((NSA_KV, nq, TQ, n_slc), BF16),
                   jax.ShapeDtypeStruct((NSA_KV, nq, 1, n_slc), F32)],
        scratch_shapes=[pltpu.VMEM((NSA_HEADS * TQ, HEAD_DIM), BF16)],
        compiler_params=_cparams(("parallel",), 40 << 20),
        name="nsa_cmp_select",
    )(slopes, proj, k_cmp, v_cmp, msel)


def _nsa_main_kernel(tiles_ref, count_ref, slopes_ref, q_ref, ks_ref, vs_ref, kw_ref, vw_ref, sel_ref, oc_ref,
                     gate_ref, o_ref, qa_ref, acc_s, m_s):
    g = pl.program_id(0)
    i = pl.program_id(1)
    n_kt = pl.num_programs(1) * TQ // NSA_TK
    n_slc = sel_ref.shape[3]
    not_picked = jnp.where(sel_ref[0, 0].astype(F32) > 0.5, 0.0, NEG_INF).astype(BF16)
    for h in range(NSA_HPG):
        qa_ref[h * TQ:(h + 1) * TQ, :HEAD_DIM] = q_ref[:, h * HEAD_DIM:(h + 1) * HEAD_DIM]
        qa_ref[h * TQ:(h + 1) * TQ, HEAD_DIM:] = not_picked
    slopes = [slopes_ref[g * NSA_HPG + h] for h in range(NSA_HPG)]
    diag = lax.div(i * TQ, NSA_TK)

    def scores(k_ref, v_ref, pieces, bias, select):
        zs = []
        for r0, n in pieces:
            k = k_ref[pl.ds(r0, n), :]
            if select:
                blk_of_key = lax.shift_right_logical(r0 + lax.broadcasted_iota(jnp.int32, (n, n_slc), 0), LOG2_L_SLC)
                one_hot = jnp.where(lax.broadcasted_iota(jnp.int32, (n, n_slc), 1) == blk_of_key, 1.0, 0.0)
                zs.append(_by_row_halves(_dot_nt, qa_ref[...], jnp.concatenate([k, one_hot.astype(BF16)], axis=1)))
            else:
                zs.append(_by_row_halves(_dot_nt, qa_ref[:, :HEAD_DIM], k))
        z = jnp.concatenate(zs, axis=1)
        v_aug = jnp.concatenate(
            [jnp.concatenate([v_ref[pl.ds(r0, n), :], jnp.ones((n, HEAD_DIM), BF16)], axis=1) for r0, n in pieces],
            axis=0)
        key_rel = jnp.concatenate(
            [(r0 - i * TQ + lax.broadcasted_iota(jnp.int32, (1, n), 1)).astype(F32) for r0, n in pieces], axis=1)
        if bias is None:
            return [z[h * TQ:(h + 1) * TQ] + slopes[h] * key_rel for h in range(NSA_HPG)], v_aug
        return [z[h * TQ:(h + 1) * TQ] + (bias + slopes[h] * key_rel) for h in range(NSA_HPG)], v_aug

    def attend(pieces, bias, first):
        s_all, v_aug = scores(ks_ref, vs_ref, pieces, bias, True)
        width = s_all[0].shape[1]
        ps, alphas = [], []
        for h, s in enumerate(s_all):
            rows = slice(h * TQ, (h + 1) * TQ)
            m_tile = jnp.max(s, axis=-1, keepdims=True)
            if first:
                m_new = jnp.broadcast_to(m_tile, (TQ, LANES))
            else:
                m_old = m_s[rows]
                m_new = jnp.maximum(m_old, m_tile)
                alphas.append(jnp.exp(m_old - m_new))
            m_s[rows] = m_new
            ps.append(jnp.exp(s - jnp.concatenate([m_new] * (width // LANES), axis=1)).astype(BF16))
        pv = _by_row_halves(_dot, jnp.concatenate(ps, axis=0), v_aug)
        for h in range(NSA_HPG):
            rows = slice(h * TQ, (h + 1) * TQ)
            if first:
                acc_s[rows] = pv[rows]
            else:
                acc_s[rows] = acc_s[rows] * jnp.concatenate([alphas[h], alphas[h]], axis=1) + pv[rows]

    def tile_rows(kt):
        return pl.multiple_of(kt * NSA_TK, NSA_TK), NSA_TK

    near = jnp.maximum(diag - 1, 0)
    near0 = pl.multiple_of(near * NSA_TK, NSA_TK)
    nw = 2 * NSA_TK
    ahead = (near0 + lax.broadcasted_iota(jnp.int32, (TQ, nw), 1)) - (i * TQ + lax.broadcasted_iota(jnp.int32, (TQ, nw), 0))
    attend([(near0, nw)], jnp.where(ahead <= 0, 0.0, NEG_INF), True)
    step = g * pl.num_programs(1) + i
    n_far = count_ref[step]

    def pair_body(p, _):
        ta = tiles_ref[step * n_kt + 2 * p]
        tb = tiles_ref[step * n_kt + 2 * p + 1]
        attend([tile_rows(ta), tile_rows(tb)], None, False)
        return 0

    lax.fori_loop(0, lax.shift_right_logical(n_far, 1), pair_body, 0)

    @pl.when(lax.rem(n_far, 2) == 1)
    def _():
        attend([tile_rows(tiles_ref[step * n_kt + n_far - 1])], None, False)

    wn = WINDOW + TQ
    w0 = pl.multiple_of(jnp.maximum(i * TQ - WINDOW, 0), TQ)
    dist = i * TQ + lax.broadcasted_iota(jnp.int32, (TQ, wn), 0) - (w0 + lax.broadcasted_iota(jnp.int32, (TQ, wn), 1))
    w_bias = jnp.where(dist >= 0, jnp.where(dist < WINDOW, 0.0, NEG_INF), NEG_INF)
    s_win, v_win = scores(kw_ref, vw_ref, [(w0, wn)], w_bias, False)
    p_win = [jnp.exp(s - jnp.max(s, axis=-1, keepdims=True)).astype(BF16) for s in s_win]
    pv_win = _by_row_halves(_dot, jnp.concatenate(p_win, axis=0), v_win)

    gates = gate_ref[...]
    for h in range(NSA_HPG):
        cs = slice(h * HEAD_DIM, (h + 1) * HEAD_DIM)
        a_s = acc_s[h * TQ:(h + 1) * TQ]
        a_w = pv_win[h * TQ:(h + 1) * TQ]
        o_sel = a_s[:, :HEAD_DIM] / a_s[:, HEAD_DIM:HEAD_DIM + 1]
        o_win = a_w[:, :HEAD_DIM] / a_w[:, HEAD_DIM:HEAD_DIM + 1]
        o_ref[:, cs] = (gates[:, 3 * h:3 * h + 1] * oc_ref[:, cs] + gates[:, 3 * h + 1:3 * h + 2] * o_sel
                        + gates[:, 3 * h + 2:3 * h + 3] * o_win).astype(o_ref.dtype)


def _far_tile_lists(tile_any, s):
    nq = s // TQ
    n_kt = s // NSA_TK
    hit = tile_any.reshape(NSA_KV, nq, n_kt, NSA_TK // L_SLC).max(axis=-1) > 0.5
    kt = jnp.arange(n_kt, dtype=jnp.int32)
    diag = (jnp.arange(nq, dtype=jnp.int32) * TQ) // NSA_TK
    far = jnp.logical_and(hit, kt[None, None, :] < diag[None, :, None] - 1)
    tiles = jnp.sort(jnp.where(far, kt[None, None, :], n_kt), axis=-1)
    tiles = jnp.minimum(tiles, n_kt - 1)
    return tiles.reshape(-1), jnp.sum(far, axis=-1, dtype=jnp.int32).reshape(-1)


def _nsa_main(proj, sel, tile_any, o_c, gates, slopes):
    s = proj.shape[0]
    nq = s // TQ
    n_slc = s // L_SLC
    qblk = (3 * SB_W) // GROUP_W
    tiles, counts = _far_tile_lists(tile_any, s)
    kv_spec = lambda cb: pl.BlockSpec((s, HEAD_DIM), lambda g, i, t, c: (0, cb + g))
    rows = NSA_HPG * TQ
    grid_spec = pltpu.PrefetchScalarGridSpec(
        num_scalar_prefetch=2,
        grid=(NSA_KV, nq),
        in_specs=[pl.BlockSpec(memory_space=pltpu.SMEM),
                  pl.BlockSpec((TQ, GROUP_W), lambda g, i, t, c: (i, qblk + g)),
                  kv_spec(CB_KS), kv_spec(CB_VS), kv_spec(CB_KW), kv_spec(CB_VW),
                  pl.BlockSpec((1, 1, TQ, n_slc), lambda g, i, t, c: (g, i, 0, 0)),
                  pl.BlockSpec((TQ, GROUP_W), lambda g, i, t, c: (i, g)),
                  pl.BlockSpec((TQ, LANES), lambda g, i, t, c: (i, g))],
        out_specs=pl.BlockSpec((TQ, GROUP_W), lambda g, i, t, c: (i, g)),
        scratch_shapes=[pltpu.VMEM((rows, HEAD_DIM + n_slc), BF16),
                        pltpu.VMEM((rows, 2 * HEAD_DIM), F32), pltpu.VMEM((rows, LANES), F32)],
    )
    vmem = 2 * 4 * s * HEAD_DIM * 2 + (24 << 20)
    return pl.pallas_call(
        _nsa_main_kernel,
        grid_spec=grid_spec,
        out_shape=jax.ShapeDtypeStruct((s, NSA_W), BF16),
        compiler_params=_cparams(("parallel", "arbitrary"), vmem),
        name="nsa_select_window",
    )(tiles, counts, slopes, proj, proj, proj, proj, proj, sel, o_c, gates)


def kernel(x, attn_norm, w_in, pos_cmp_k, pos_cmp_v, w_cmp_k1, w_cmp_k2, w_cmp_v1, w_cmp_v2, norm_sb, norm_nsa,
           w_out, ffn_norm, w_gate, w_up, w_down, final_norm):
    batch, s, d_model = x.shape
    assert batch == 1 and s % 1024 == 0 and w_in.shape[2] == MAIN_COLS + GATE_COLS
    depth = w_in.shape[0]
    d_ff = w_gate.shape[2]
    assert d_ff % (2 * LANES) == 0 and (d_ff // 2) % LANES == 0
    scale = HEAD_DIM ** -0.5
    head_idx = jnp.arange(1, NSA_HEADS + 1, dtype=F32)
    slopes = 2.0 ** (-8.0 * head_idx / NSA_HEADS)
    col_scale = jnp.concatenate([jnp.full((SB_W,), scale, F32), jnp.ones((2 * SB_W,), F32),
                                 jnp.full((NSA_W,), scale, F32), jnp.ones((MAIN_COLS - 3 * SB_W - NSA_W,), F32)])
    col_scale = col_scale.reshape(1, MAIN_COLS)
    xs = x[0]
    w_in_t = jnp.swapaxes(w_in, 1, 2)
    for l in range(depth):
        w_g = w_in[l][:, MAIN_COLS:].reshape(d_model, NSA_KV, 3 * NSA_HPG)
        w_g = jnp.pad(w_g, ((0, 0), (0, 0), (0, LANES - 3 * NSA_HPG))).reshape(d_model, NSA_KV * LANES)

        h = _rmsnorm(xs, attn_norm[l], BF16)
        proj, w_gate_bf16, w_up_bf16 = _in_proj(h, w_in_t, l, MAIN_COLS, col_scale, w_gate, w_up, ffn_norm[l])
        gates = _gate_proj(h, w_g)

        o_sb, w_out_bf16 = _sb_attention(proj, w_out, l, jnp.concatenate([norm_sb[l], norm_nsa[l]]))

        def blocks_view(cb):
            cols = proj[:, cb * HEAD_DIM:(cb + NSA_KV) * HEAD_DIM]
            r = cols.reshape(s // STRIDE_CMP, STRIDE_CMP, NSA_KV, HEAD_DIM).transpose(2, 0, 1, 3)
            return r.reshape(NSA_KV, s // STRIDE_CMP, STRIDE_CMP * HEAD_DIM)

        k_cmp = _compress(blocks_view(CB_KC), pos_cmp_k[l], w_cmp_k1[l], w_cmp_k2[l])
        v_cmp = _compress(blocks_view(CB_VC), pos_cmp_v[l], w_cmp_v1[l], w_cmp_v2[l])
        o_c, sel, tile_any = _nsa_cmp(proj, k_cmp, v_cmp, slopes)
        o_nsa = _nsa_main(proj, sel, tile_any, o_c, gates, slopes)

        xs, xs_bf16 = _out_proj(o_sb, o_nsa, w_out_bf16, xs)
        act, w_down_bf16 = _ffn_up(xs_bf16, w_gate_bf16, w_up_bf16, w_down, l)
        xs = _ffn_down(act, w_down_bf16, xs)
    return _rmsnorm(xs, final_norm, F32)[None]
```

```python
import functools

import numpy as np
import jax
import jax.numpy as jnp
from jax import lax
from jax.experimental import pallas as pl
from jax.experimental.pallas import tpu as pltpu

HEAD_DIM = 128
SB_HEADS = 16
NSA_HEADS = 16
NSA_KV = 2
NSA_HPG = NSA_HEADS // NSA_KV
L_CMP = 32
STRIDE_CMP = 16
CMP_HIDDEN = 256
L_SLC = 64
LOG2_L_SLC = 6
N_TOPK = 16
WINDOW = 512
EPS = 1e-6
NEG_INF = -1e30
FORCE_BONUS = 1e6

LANES = 128
SUBLANES = 8
TQ = 128
TK = 128
SB_WIN = 3
SB_QT = 8
NSA_TK = 256
VMEM_CAP = 56 * 1024 * 1024
SB_DEAD_LOG = -105.0

F32 = jnp.float32
BF16 = jnp.bfloat16

SB_W = SB_HEADS * HEAD_DIM
NSA_W = NSA_HEADS * HEAD_DIM
GROUP_W = NSA_HPG * HEAD_DIM
MAIN_COLS = 3 * SB_W + NSA_W + 3 * 2 * NSA_KV * HEAD_DIM
GATE_COLS = 3 * NSA_HEADS
CB_KC, CB_VC, CB_KS, CB_VS, CB_KW, CB_VW = 64, 66, 68, 70, 72, 74


def _cparams(sem, vmem_bytes):
    return pltpu.CompilerParams(dimension_semantics=sem, vmem_limit_bytes=int(min(vmem_bytes, VMEM_CAP)))


def _dot(a, b):
    return jnp.dot(a, b, preferred_element_type=F32)


def _dot_nt(a, b):
    return lax.dot_general(a, b, (((1,), (1,)), ((), ())), preferred_element_type=F32)


def _by_row_halves(dot_fn, a, b):
    half = a.shape[0] // 2
    return jnp.concatenate([dot_fn(a[:half], b), dot_fn(a[half:], b)], axis=0)


def _split_bf16(x):
    hi = x.astype(BF16)
    lo = (x - hi.astype(F32)).astype(BF16)
    return hi, lo


def _rms_kernel(x_ref, g_ref, o_ref):
    x = x_ref[...]
    ms = jnp.mean(x * x, axis=-1, keepdims=True)
    o_ref[...] = (x * lax.rsqrt(ms + EPS) * g_ref[...]).astype(o_ref.dtype)


def _rmsnorm(x, g, out_dtype, tm=256):
    m, d = x.shape
    return pl.pallas_call(
        _rms_kernel,
        grid=(m // tm,),
        in_specs=[pl.BlockSpec((tm, d), lambda i: (i, 0)), pl.BlockSpec((1, d), lambda i: (0, 0))],
        out_specs=pl.BlockSpec((tm, d), lambda i: (i, 0)),
        out_shape=jax.ShapeDtypeStruct((m, d), out_dtype),
        compiler_params=_cparams(("parallel",), 6 * tm * d * 4),
        name="rmsnorm",
    )(x, g.reshape(1, d))


def _row_scale(a_ref):
    a = a_ref[...].astype(F32)
    ms = jnp.mean(a * a, axis=-1, keepdims=True)
    return jnp.broadcast_to(lax.rsqrt(ms + EPS), (a.shape[0], LANES))


def _lanes(x, n):
    return jnp.concatenate([x] * (n // LANES), axis=1)


def _gained_slab(w_ref, g_ref, o_ref):
    o_ref[...] = (w_ref[...] * _lanes(g_ref[...], w_ref.shape[1])).astype(o_ref.dtype)


def _slab_specs(w, layer, n_steps, step_of):
    rows, cols = w.shape[1], w.shape[2]
    slab = rows // n_steps
    assert slab * n_steps == rows and slab % (2 * SUBLANES) == 0
    return (pl.BlockSpec((None, slab, cols), lambda *idx: (layer, step_of(*idx), 0)),
            pl.BlockSpec((slab, LANES), lambda *idx: (step_of(*idx), 0)),
            pl.BlockSpec((slab, cols), lambda *idx: (step_of(*idx), 0)),
            jax.ShapeDtypeStruct((rows, cols), BF16), slab * cols * 6 + slab * LANES * 4)


def _row_gains(g):
    return jnp.broadcast_to(g.reshape(-1, 1), (g.shape[0], LANES))


def _out_proj_kernel(a1_ref, a2_ref, w_ref, res_ref, o_ref, ob_ref, r_ref):
    @pl.when(pl.program_id(1) == 0)
    def _():
        r_ref[0] = _row_scale(a1_ref)
        r_ref[1] = _row_scale(a2_ref)

    k1 = a1_ref.shape[1]
    tn = o_ref.shape[1]
    y1 = _dot(a1_ref[...], w_ref[:k1, :])
    y2 = _dot(a2_ref[...], w_ref[k1:, :])
    out = res_ref[...] + (_lanes(r_ref[0], tn) * y1 + _lanes(r_ref[1], tn) * y2)
    o_ref[...] = out
    ob_ref[...] = out.astype(ob_ref.dtype)


def _out_proj(a1, a2, w, res, tm=1024, tn=512):
    m, k1 = a1.shape
    k2 = a2.shape[1]
    n = w.shape[1]
    vmem = 2 * tm * (k1 + k2) * 2 + 2 * (k1 + k2) * tn * 2 + 2 * tm * tn * (4 + 4 + 2) + 3 * tm * tn * 4 + (4 << 20)
    return pl.pallas_call(
        _out_proj_kernel,
        grid=(m // tm, n // tn),
        in_specs=[pl.BlockSpec((tm, k1), lambda i, j: (i, 0)), pl.BlockSpec((tm, k2), lambda i, j: (i, 0)),
                  pl.BlockSpec((k1 + k2, tn), lambda i, j: (0, j)),
                  pl.BlockSpec((tm, tn), lambda i, j: (i, j))],
        out_specs=[pl.BlockSpec((tm, tn), lambda i, j: (i, j)), pl.BlockSpec((tm, tn), lambda i, j: (i, j))],
        out_shape=[jax.ShapeDtypeStruct((m, n), F32), jax.ShapeDtypeStruct((m, n), BF16)],
        scratch_shapes=[pltpu.VMEM((2, tm, LANES), F32)],
        compiler_params=_cparams(("parallel", "arbitrary"), vmem),
        name="out_proj",
    )(a1, a2, w, res)


def _ffn_up_kernel(a_ref, wg_ref, wu_ref, wd_ref, o_ref, wd_bf16_ref, r_ref):
    @pl.when(pl.program_id(1) == 0)
    def _():
        r_ref[...] = _row_scale(a_ref)

    a = a_ref[...]
    r = _lanes(r_ref[...], o_ref.shape[1])
    gate = r * _dot(a, wg_ref[...])
    up = r * _dot(a, wu_ref[...])
    o_ref[...] = (jax.nn.silu(gate) * up).astype(o_ref.dtype)
    wd_bf16_ref[...] = wd_ref[...].astype(wd_bf16_ref.dtype)


def _ffn_up(a, w_gate, w_up, w_down, layer, tm=2048, tn=2 * LANES):
    m, k = a.shape
    n = w_gate.shape[1]
    d_out = w_down.shape[2]
    nj = n // tn
    slab = n // ((m // tm) * nj)
    assert slab * (m // tm) * nj == n and slab % (2 * SUBLANES) == 0
    vmem = 2 * tm * k * 2 + 2 * 2 * k * tn * 2 + 2 * tm * tn * 2 + 5 * tm * tn * 4 + 2 * slab * d_out * 6 + (4 << 20)
    w_spec = pl.BlockSpec((k, tn), lambda i, j: (0, j))
    return pl.pallas_call(
        _ffn_up_kernel,
        grid=(m // tm, nj),
        in_specs=[pl.BlockSpec((tm, k), lambda i, j: (i, 0)), w_spec, w_spec,
                  pl.BlockSpec((None, slab, d_out), lambda i, j: (layer, i * nj + j, 0))],
        out_specs=[pl.BlockSpec((tm, tn), lambda i, j: (i, j)),
                   pl.BlockSpec((slab, d_out), lambda i, j: (i * nj + j, 0))],
        out_shape=[jax.ShapeDtypeStruct((m, n), BF16), jax.ShapeDtypeStruct((n, d_out), BF16)],
        scratch_shapes=[pltpu.VMEM((tm, LANES), F32)],
        compiler_params=_cparams(("parallel", "arbitrary"), vmem),
        name="ffn_gate_up",
    )(a, w_gate, w_up, w_down)


IN_PROJ_SLABS = 16


def _in_proj_kernel(a_ref, bt_ref, cs_ref, wa_ref, wb_ref, wgain_ref, o_ref, wa_bf16_ref, wb_bf16_ref):
    o_ref[...] = (_dot_nt(a_ref[...], bt_ref[...].astype(BF16)) * cs_ref[...]).astype(o_ref.dtype)

    @pl.when(pl.program_id(1) < IN_PROJ_SLABS)
    def _():
        _gained_slab(wa_ref, wgain_ref, wa_bf16_ref)
        _gained_slab(wb_ref, wgain_ref, wb_bf16_ref)


def _in_proj(a, w_t, layer, n, col_scale, w_a, w_b, next_gains, tm=1024, tn=512):
    m, k = a.shape
    ni, nj = m // tm, n // tn
    assert nj >= IN_PROJ_SLABS
    w_spec, g_spec, wo_spec, wo_shape, slab_bytes = _slab_specs(
        w_a, layer, ni * IN_PROJ_SLABS, lambda i, j: i * IN_PROJ_SLABS + jnp.minimum(j, IN_PROJ_SLABS - 1))
    vmem = 2 * tm * k * 2 + tn * k * (2 * 4 + 2) + 2 * tm * tn * 2 + 2 * tm * tn * 4 + 4 * slab_bytes + (4 << 20)
    return pl.pallas_call(
        _in_proj_kernel,
        grid=(ni, nj),
        in_specs=[pl.BlockSpec((tm, k), lambda i, j: (i, 0)),
                  pl.BlockSpec((None, tn, k), lambda i, j: (layer, j, 0)),
                  pl.BlockSpec((1, tn), lambda i, j: (0, j)),
                  w_spec, w_spec, g_spec],
        out_specs=[pl.BlockSpec((tm, tn), lambda i, j: (i, j)), wo_spec, wo_spec],
        out_shape=[jax.ShapeDtypeStruct((m, n), BF16), wo_shape, wo_shape],
        compiler_params=_cparams(("parallel", "arbitrary"), vmem),
        name="in_proj",
    )(a, w_t, col_scale, w_a, w_b, _row_gains(next_gains))


def _mm_sigmoid_kernel(a_ref, b_ref, o_ref):
    o_ref[...] = jax.nn.sigmoid(_dot(a_ref[...], b_ref[...].astype(BF16)))


def _ffn_down_kernel(a_ref, b_ref, res_ref, o_ref, acc_ref):
    kk = pl.program_id(2)

    @pl.when(kk == 0)
    def _():
        acc_ref[...] = res_ref[...]

    acc_ref[...] += _dot(a_ref[...], b_ref[...])

    @pl.when(kk == pl.num_programs(2) - 1)
    def _():
        o_ref[...] = acc_ref[...]


def _ffn_down(a, b, res, tm=1024, tn=512):
    m, k = a.shape
    n = b.shape[1]
    tk = k // 2
    assert tk % LANES == 0
    vmem = 2 * tm * tk * 2 + 2 * tk * tn * 2 + 6 * tm * tn * 4 + (4 << 20)
    return pl.pallas_call(
        _ffn_down_kernel,
        grid=(m // tm, n // tn, k // tk),
        in_specs=[pl.BlockSpec((tm, tk), lambda i, j, kk: (i, kk)),
                  pl.BlockSpec((tk, tn), lambda i, j, kk: (kk, j)),
                  pl.BlockSpec((tm, tn), lambda i, j, kk: (i, j))],
        out_specs=pl.BlockSpec((tm, tn), lambda i, j, kk: (i, j)),
        out_shape=jax.ShapeDtypeStruct((m, n), F32),
        scratch_shapes=[pltpu.VMEM((tm, tn), F32)],
        compiler_params=_cparams(("parallel", "parallel", "arbitrary"), vmem),
        name="ffn_down",
    )(a, b, res)


def _gate_proj(a, w, tm=1024):
    m, k = a.shape
    n = w.shape[1]
    vmem = 2 * tm * k * 2 + k * n * (2 * 4 + 2) + 4 * tm * n * 4 + (4 << 20)
    return pl.pallas_call(
        _mm_sigmoid_kernel,
        grid=(m // tm,),
        in_specs=[pl.BlockSpec((tm, k), lambda i: (i, 0)), pl.BlockSpec((k, n), lambda i: (0, 0))],
        out_specs=pl.BlockSpec((tm, n), lambda i: (i, 0)),
        out_shape=jax.ShapeDtypeStruct((m, n), F32),
        compiler_params=_cparams(("parallel",), vmem),
        name="gate_proj",
    )(a, w)


def _sb_kernel(q_ref, k_ref, v_ref, u_ref, w_ref, wgain_ref, o_ref, w_bf16_ref, carry_ref, acc_ref, *, hg):
    _gained_slab(w_ref, wgain_ref, w_bf16_ref)
    step = pl.program_id(1)
    u = u_ref[...]
    heads = [slice(h * HEAD_DIM, (h + 1) * HEAD_DIM) for h in range(hg)]

    def logs(z):
        log1p_e = jnp.log(1.0 + jnp.exp(-jnp.abs(z)))
        log_sig = jnp.minimum(z, 0.0) - log1p_e
        return log_sig, log_sig - z

    def tile_sums(log_not, n):
        hi, lo = _split_bf16(log_not)
        return [jnp.concatenate([hi[:, t * TK:(t + 1) * TK], lo[:, t * TK:(t + 1) * TK]], axis=1) for t in range(n)]

    def window(sub):
        qt = step * SB_QT + sub
        rows = slice(sub * TQ, (sub + 1) * TQ)
        base = jnp.maximum(qt - (SB_WIN - 1), 0)
        row0 = pl.multiple_of(base * TK, TK)
        wk = SB_WIN * TK
        key_pos = base * TK + lax.broadcasted_iota(jnp.int32, (TQ, wk), 1)
        visible = key_pos < qt * TQ + lax.broadcasted_iota(jnp.int32, (TQ, wk), 0)
        zs = [jnp.where(visible, _dot_nt(q_ref[rows, cs], k_ref[pl.ds(row0, wk), cs]), NEG_INF) for cs in heads]
        lg = [logs(z) for z in zs]
        lhs = [part for (_, log_not) in lg for part in tile_sums(log_not, SB_WIN)]
        sums = _by_row_halves(_dot, jnp.concatenate(lhs, axis=0), u)
        worst = None
        for h, cs in enumerate(heads):
            carry = None
            shifted = [None] * SB_WIN
            for t in reversed(range(SB_WIN)):
                blk = sums[(h * SB_WIN + t) * TQ:(h * SB_WIN + t + 1) * TQ]
                tail, total = blk[:, :TK], blk[:, TK:]
                shifted[t] = tail if carry is None else tail + carry
                carry = total if carry is None else carry + total
            a = jnp.exp(lg[h][0] + jnp.concatenate(shifted, axis=1))
            acc_ref[sub * hg + h] = _dot(a.astype(BF16), v_ref[pl.ds(row0, wk), cs])
            carry_ref[sub * hg + h] = carry
            worst = carry if worst is None else jnp.maximum(worst, carry)
        return base, jnp.max(worst)

    def far_sweep(sub, base, worst):
        rows = slice(sub * TQ, (sub + 1) * TQ)

        def tile(kb):
            r0 = pl.multiple_of(kb * TK, TK)
            lg1 = [logs(_dot_nt(q_ref[rows, cs], k_ref[pl.ds(r0, TK), cs])) for cs in heads]
            sums1 = _dot(jnp.concatenate([tile_sums(log_not, 1)[0] for (_, log_not) in lg1], axis=0), u)
            far = None
            for h, cs in enumerate(heads):
                blk = sums1[h * TQ:(h + 1) * TQ]
                carry = carry_ref[sub * hg + h]
                a = jnp.exp(lg1[h][0] + blk[:, :TK] + carry)
                acc_ref[sub * hg + h] += _dot(a.astype(BF16), v_ref[pl.ds(r0, TK), cs])
                carry = carry + blk[:, TK:]
                carry_ref[sub * hg + h] = carry
                far = carry if far is None else jnp.maximum(far, carry)
            return jnp.max(far)

        def cond(c):
            kb, far = c
            return jnp.logical_and(kb >= 0, far > SB_DEAD_LOG)

        def body(c):
            kb, _ = c
            return kb - 1, tile(kb)

        lax.while_loop(cond, body, (base - 1, worst))

    starts = [window(sub) for sub in range(SB_QT)]
    for sub, (base, worst) in enumerate(starts):
        far_sweep(sub, base, worst)
    for sub in range(SB_QT):
        for h, cs in enumerate(heads):
            o_ref[sub * TQ:(sub + 1) * TQ, cs] = acc_ref[sub * hg + h].astype(o_ref.dtype)


def _sb_attention(proj, w_next, layer, next_gains, hg=4):
    s = proj.shape[0]
    w = hg * HEAD_DIM
    nblk = SB_W // w
    tri = np.arange(TK)[:, None] > np.arange(TK)[None, :]
    half = np.concatenate([tri, np.ones((TK, TK), bool)], axis=1)
    u = jnp.asarray(np.concatenate([half, half], axis=0), dtype=BF16)
    tq = SB_QT * TQ
    nq = s // tq
    w_spec, g_spec, wo_spec, wo_shape, slab_bytes = _slab_specs(w_next, layer, nblk * nq, lambda g, i: g * nq + i)
    vmem = (2 * 2 * s * w * 2 + 2 * tq * w * 2 + 2 * tq * w * 4 + 2 * SB_QT * hg * TQ * TK * 4 + 2 * slab_bytes
            + (16 << 20))
    return pl.pallas_call(
        functools.partial(_sb_kernel, hg=hg),
        grid=(nblk, nq),
        in_specs=[pl.BlockSpec((tq, w), lambda g, i: (i, g)),
                  pl.BlockSpec((s, w), lambda g, i: (0, nblk + g)),
                  pl.BlockSpec((s, w), lambda g, i: (0, 2 * nblk + g)),
                  pl.BlockSpec((2 * TK, 2 * TK), lambda g, i: (0, 0)),
                  w_spec, g_spec],
        out_specs=[pl.BlockSpec((tq, w), lambda g, i: (i, g)), wo_spec],
        out_shape=[jax.ShapeDtypeStruct((s, SB_W), BF16), wo_shape],
        scratch_shapes=[pltpu.VMEM((SB_QT * hg, TQ, TK), F32), pltpu.VMEM((SB_QT * hg, TQ, HEAD_DIM), F32)],
        compiler_params=_cparams(("parallel", "arbitrary"), vmem),
        name="sb_attention",
    )(proj, proj, proj, u, w_next, _row_gains(next_gains))


def _compress_kernel(r_ref, pos_ref, w1_ref, w2_ref, o_ref):
    r = r_ref[0]
    half = r.shape[1]
    first = _dot(r, w1_ref[:half, :])
    second = _dot(r, w1_ref[half:, :])
    second = pltpu.roll(second, second.shape[0] - 1, 0)
    pos_term = _dot(pos_ref[...].astype(BF16), w1_ref[...])[0:1, :]
    hidden = jax.nn.gelu(first + second + pos_term)
    o_ref[0] = _dot(hidden.astype(BF16), w2_ref[...]).astype(o_ref.dtype)


def _compress(r, pos, w1, w2):
    g, n, half = r.shape
    pos_rows = jnp.zeros((SUBLANES, 2 * half), F32).at[0].set(pos.reshape(-1))
    return pl.pallas_call(
        _compress_kernel,
        grid=(g,),
        in_specs=[pl.BlockSpec((1, n, half), lambda i: (i, 0, 0)),
                  pl.BlockSpec((SUBLANES, 2 * half), lambda i: (0, 0)),
                  pl.BlockSpec((2 * half, CMP_HIDDEN), lambda i: (0, 0)),
                  pl.BlockSpec((CMP_HIDDEN, HEAD_DIM), lambda i: (0, 0))],
        out_specs=pl.BlockSpec((1, n, HEAD_DIM), lambda i: (i, 0, 0)),
        out_shape=jax.ShapeDtypeStruct((g, n, HEAD_DIM), BF16),
        compiler_params=_cparams(("parallel",), 24 << 20),
        name="nsa_compress",
    )(r, pos_rows, w1.astype(BF16), w2.astype(BF16))


CMP_QT = 4


def _nsa_cmp_kernel(slopes_ref, q_ref, kc_ref, vc_ref, msel_ref, oc_ref, sel_ref, flag_ref, qs_ref, *, n_top):
    n_cmp = kc_ref.shape[1]
    n_slc = msel_ref.shape[1]
    msel = msel_ref[...]
    rows = NSA_HPG * TQ
    t_rel = lax.broadcasted_iota(jnp.int32, (TQ, n_cmp), 0)
    blk = lax.broadcasted_iota(jnp.int32, (n_slc, TQ), 0)
    blk_f = blk.astype(F32)
    for sub in range(CMP_QT):
        i = pl.program_id(0) * CMP_QT + sub
        tile = slice(sub * TQ, (sub + 1) * TQ)
        end_rel = lax.broadcasted_iota(jnp.int32, (TQ, n_cmp), 1) * STRIDE_CMP + (L_CMP - 1) - i * TQ
        bias = jnp.where(end_rel <= t_rel, 0.0, NEG_INF)
        end_row = end_rel[0:1, :].astype(F32)
        t_col = i * TQ + lax.broadcasted_iota(jnp.int32, (TQ, 1), 0)
        row_valid = jnp.where(t_col >= L_CMP - 1, 1.0, 0.0)
        cur = lax.shift_right_logical(i * TQ + lax.broadcasted_iota(jnp.int32, (n_slc, TQ), 1), LOG2_L_SLC)
        valid = blk <= cur
        bonus = jnp.where(blk == 0, FORCE_BONUS, 0.0)
        bonus = jnp.where(blk == cur, FORCE_BONUS, bonus)
        bonus = jnp.where(blk == cur - 1, FORCE_BONUS, bonus)
        for g in range(NSA_KV):
            base = (sub * NSA_KV + g) * rows
            for h in range(NSA_HPG):
                c0 = (g * NSA_HPG + h) * HEAD_DIM
                qs_ref[base + h * TQ:base + (h + 1) * TQ, :] = q_ref[tile, c0:c0 + HEAD_DIM]
            z = _by_row_halves(_dot_nt, qs_ref[base:base + rows, :], kc_ref[g])
            p_sum = jnp.zeros((TQ, n_cmp), F32)
            ps = []
            for h in range(NSA_HPG):
                s = z[h * TQ:(h + 1) * TQ] + (bias + slopes_ref[g * NSA_HPG + h] * end_row)
                m = jnp.max(s, axis=-1, keepdims=True)
                e = jnp.exp(s - m)
                p = e * (row_valid / jnp.sum(e, axis=-1, keepdims=True))
                ps.append(p.astype(BF16))
                p_sum = p_sum + p
            oc = _by_row_halves(_dot, jnp.concatenate(ps, axis=0), vc_ref[g])
            for h in range(NSA_HPG):
                c0 = (g * NSA_HPG + h) * HEAD_DIM
                oc_ref[tile, c0:c0 + HEAD_DIM] = oc[h * TQ:(h + 1) * TQ]
            hi, lo = _split_bf16(p_sum)
            imp = _dot(hi, msel) + _dot(lo, msel)
            score = jnp.where(valid, imp.T + bonus, NEG_INF)
            taken = jnp.zeros((n_slc, TQ), F32)
            for _ in range(n_top):
                best = jnp.max(score, axis=0, keepdims=True)
                first = jnp.min(jnp.where(score == best, blk_f, float(n_slc)), axis=0, keepdims=True)
                hit = blk_f == first
                taken = jnp.where(hit, 1.0, taken)
                score = jnp.where(hit, -jnp.inf, score)
            chosen_q = jnp.where(valid, taken, 0.0).T
            sel_ref[g, sub] = chosen_q.astype(sel_ref.dtype)
            flag_ref[g, sub] = jnp.max(chosen_q, axis=0, keepdims=True)


def _select_matrix(n_cmp_pad, n_slc):
    rs = L_SLC // STRIDE_CMP
    rc = L_CMP // STRIDE_CMP
    m = np.zeros((n_cmp_pad, n_slc), np.float32)
    for j in range(n_slc):
        for a in range(rs):
            for b in range(rc):
                src = rs * j - a - b
                if 0 <= src < n_cmp_pad - 1:
                    m[src, j] += 1.0
    return jnp.asarray(m, dtype=BF16)


def _nsa_cmp(proj, k_cmp, v_cmp, slopes):
    s = proj.shape[0]
    n_cmp = k_cmp.shape[1]
    n_slc = s // L_SLC
    nq = s // TQ
    msel = _select_matrix(n_cmp, n_slc)
    qblk = (3 * SB_W) // NSA_W
    return pl.pallas_call(
        functools.partial(_nsa_cmp_kernel, n_top=min(N_TOPK, n_slc)),
        grid=(nq // CMP_QT,),
        in_specs=[pl.BlockSpec(memory_space=pltpu.SMEM),
                  pl.BlockSpec((CMP_QT * TQ, NSA_W), lambda i: (i, qblk)),
                  pl.BlockSpec((NSA_KV, n_cmp, HEAD_DIM), lambda i: (0, 0, 0)),
                  pl.BlockSpec((NSA_KV, n_cmp, HEAD_DIM), lambda i: (0, 0, 0)),
                  pl.BlockSpec((n_cmp, n_slc), lambda i: (0, 0))],
        out_specs=[pl.BlockSpec((CMP_QT * TQ, NSA_W), lambda i: (i, 0)),
                   pl.BlockSpec((NSA_KV, CMP_QT, TQ, n_slc), lambda i: (0, i, 0, 0)),
                   pl.BlockSpec((NSA_KV, CMP_QT, 1, n_slc), lambda i: (0, i, 0, 0))],
        out_shape=[jax.ShapeDtypeStruct((s, NSA_W), F32),
                   jax.ShapeDtypeStruct((NSA_KV, nq, TQ, n_slc), BF16),
                   jax.ShapeDtypeStruct((NSA_KV, nq, 1, n_slc), F32)],
        scratch_shapes=[pltpu.VMEM((CMP_QT * NSA_HEADS * TQ, HEAD_DIM), BF16)],
        compiler_params=_cparams(("parallel",), 48 << 20),
        name="nsa_cmp_select",
    )(slopes, proj, k_cmp, v_cmp, msel)


def _nsa_main_kernel(tiles_ref, count_ref, slopes_ref, q_ref, ks_ref, vs_ref, kw_ref, vw_ref, sel_ref, oc_ref,
                     gate_ref, o_ref, qa_ref, acc_s, m_s):
    g = pl.program_id(0)
    i = pl.program_id(1)
    n_kt = pl.num_programs(1) * TQ // NSA_TK
    n_slc = sel_ref.shape[3]
    not_picked = jnp.where(sel_ref[0, 0].astype(F32) > 0.5, 0.0, NEG_INF).astype(BF16)
    for h in range(NSA_HPG):
        qa_ref[h * TQ:(h + 1) * TQ, :HEAD_DIM] = q_ref[:, h * HEAD_DIM:(h + 1) * HEAD_DIM]
        qa_ref[h * TQ:(h + 1) * TQ, HEAD_DIM:] = not_picked
    slopes = [slopes_ref[g * NSA_HPG + h] for h in range(NSA_HPG)]
    diag = lax.div(i * TQ, NSA_TK)

    def scores(k_ref, v_ref, pieces, bias, select):
        zs = []
        for r0, n in pieces:
            k = k_ref[pl.ds(r0, n), :]
            if select:
                blk_of_key = lax.shift_right_logical(r0 + lax.broadcasted_iota(jnp.int32, (n, n_slc), 0), LOG2_L_SLC)
                one_hot = jnp.where(lax.broadcasted_iota(jnp.int32, (n, n_slc), 1) == blk_of_key, 1.0, 0.0)
                zs.append(_by_row_halves(_dot_nt, qa_ref[...], jnp.concatenate([k, one_hot.astype(BF16)], axis=1)))
            else:
                zs.append(_by_row_halves(_dot_nt, qa_ref[:, :HEAD_DIM], k))
        z = jnp.concatenate(zs, axis=1)
        v_aug = jnp.concatenate(
            [jnp.concatenate([v_ref[pl.ds(r0, n), :], jnp.ones((n, HEAD_DIM), BF16)], axis=1) for r0, n in pieces],
            axis=0)
        key_rel = jnp.concatenate(
            [(r0 - i * TQ + lax.broadcasted_iota(jnp.int32, (1, n), 1)).astype(F32) for r0, n in pieces], axis=1)
        if bias is None:
            return [z[h * TQ:(h + 1) * TQ] + slopes[h] * key_rel for h in range(NSA_HPG)], v_aug
        return [z[h * TQ:(h + 1) * TQ] + (bias + slopes[h] * key_rel) for h in range(NSA_HPG)], v_aug

    def attend(pieces, bias, first):
        s_all, v_aug = scores(ks_ref, vs_ref, pieces, bias, True)
        width = s_all[0].shape[1]
        ps, alphas = [], []
        for h, s in enumerate(s_all):
            rows = slice(h * TQ, (h + 1) * TQ)
            m_tile = jnp.max(s, axis=-1, keepdims=True)
            if first:
                m_new = jnp.broadcast_to(m_tile, (TQ, LANES))
            else:
                m_old = m_s[rows]
                m_new = jnp.maximum(m_old, m_tile)
                alphas.append(jnp.exp(m_old - m_new))
            m_s[rows] = m_new
            ps.append(jnp.exp(s - jnp.concatenate([m_new] * (width // LANES), axis=1)).astype(BF16))
        pv = _by_row_halves(_dot, jnp.concatenate(ps, axis=0), v_aug)
        for h in range(NSA_HPG):
            rows = slice(h * TQ, (h + 1) * TQ)
            if first:
                acc_s[rows] = pv[rows]
            else:
                acc_s[rows] = acc_s[rows] * jnp.concatenate([alphas[h], alphas[h]], axis=1) + pv[rows]

    def tile_rows(kt):
        return pl.multiple_of(kt * NSA_TK, NSA_TK), NSA_TK

    near = jnp.maximum(diag - 1, 0)
    near0 = pl.multiple_of(near * NSA_TK, NSA_TK)
    nw = 2 * NSA_TK
    ahead = (near0 + lax.broadcasted_iota(jnp.int32, (TQ, nw), 1)) - (i * TQ + lax.broadcasted_iota(jnp.int32, (TQ, nw), 0))
    attend([(near0, nw)], jnp.where(ahead <= 0, 0.0, NEG_INF), True)
    step = g * pl.num_programs(1) + i
    n_far = count_ref[step]

    def pair_body(p, _):
        ta = tiles_ref[step * n_kt + 2 * p]
        tb = tiles_ref[step * n_kt + 2 * p + 1]
        attend([tile_rows(ta), tile_rows(tb)], None, False)
        return 0

    lax.fori_loop(0, lax.shift_right_logical(n_far, 1), pair_body, 0)

    @pl.when(lax.rem(n_far, 2) == 1)
    def _():
        attend([tile_rows(tiles_ref[step * n_kt + n_far - 1])], None, False)

    wn = WINDOW + TQ
    w0 = pl.multiple_of(jnp.maximum(i * TQ - WINDOW, 0), TQ)
    dist = i * TQ + lax.broadcasted_iota(jnp.int32, (TQ, wn), 0) - (w0 + lax.broadcasted_iota(jnp.int32, (TQ, wn), 1))
    w_bias = jnp.where(dist >= 0, jnp.where(dist < WINDOW, 0.0, NEG_INF), NEG_INF)
    s_win, v_win = scores(kw_ref, vw_ref, [(w0, wn)], w_bias, False)
    p_win = [jnp.exp(s - jnp.max(s, axis=-1, keepdims=True)).astype(BF16) for s in s_win]
    pv_win = _by_row_halves(_dot, jnp.concatenate(p_win, axis=0), v_win)

    gates = gate_ref[...]
    for h in range(NSA_HPG):
        cs = slice(h * HEAD_DIM, (h + 1) * HEAD_DIM)
        a_s = acc_s[h * TQ:(h + 1) * TQ]
        a_w = pv_win[h * TQ:(h + 1) * TQ]
        o_sel = a_s[:, :HEAD_DIM] / a_s[:, HEAD_DIM:HEAD_DIM + 1]
        o_win = a_w[:, :HEAD_DIM] / a_w[:, HEAD_DIM:HEAD_DIM + 1]
        o_ref[:, cs] = (gates[:, 3 * h:3 * h + 1] * oc_ref[:, cs] + gates[:, 3 * h + 1:3 * h + 2] * o_sel
                        + gates[:, 3 * h + 2:3 * h + 3] * o_win).astype(o_ref.dtype)


def _far_tile_lists(tile_any, s):
    nq = s // TQ
    n_kt = s // NSA_TK
    hit = tile_any.reshape(NSA_KV, nq, n_kt, NSA_TK // L_SLC).max(axis=-1) > 0.5
    kt = jnp.arange(n_kt, dtype=jnp.int32)
    diag = (jnp.arange(nq, dtype=jnp.int32) * TQ) // NSA_TK
    far = jnp.logical_and(hit, kt[None, None, :] < diag[None, :, None] - 1)
    tiles = jnp.sort(jnp.where(far, kt[None, None, :], n_kt), axis=-1)
    tiles = jnp.minimum(tiles, n_kt - 1)
    return tiles.reshape(-1), jnp.sum(far, axis=-1, dtype=jnp.int32).reshape(-1)


def _nsa_main(proj, sel, tile_any, o_c, gates, slopes):
    s = proj.shape[0]
    nq = s // TQ
    n_slc = s // L_SLC
    qblk = (3 * SB_W) // GROUP_W
    tiles, counts = _far_tile_lists(tile_any, s)
    kv_spec = lambda cb: pl.BlockSpec((s, HEAD_DIM), lambda g, i, t, c: (0, cb + g))
    rows = NSA_HPG * TQ
    grid_spec = pltpu.PrefetchScalarGridSpec(
        num_scalar_prefetch=2,
        grid=(NSA_KV, nq),
        in_specs=[pl.BlockSpec(memory_space=pltpu.SMEM),
                  pl.BlockSpec((TQ, GROUP_W), lambda g, i, t, c: (i, qblk + g)),
                  kv_spec(CB_KS), kv_spec(CB_VS), kv_spec(CB_KW), kv_spec(CB_VW),
                  pl.BlockSpec((1, 1, TQ, n_slc), lambda g, i, t, c: (g, i, 0, 0)),
                  pl.BlockSpec((TQ, GROUP_W), lambda g, i, t, c: (i, g)),
                  pl.BlockSpec((TQ, LANES), lambda g, i, t, c: (i, g))],
        out_specs=pl.BlockSpec((TQ, GROUP_W), lambda g, i, t, c: (i, g)),
        scratch_shapes=[pltpu.VMEM((rows, HEAD_DIM + n_slc), BF16),
                        pltpu.VMEM((rows, 2 * HEAD_DIM), F32), pltpu.VMEM((rows, LANES), F32)],
    )
    vmem = 2 * 4 * s * HEAD_DIM * 2 + (24 << 20)
    return pl.pallas_call(
        _nsa_main_kernel,
        grid_spec=grid_spec,
        out_shape=jax.ShapeDtypeStruct((s, NSA_W), BF16),
        compiler_params=_cparams(("parallel", "arbitrary"), vmem),
        name="nsa_select_window",
    )(tiles, counts, slopes, proj, proj, proj, proj, proj, sel, o_c, gates)


def kernel(x, attn_norm, w_in, pos_cmp_k, pos_cmp_v, w_cmp_k1, w_cmp_k2, w_cmp_v1, w_cmp_v2, norm_sb, norm_nsa,
           w_out, ffn_norm, w_gate, w_up, w_down, final_norm):
    batch, s, d_model = x.shape
    assert batch == 1 and s % 1024 == 0 and w_in.shape[2] == MAIN_COLS + GATE_COLS
    depth = w_in.shape[0]
    d_ff = w_gate.shape[2]
    assert d_ff % (2 * LANES) == 0 and (d_ff // 2) % LANES == 0
    scale = HEAD_DIM ** -0.5
    head_idx = jnp.arange(1, NSA_HEADS + 1, dtype=F32)
    slopes = 2.0 ** (-8.0 * head_idx / NSA_HEADS)
    col_scale = jnp.concatenate([jnp.full((SB_W,), scale, F32), jnp.ones((2 * SB_W,), F32),
                                 jnp.full((NSA_W,), scale, F32), jnp.ones((MAIN_COLS - 3 * SB_W - NSA_W,), F32)])
    col_scale = col_scale.reshape(1, MAIN_COLS)
    xs = x[0]
    w_in_t = jnp.swapaxes(w_in, 1, 2)
    for l in range(depth):
        w_g = w_in[l][:, MAIN_COLS:].reshape(d_model, NSA_KV, 3 * NSA_HPG)
        w_g = jnp.pad(w_g, ((0, 0), (0, 0), (0, LANES - 3 * NSA_HPG))).reshape(d_model, NSA_KV * LANES)

        h = _rmsnorm(xs, attn_norm[l], BF16)
        proj, w_gate_bf16, w_up_bf16 = _in_proj(h, w_in_t, l, MAIN_COLS, col_scale, w_gate, w_up, ffn_norm[l])
        gates = _gate_proj(h, w_g)

        o_sb, w_out_bf16 = _sb_attention(proj, w_out, l, jnp.concatenate([norm_sb[l], norm_nsa[l]]))

        def blocks_view(cb):
            cols = proj[:, cb * HEAD_DIM:(cb + NSA_KV) * HEAD_DIM]
            r = cols.reshape(s // STRIDE_CMP, STRIDE_CMP, NSA_KV, HEAD_DIM).transpose(2, 0, 1, 3)
            return r.reshape(NSA_KV, s // STRIDE_CMP, STRIDE_CMP * HEAD_DIM)

        k_cmp = _compress(blocks_view(CB_KC), pos_cmp_k[l], w_cmp_k1[l], w_cmp_k2[l])
        v_cmp = _compress(blocks_view(CB_VC), pos_cmp_v[l], w_cmp_v1[l], w_cmp_v2[l])
        o_c, sel, tile_any = _nsa_cmp(proj, k_cmp, v_cmp, slopes)
        o_nsa = _nsa_main(proj, sel, tile_any, o_c, gates, slopes)

        xs, xs_bf16 = _out_proj(o_sb, o_nsa, w_out_bf16, xs)
        act, w_down_bf16 = _ffn_up(xs_bf16, w_gate_bf16, w_up_bf16, w_down, l)
        xs = _ffn_down(act, w_down_bf16, xs)
    return _rmsnorm(xs, final_norm, F32)[None]
```

```python
import functools

import numpy as np
import jax
import jax.numpy as jnp
from jax import lax
from jax.experimental import pallas as pl
from jax.experimental.pallas import tpu as pltpu

HEAD_DIM = 128
SB_HEADS = 16
NSA_HEADS = 16
NSA_KV = 2
NSA_HPG = NSA_HEADS // NSA_KV
L_CMP = 32
STRIDE_CMP = 16
CMP_HIDDEN = 256
L_SLC = 64
LOG2_L_SLC = 6
N_TOPK = 16
WINDOW = 512
EPS = 1e-6
NEG_INF = -1e30
FORCE_BONUS = 1e6

LANES = 128
SUBLANES = 8
TQ = 128
TK = 128
SB_WIN = 3
SB_QT = 8
NSA_TK = 256
NSA_QT = 2
VMEM_CAP = 56 * 1024 * 1024
SB_DEAD_LOG = -105.0

F32 = jnp.float32
BF16 = jnp.bfloat16

SB_W = SB_HEADS * HEAD_DIM
NSA_W = NSA_HEADS * HEAD_DIM
GROUP_W = NSA_HPG * HEAD_DIM
MAIN_COLS = 3 * SB_W + NSA_W + 3 * 2 * NSA_KV * HEAD_DIM
GATE_COLS = 3 * NSA_HEADS
CB_KC, CB_VC, CB_KS, CB_VS, CB_KW, CB_VW = 64, 66, 68, 70, 72, 74


def _cparams(sem, vmem_bytes):
    return pltpu.CompilerParams(dimension_semantics=sem, vmem_limit_bytes=int(min(vmem_bytes, VMEM_CAP)))


def _dot(a, b):
    return jnp.dot(a, b, preferred_element_type=F32)


def _dot_nt(a, b):
    return lax.dot_general(a, b, (((1,), (1,)), ((), ())), preferred_element_type=F32)


def _by_row_halves(dot_fn, a, b):
    half = a.shape[0] // 2
    return jnp.concatenate([dot_fn(a[:half], b), dot_fn(a[half:], b)], axis=0)


def _split_bf16(x):
    hi = x.astype(BF16)
    lo = (x - hi.astype(F32)).astype(BF16)
    return hi, lo


def _rms_kernel(x_ref, g_ref, o_ref):
    x = x_ref[...]
    ms = jnp.mean(x * x, axis=-1, keepdims=True)
    o_ref[...] = (x * lax.rsqrt(ms + EPS) * g_ref[...]).astype(o_ref.dtype)


def _rmsnorm(x, g, out_dtype, tm=256):
    m, d = x.shape
    return pl.pallas_call(
        _rms_kernel,
        grid=(m // tm,),
        in_specs=[pl.BlockSpec((tm, d), lambda i: (i, 0)), pl.BlockSpec((1, d), lambda i: (0, 0))],
        out_specs=pl.BlockSpec((tm, d), lambda i: (i, 0)),
        out_shape=jax.ShapeDtypeStruct((m, d), out_dtype),
        compiler_params=_cparams(("parallel",), 6 * tm * d * 4),
        name="rmsnorm",
    )(x, g.reshape(1, d))


def _row_scale(a_ref):
    a = a_ref[...].astype(F32)
    ms = jnp.mean(a * a, axis=-1, keepdims=True)
    return jnp.broadcast_to(lax.rsqrt(ms + EPS), (a.shape[0], LANES))


def _lanes(x, n):
    return jnp.concatenate([x] * (n // LANES), axis=1)


def _gained_slab(w_ref, g_ref, o_ref):
    o_ref[...] = (w_ref[...] * _lanes(g_ref[...], w_ref.shape[1])).astype(o_ref.dtype)


def _slab_specs(w, layer, n_steps, step_of):
    rows, cols = w.shape[1], w.shape[2]
    slab = rows // n_steps
    assert slab * n_steps == rows and slab % (2 * SUBLANES) == 0
    return (pl.BlockSpec((None, slab, cols), lambda *idx: (layer, step_of(*idx), 0)),
            pl.BlockSpec((slab, LANES), lambda *idx: (step_of(*idx), 0)),
            pl.BlockSpec((slab, cols), lambda *idx: (step_of(*idx), 0)),
            jax.ShapeDtypeStruct((rows, cols), BF16), slab * cols * 6 + slab * LANES * 4)


def _row_gains(g):
    return jnp.broadcast_to(g.reshape(-1, 1), (g.shape[0], LANES))


def _out_proj_kernel(a1_ref, a2_ref, w_ref, res_ref, o_ref, ob_ref, r_ref):
    @pl.when(pl.program_id(1) == 0)
    def _():
        r_ref[0] = _row_scale(a1_ref)
        r_ref[1] = _row_scale(a2_ref)

    k1 = a1_ref.shape[1]
    tn = o_ref.shape[1]
    y1 = _dot(a1_ref[...], w_ref[:k1, :])
    y2 = _dot(a2_ref[...], w_ref[k1:, :])
    out = res_ref[...] + (_lanes(r_ref[0], tn) * y1 + _lanes(r_ref[1], tn) * y2)
    o_ref[...] = out
    ob_ref[...] = out.astype(ob_ref.dtype)


def _out_proj(a1, a2, w, res, tm=1024, tn=512):
    m, k1 = a1.shape
    k2 = a2.shape[1]
    n = w.shape[1]
    vmem = 2 * tm * (k1 + k2) * 2 + 2 * (k1 + k2) * tn * 2 + 2 * tm * tn * (4 + 4 + 2) + 3 * tm * tn * 4 + (4 << 20)
    return pl.pallas_call(
        _out_proj_kernel,
        grid=(m // tm, n // tn),
        in_specs=[pl.BlockSpec((tm, k1), lambda i, j: (i, 0)), pl.BlockSpec((tm, k2), lambda i, j: (i, 0)),
                  pl.BlockSpec((k1 + k2, tn), lambda i, j: (0, j)),
                  pl.BlockSpec((tm, tn), lambda i, j: (i, j))],
        out_specs=[pl.BlockSpec((tm, tn), lambda i, j: (i, j)), pl.BlockSpec((tm, tn), lambda i, j: (i, j))],
        out_shape=[jax.ShapeDtypeStruct((m, n), F32), jax.ShapeDtypeStruct((m, n), BF16)],
        scratch_shapes=[pltpu.VMEM((2, tm, LANES), F32)],
        compiler_params=_cparams(("parallel", "arbitrary"), vmem),
        name="out_proj",
    )(a1, a2, w, res)


def _ffn_up_kernel(a_ref, wg_ref, wu_ref, wd_ref, o_ref, wd_bf16_ref, r_ref):
    @pl.when(pl.program_id(1) == 0)
    def _():
        r_ref[...] = _row_scale(a_ref)

    a = a_ref[...]
    r = _lanes(r_ref[...], o_ref.shape[1])
    gate = r * _dot(a, wg_ref[...])
    up = r * _dot(a, wu_ref[...])
    o_ref[...] = (jax.nn.silu(gate) * up).astype(o_ref.dtype)
    wd_bf16_ref[...] = wd_ref[...].astype(wd_bf16_ref.dtype)


def _ffn_up(a, w_gate, w_up, w_down, layer, tm=2048, tn=2 * LANES):
    m, k = a.shape
    n = w_gate.shape[1]
    d_out = w_down.shape[2]
    nj = n // tn
    slab = n // ((m // tm) * nj)
    assert slab * (m // tm) * nj == n and slab % (2 * SUBLANES) == 0
    vmem = 2 * tm * k * 2 + 2 * 2 * k * tn * 2 + 2 * tm * tn * 2 + 5 * tm * tn * 4 + 2 * slab * d_out * 6 + (4 << 20)
    w_spec = pl.BlockSpec((k, tn), lambda i, j: (0, j))
    return pl.pallas_call(
        _ffn_up_kernel,
        grid=(m // tm, nj),
        in_specs=[pl.BlockSpec((tm, k), lambda i, j: (i, 0)), w_spec, w_spec,
                  pl.BlockSpec((None, slab, d_out), lambda i, j: (layer, i * nj + j, 0))],
        out_specs=[pl.BlockSpec((tm, tn), lambda i, j: (i, j)),
                   pl.BlockSpec((slab, d_out), lambda i, j: (i * nj + j, 0))],
        out_shape=[jax.ShapeDtypeStruct((m, n), BF16), jax.ShapeDtypeStruct((n, d_out), BF16)],
        scratch_shapes=[pltpu.VMEM((tm, LANES), F32)],
        compiler_params=_cparams(("parallel", "arbitrary"), vmem),
        name="ffn_gate_up",
    )(a, w_gate, w_up, w_down)


IN_PROJ_SLABS = 16


def _in_proj_kernel(a_ref, bt_ref, cs_ref, wa_ref, wb_ref, wgain_ref, o_ref, wa_bf16_ref, wb_bf16_ref):
    o_ref[...] = (_dot_nt(a_ref[...], bt_ref[...].astype(BF16)) * cs_ref[...]).astype(o_ref.dtype)

    @pl.when(pl.program_id(1) < IN_PROJ_SLABS)
    def _():
        _gained_slab(wa_ref, wgain_ref, wa_bf16_ref)
        _gained_slab(wb_ref, wgain_ref, wb_bf16_ref)


def _in_proj(a, w_t, layer, n, col_scale, w_a, w_b, next_gains, tm=1024, tn=512):
    m, k = a.shape
    ni, nj = m // tm, n // tn
    assert nj >= IN_PROJ_SLABS
    w_spec, g_spec, wo_spec, wo_shape, slab_bytes = _slab_specs(
        w_a, layer, ni * IN_PROJ_SLABS, lambda i, j: i * IN_PROJ_SLABS + jnp.minimum(j, IN_PROJ_SLABS - 1))
    vmem = 2 * tm * k * 2 + tn * k * (2 * 4 + 2) + 2 * tm * tn * 2 + 2 * tm * tn * 4 + 4 * slab_bytes + (4 << 20)
    return pl.pallas_call(
        _in_proj_kernel,
        grid=(ni, nj),
        in_specs=[pl.BlockSpec((tm, k), lambda i, j: (i, 0)),
                  pl.BlockSpec((None, tn, k), lambda i, j: (layer, j, 0)),
                  pl.BlockSpec((1, tn), lambda i, j: (0, j)),
                  w_spec, w_spec, g_spec],
        out_specs=[pl.BlockSpec((tm, tn), lambda i, j: (i, j)), wo_spec, wo_spec],
        out_shape=[jax.ShapeDtypeStruct((m, n), BF16), wo_shape, wo_shape],
        compiler_params=_cparams(("parallel", "arbitrary"), vmem),
        name="in_proj",
    )(a, w_t, col_scale, w_a, w_b, _row_gains(next_gains))


def _mm_sigmoid_kernel(a_ref, b_ref, o_ref):
    o_ref[...] = jax.nn.sigmoid(_dot(a_ref[...], b_ref[...].astype(BF16)))


def _ffn_down_kernel(a_ref, b_ref, res_ref, o_ref, acc_ref):
    kk = pl.program_id(2)

    @pl.when(kk == 0)
    def _():
        acc_ref[...] = res_ref[...]

    acc_ref[...] += _dot(a_ref[...], b_ref[...])

    @pl.when(kk == pl.num_programs(2) - 1)
    def _():
        o_ref[...] = acc_ref[...]


def _ffn_down(a, b, res, tm=1024, tn=512):
    m, k = a.shape
    n = b.shape[1]
    tk = k // 2
    assert tk % LANES == 0
    vmem = 2 * tm * tk * 2 + 2 * tk * tn * 2 + 6 * tm * tn * 4 + (4 << 20)
    return pl.pallas_call(
        _ffn_down_kernel,
        grid=(m // tm, n // tn, k // tk),
        in_specs=[pl.BlockSpec((tm, tk), lambda i, j, kk: (i, kk)),
                  pl.BlockSpec((tk, tn), lambda i, j, kk: (kk, j)),
                  pl.BlockSpec((tm, tn), lambda i, j, kk: (i, j))],
        out_specs=pl.BlockSpec((tm, tn), lambda i, j, kk: (i, j)),
        out_shape=jax.ShapeDtypeStruct((m, n), F32),
        scratch_shapes=[pltpu.VMEM((tm, tn), F32)],
        compiler_params=_cparams(("parallel", "parallel", "arbitrary"), vmem),
        name="ffn_down",
    )(a, b, res)


def _gate_proj(a, w, tm=1024):
    m, k = a.shape
    n = w.shape[1]
    vmem = 2 * tm * k * 2 + k * n * (2 * 4 + 2) + 4 * tm * n * 4 + (4 << 20)
    return pl.pallas_call(
        _mm_sigmoid_kernel,
        grid=(m // tm,),
        in_specs=[pl.BlockSpec((tm, k), lambda i: (i, 0)), pl.BlockSpec((k, n), lambda i: (0, 0))],
        out_specs=pl.BlockSpec((tm, n), lambda i: (i, 0)),
        out_shape=jax.ShapeDtypeStruct((m, n), F32),
        compiler_params=_cparams(("parallel",), vmem),
        name="gate_proj",
    )(a, w)


def _sb_kernel(q_ref, k_ref, v_ref, u_ref, w_ref, wgain_ref, o_ref, w_bf16_ref, carry_ref, acc_ref, *, hg):
    _gained_slab(w_ref, wgain_ref, w_bf16_ref)
    step = pl.program_id(1)
    u = u_ref[...]
    heads = [slice(h * HEAD_DIM, (h + 1) * HEAD_DIM) for h in range(hg)]

    def logs(z):
        log1p_e = jnp.log(1.0 + jnp.exp(-jnp.abs(z)))
        log_sig = jnp.minimum(z, 0.0) - log1p_e
        return log_sig, log_sig - z

    def tile_sums(log_not, n):
        hi, lo = _split_bf16(log_not)
        return [jnp.concatenate([hi[:, t * TK:(t + 1) * TK], lo[:, t * TK:(t + 1) * TK]], axis=1) for t in range(n)]

    def window(sub):
        qt = step * SB_QT + sub
        rows = slice(sub * TQ, (sub + 1) * TQ)
        base = jnp.maximum(qt - (SB_WIN - 1), 0)
        row0 = pl.multiple_of(base * TK, TK)
        wk = SB_WIN * TK
        key_pos = base * TK + lax.broadcasted_iota(jnp.int32, (TQ, wk), 1)
        visible = key_pos < qt * TQ + lax.broadcasted_iota(jnp.int32, (TQ, wk), 0)
        zs = [jnp.where(visible, _dot_nt(q_ref[rows, cs], k_ref[pl.ds(row0, wk), cs]), NEG_INF) for cs in heads]
        lg = [logs(z) for z in zs]
        lhs = [part for (_, log_not) in lg for part in tile_sums(log_not, SB_WIN)]
        sums = _by_row_halves(_dot, jnp.concatenate(lhs, axis=0), u)
        worst = None
        for h, cs in enumerate(heads):
            carry = None
            shifted = [None] * SB_WIN
            for t in reversed(range(SB_WIN)):
                blk = sums[(h * SB_WIN + t) * TQ:(h * SB_WIN + t + 1) * TQ]
                tail, total = blk[:, :TK], blk[:, TK:]
                shifted[t] = tail if carry is None else tail + carry
                carry = total if carry is None else carry + total
            a = jnp.exp(lg[h][0] + jnp.concatenate(shifted, axis=1))
            acc_ref[sub * hg + h] = _dot(a.astype(BF16), v_ref[pl.ds(row0, wk), cs])
            carry_ref[sub * hg + h] = carry
            worst = carry if worst is None else jnp.maximum(worst, carry)
        return base, jnp.max(worst)

    def far_sweep(sub, base, worst):
        rows = slice(sub * TQ, (sub + 1) * TQ)

        def tile(kb):
            r0 = pl.multiple_of(kb * TK, TK)
            lg1 = [logs(_dot_nt(q_ref[rows, cs], k_ref[pl.ds(r0, TK), cs])) for cs in heads]
            sums1 = _dot(jnp.concatenate([tile_sums(log_not, 1)[0] for (_, log_not) in lg1], axis=0), u)
            far = None
            for h, cs in enumerate(heads):
                blk = sums1[h * TQ:(h + 1) * TQ]
                carry = carry_ref[sub * hg + h]
                a = jnp.exp(lg1[h][0] + blk[:, :TK] + carry)
                acc_ref[sub * hg + h] += _dot(a.astype(BF16), v_ref[pl.ds(r0, TK), cs])
                carry = carry + blk[:, TK:]
                carry_ref[sub * hg + h] = carry
                far = carry if far is None else jnp.maximum(far, carry)
            return jnp.max(far)

        def cond(c):
            kb, far = c
            return jnp.logical_and(kb >= 0, far > SB_DEAD_LOG)

        def body(c):
            kb, _ = c
            return kb - 1, tile(kb)

        lax.while_loop(cond, body, (base - 1, worst))

    starts = [window(sub) for sub in range(SB_QT)]
    for sub, (base, worst) in enumerate(starts):
        far_sweep(sub, base, worst)
    for sub in range(SB_QT):
        for h, cs in enumerate(heads):
            o_ref[sub * TQ:(sub + 1) * TQ, cs] = acc_ref[sub * hg + h].astype(o_ref.dtype)


def _sb_attention(proj, w_next, layer, next_gains, hg=4):
    s = proj.shape[0]
    w = hg * HEAD_DIM
    nblk = SB_W // w
    tri = np.arange(TK)[:, None] > np.arange(TK)[None, :]
    half = np.concatenate([tri, np.ones((TK, TK), bool)], axis=1)
    u = jnp.asarray(np.concatenate([half, half], axis=0), dtype=BF16)
    tq = SB_QT * TQ
    nq = s // tq
    w_spec, g_spec, wo_spec, wo_shape, slab_bytes = _slab_specs(w_next, layer, nblk * nq, lambda g, i: g * nq + i)
    vmem = (2 * 2 * s * w * 2 + 2 * tq * w * 2 + 2 * tq * w * 4 + 2 * SB_QT * hg * TQ * TK * 4 + 2 * slab_bytes
            + (16 << 20))
    return pl.pallas_call(
        functools.partial(_sb_kernel, hg=hg),
        grid=(nblk, nq),
        in_specs=[pl.BlockSpec((tq, w), lambda g, i: (i, g)),
                  pl.BlockSpec((s, w), lambda g, i: (0, nblk + g)),
                  pl.BlockSpec((s, w), lambda g, i: (0, 2 * nblk + g)),
                  pl.BlockSpec((2 * TK, 2 * TK), lambda g, i: (0, 0)),
                  w_spec, g_spec],
        out_specs=[pl.BlockSpec((tq, w), lambda g, i: (i, g)), wo_spec],
        out_shape=[jax.ShapeDtypeStruct((s, SB_W), BF16), wo_shape],
        scratch_shapes=[pltpu.VMEM((SB_QT * hg, TQ, TK), F32), pltpu.VMEM((SB_QT * hg, TQ, HEAD_DIM), F32)],
        compiler_params=_cparams(("parallel", "arbitrary"), vmem),
        name="sb_attention",
    )(proj, proj, proj, u, w_next, _row_gains(next_gains))


def _compress_kernel(r_ref, pos_ref, w1_ref, w2_ref, o_ref):
    r = r_ref[0]
    half = r.shape[1]
    first = _dot(r, w1_ref[:half, :])
    second = _dot(r, w1_ref[half:, :])
    second = pltpu.roll(second, second.shape[0] - 1, 0)
    pos_term = _dot(pos_ref[...].astype(BF16), w1_ref[...])[0:1, :]
    hidden = jax.nn.gelu(first + second + pos_term)
    o_ref[0] = _dot(hidden.astype(BF16), w2_ref[...]).astype(o_ref.dtype)


def _compress(r, pos, w1, w2):
    g, n, half = r.shape
    pos_rows = jnp.zeros((SUBLANES, 2 * half), F32).at[0].set(pos.reshape(-1))
    return pl.pallas_call(
        _compress_kernel,
        grid=(g,),
        in_specs=[pl.BlockSpec((1, n, half), lambda i: (i, 0, 0)),
                  pl.BlockSpec((SUBLANES, 2 * half), lambda i: (0, 0)),
                  pl.BlockSpec((2 * half, CMP_HIDDEN), lambda i: (0, 0)),
                  pl.BlockSpec((CMP_HIDDEN, HEAD_DIM), lambda i: (0, 0))],
        out_specs=pl.BlockSpec((1, n, HEAD_DIM), lambda i: (i, 0, 0)),
        out_shape=jax.ShapeDtypeStruct((g, n, HEAD_DIM), BF16),
        compiler_params=_cparams(("parallel",), 24 << 20),
        name="nsa_compress",
    )(r, pos_rows, w1.astype(BF16), w2.astype(BF16))


CMP_QT = 4


def _nsa_cmp_kernel(slopes_ref, q_ref, kc_ref, vc_ref, msel_ref, oc_ref, sel_ref, flag_ref, qs_ref, *, n_top):
    n_cmp = kc_ref.shape[1]
    n_slc = msel_ref.shape[1]
    msel = msel_ref[...]
    rows = NSA_HPG * TQ
    t_rel = lax.broadcasted_iota(jnp.int32, (TQ, n_cmp), 0)
    blk = lax.broadcasted_iota(jnp.int32, (n_slc, TQ), 0)
    blk_f = blk.astype(F32)
    for sub in range(CMP_QT):
        i = pl.program_id(0) * CMP_QT + sub
        tile = slice(sub * TQ, (sub + 1) * TQ)
        end_rel = lax.broadcasted_iota(jnp.int32, (TQ, n_cmp), 1) * STRIDE_CMP + (L_CMP - 1) - i * TQ
        bias = jnp.where(end_rel <= t_rel, 0.0, NEG_INF)
        end_row = end_rel[0:1, :].astype(F32)
        t_col = i * TQ + lax.broadcasted_iota(jnp.int32, (TQ, 1), 0)
        row_valid = jnp.where(t_col >= L_CMP - 1, 1.0, 0.0)
        cur = lax.shift_right_logical(i * TQ + lax.broadcasted_iota(jnp.int32, (n_slc, TQ), 1), LOG2_L_SLC)
        valid = blk <= cur
        bonus = jnp.where(blk == 0, FORCE_BONUS, 0.0)
        bonus = jnp.where(blk == cur, FORCE_BONUS, bonus)
        bonus = jnp.where(blk == cur - 1, FORCE_BONUS, bonus)
        for g in range(NSA_KV):
            base = (sub * NSA_KV + g) * rows
            for h in range(NSA_HPG):
                c0 = (g * NSA_HPG + h) * HEAD_DIM
                qs_ref[base + h * TQ:base + (h + 1) * TQ, :] = q_ref[tile, c0:c0 + HEAD_DIM]
            z = _by_row_halves(_dot_nt, qs_ref[base:base + rows, :], kc_ref[g])
            p_sum = jnp.zeros((TQ, n_cmp), F32)
            ps = []
            for h in range(NSA_HPG):
                s = z[h * TQ:(h + 1) * TQ] + (bias + slopes_ref[g * NSA_HPG + h] * end_row)
                m = jnp.max(s, axis=-1, keepdims=True)
                e = jnp.exp(s - m)
                p = e * (row_valid / jnp.sum(e, axis=-1, keepdims=True))
                ps.append(p.astype(BF16))
                p_sum = p_sum + p
            oc = _by_row_halves(_dot, jnp.concatenate(ps, axis=0), vc_ref[g])
            for h in range(NSA_HPG):
                c0 = (g * NSA_HPG + h) * HEAD_DIM
                oc_ref[tile, c0:c0 + HEAD_DIM] = oc[h * TQ:(h + 1) * TQ]
            hi, lo = _split_bf16(p_sum)
            imp = _dot(hi, msel) + _dot(lo, msel)
            score = jnp.where(valid, imp.T + bonus, NEG_INF)
            taken = jnp.zeros((n_slc, TQ), F32)
            for _ in range(n_top):
                best = jnp.max(score, axis=0, keepdims=True)
                first = jnp.min(jnp.where(score == best, blk_f, float(n_slc)), axis=0, keepdims=True)
                hit = blk_f == first
                taken = jnp.where(hit, 1.0, taken)
                score = jnp.where(hit, -jnp.inf, score)
            chosen_q = jnp.where(valid, taken, 0.0).T
            sel_ref[g, sub] = chosen_q.astype(sel_ref.dtype)
            flag_ref[g, sub] = jnp.max(chosen_q, axis=0, keepdims=True)


def _select_matrix(n_cmp_pad, n_slc):
    rs = L_SLC // STRIDE_CMP
    rc = L_CMP // STRIDE_CMP
    m = np.zeros((n_cmp_pad, n_slc), np.float32)
    for j in range(n_slc):
        for a in range(rs):
            for b in range(rc):
                src = rs * j - a - b
                if 0 <= src < n_cmp_pad - 1:
                    m[src, j] += 1.0
    return jnp.asarray(m, dtype=BF16)


def _nsa_cmp(proj, k_cmp, v_cmp, slopes):
    s = proj.shape[0]
    n_cmp = k_cmp.shape[1]
    n_slc = s // L_SLC
    nq = s // TQ
    msel = _select_matrix(n_cmp, n_slc)
    qblk = (3 * SB_W) // NSA_W
    return pl.pallas_call(
        functools.partial(_nsa_cmp_kernel, n_top=min(N_TOPK, n_slc)),
        grid=(nq // CMP_QT,),
        in_specs=[pl.BlockSpec(memory_space=pltpu.SMEM),
                  pl.BlockSpec((CMP_QT * TQ, NSA_W), lambda i: (i, qblk)),
                  pl.BlockSpec((NSA_KV, n_cmp, HEAD_DIM), lambda i: (0, 0, 0)),
                  pl.BlockSpec((NSA_KV, n_cmp, HEAD_DIM), lambda i: (0, 0, 0)),
                  pl.BlockSpec((n_cmp, n_slc), lambda i: (0, 0))],
        out_specs=[pl.BlockSpec((CMP_QT * TQ, NSA_W), lambda i: (i, 0)),
                   pl.BlockSpec((NSA_KV, CMP_QT, TQ, n_slc), lambda i: (0, i, 0, 0)),
                   pl.BlockSpec((NSA_KV, CMP_QT, 1, n_slc), lambda i: (0, i, 0, 0))],
        out_shape=[jax.ShapeDtypeStruct((s, NSA_W), F32),
                   jax.ShapeDtypeStruct((NSA_KV, nq, TQ, n_slc), BF16),
                   jax.ShapeDtypeStruct((NSA_KV, nq, 1, n_slc), F32)],
        scratch_shapes=[pltpu.VMEM((CMP_QT * NSA_HEADS * TQ, HEAD_DIM), BF16)],
        compiler_params=_cparams(("parallel",), 48 << 20),
        name="nsa_cmp_select",
    )(slopes, proj, k_cmp, v_cmp, msel)


def _nsa_main_kernel(tiles_ref, count_ref, slopes_ref, q_ref, ks_ref, vs_ref, kw_ref, vw_ref, sel_ref, oc_ref,
                     gate_ref, o_ref, qa_ref, acc_s, m_s):
    g = pl.program_id(0)
    nq = pl.num_programs(1) * NSA_QT
    n_kt = nq * TQ // NSA_TK
    n_slc = sel_ref.shape[3]
    slopes = [slopes_ref[g * NSA_HPG + h] for h in range(NSA_HPG)]
    tiles_of_step = [pl.program_id(1) * NSA_QT + sub for sub in range(NSA_QT)]

    for sub in range(NSA_QT):
        not_picked = jnp.where(sel_ref[0, sub].astype(F32) > 0.5, 0.0, NEG_INF).astype(BF16)
        for h in range(NSA_HPG):
            qa_ref[sub, h * TQ:(h + 1) * TQ, :HEAD_DIM] = q_ref[sub * TQ:(sub + 1) * TQ, h * HEAD_DIM:(h + 1) * HEAD_DIM]
            qa_ref[sub, h * TQ:(h + 1) * TQ, HEAD_DIM:] = not_picked

    def scores(sub, k_ref, v_ref, pieces, bias, select):
        i = tiles_of_step[sub]
        zs = []
        for r0, n in pieces:
            k = k_ref[pl.ds(r0, n), :]
            if select:
                blk_of_key = lax.shift_right_logical(r0 + lax.broadcasted_iota(jnp.int32, (n, n_slc), 0), LOG2_L_SLC)
                one_hot = jnp.where(lax.broadcasted_iota(jnp.int32, (n, n_slc), 1) == blk_of_key, 1.0, 0.0)
                zs.append(_by_row_halves(_dot_nt, qa_ref[sub], jnp.concatenate([k, one_hot.astype(BF16)], axis=1)))
            else:
                zs.append(_by_row_halves(_dot_nt, qa_ref[sub, :, :HEAD_DIM], k))
        z = jnp.concatenate(zs, axis=1)
        v_aug = jnp.concatenate(
            [jnp.concatenate([v_ref[pl.ds(r0, n), :], jnp.ones((n, HEAD_DIM), BF16)], axis=1) for r0, n in pieces],
            axis=0)
        key_rel = jnp.concatenate(
            [(r0 - i * TQ + lax.broadcasted_iota(jnp.int32, (1, n), 1)).astype(F32) for r0, n in pieces], axis=1)
        if bias is None:
            return [z[h * TQ:(h + 1) * TQ] + slopes[h] * key_rel for h in range(NSA_HPG)], v_aug
        return [z[h * TQ:(h + 1) * TQ] + (bias + slopes[h] * key_rel) for h in range(NSA_HPG)], v_aug

    def attend(sub, pieces, bias, first):
        s_all, v_aug = scores(sub, ks_ref, vs_ref, pieces, bias, True)
        width = s_all[0].shape[1]
        ps, alphas = [], []
        for h, s in enumerate(s_all):
            rows = slice(h * TQ, (h + 1) * TQ)
            m_tile = jnp.max(s, axis=-1, keepdims=True)
            if first:
                m_new = jnp.broadcast_to(m_tile, (TQ, LANES))
            else:
                m_old = m_s[sub, rows]
                m_new = jnp.maximum(m_old, m_tile)
                alphas.append(jnp.exp(m_old - m_new))
            m_s[sub, rows] = m_new
            ps.append(jnp.exp(s - jnp.concatenate([m_new] * (width // LANES), axis=1)).astype(BF16))
        pv = _by_row_halves(_dot, jnp.concatenate(ps, axis=0), v_aug)
        for h in range(NSA_HPG):
            rows = slice(h * TQ, (h + 1) * TQ)
            if first:
                acc_s[sub, rows] = pv[rows]
            else:
                acc_s[sub, rows] = acc_s[sub, rows] * jnp.concatenate([alphas[h], alphas[h]], axis=1) + pv[rows]

    def tile_rows(kt):
        return pl.multiple_of(kt * NSA_TK, NSA_TK), NSA_TK

    for sub, i in enumerate(tiles_of_step):
        near = jnp.maximum(lax.div(i * TQ, NSA_TK) - 1, 0)
        near0 = pl.multiple_of(near * NSA_TK, NSA_TK)
        nw = 2 * NSA_TK
        ahead = ((near0 + lax.broadcasted_iota(jnp.int32, (TQ, nw), 1))
                 - (i * TQ + lax.broadcasted_iota(jnp.int32, (TQ, nw), 0)))
        attend(sub, [(near0, nw)], jnp.where(ahead <= 0, 0.0, NEG_INF), True)

    for sub, i in enumerate(tiles_of_step):
        step = g * nq + i
        n_far = count_ref[step]

        def pair_body(p, _, sub=sub, step=step):
            ta = tiles_ref[step * n_kt + 2 * p]
            tb = tiles_ref[step * n_kt + 2 * p + 1]
            attend(sub, [tile_rows(ta), tile_rows(tb)], None, False)
            return 0

        lax.fori_loop(0, lax.shift_right_logical(n_far, 1), pair_body, 0)

        @pl.when(lax.rem(n_far, 2) == 1)
        def _(sub=sub, step=step, n_far=n_far):
            attend(sub, [tile_rows(tiles_ref[step * n_kt + n_far - 1])], None, False)

    wn = WINDOW + TQ
    pv_win = []
    for sub, i in enumerate(tiles_of_step):
        w0 = pl.multiple_of(jnp.maximum(i * TQ - WINDOW, 0), TQ)
        dist = (i * TQ + lax.broadcasted_iota(jnp.int32, (TQ, wn), 0)
                - (w0 + lax.broadcasted_iota(jnp.int32, (TQ, wn), 1)))
        w_bias = jnp.where(dist >= 0, jnp.where(dist < WINDOW, 0.0, NEG_INF), NEG_INF)
        s_win, v_win = scores(sub, kw_ref, vw_ref, [(w0, wn)], w_bias, False)
        p_win = [jnp.exp(s - jnp.max(s, axis=-1, keepdims=True)).astype(BF16) for s in s_win]
        pv_win.append(_by_row_halves(_dot, jnp.concatenate(p_win, axis=0), v_win))

    gates = gate_ref[...]
    for sub in range(NSA_QT):
        tile = slice(sub * TQ, (sub + 1) * TQ)
        for h in range(NSA_HPG):
            cs = slice(h * HEAD_DIM, (h + 1) * HEAD_DIM)
            a_s = acc_s[sub, h * TQ:(h + 1) * TQ]
            a_w = pv_win[sub][h * TQ:(h + 1) * TQ]
            o_sel = a_s[:, :HEAD_DIM] / a_s[:, HEAD_DIM:HEAD_DIM + 1]
            o_win = a_w[:, :HEAD_DIM] / a_w[:, HEAD_DIM:HEAD_DIM + 1]
            o_ref[tile, cs] = (gates[tile, 3 * h:3 * h + 1] * oc_ref[tile, cs]
                               + gates[tile, 3 * h + 1:3 * h + 2] * o_sel
                               + gates[tile, 3 * h + 2:3 * h + 3] * o_win).astype(o_ref.dtype)


def _far_tile_lists(tile_any, s):
    nq = s // TQ
    n_kt = s // NSA_TK
    hit = tile_any.reshape(NSA_KV, nq, n_kt, NSA_TK // L_SLC).max(axis=-1) > 0.5
    kt = jnp.arange(n_kt, dtype=jnp.int32)
    diag = (jnp.arange(nq, dtype=jnp.int32) * TQ) // NSA_TK
    far = jnp.logical_and(hit, kt[None, None, :] < diag[None, :, None] - 1)
    tiles = jnp.sort(jnp.where(far, kt[None, None, :], n_kt), axis=-1)
    tiles = jnp.minimum(tiles, n_kt - 1)
    return tiles.reshape(-1), jnp.sum(far, axis=-1, dtype=jnp.int32).reshape(-1)


def _nsa_main(proj, sel, tile_any, o_c, gates, slopes):
    s = proj.shape[0]
    nq = s // TQ
    tq = NSA_QT * TQ
    n_slc = s // L_SLC
    qblk = (3 * SB_W) // GROUP_W
    tiles, counts = _far_tile_lists(tile_any, s)
    kv_spec = lambda cb: pl.BlockSpec((s, HEAD_DIM), lambda g, i, t, c: (0, cb + g))
    rows = NSA_HPG * TQ
    grid_spec = pltpu.PrefetchScalarGridSpec(
        num_scalar_prefetch=2,
        grid=(NSA_KV, nq // NSA_QT),
        in_specs=[pl.BlockSpec(memory_space=pltpu.SMEM),
                  pl.BlockSpec((tq, GROUP_W), lambda g, i, t, c: (i, qblk + g)),
                  kv_spec(CB_KS), kv_spec(CB_VS), kv_spec(CB_KW), kv_spec(CB_VW),
                  pl.BlockSpec((1, NSA_QT, TQ, n_slc), lambda g, i, t, c: (g, i, 0, 0)),
                  pl.BlockSpec((tq, GROUP_W), lambda g, i, t, c: (i, g)),
                  pl.BlockSpec((tq, LANES), lambda g, i, t, c: (i, g))],
        out_specs=pl.BlockSpec((tq, GROUP_W), lambda g, i, t, c: (i, g)),
        scratch_shapes=[pltpu.VMEM((NSA_QT, rows, HEAD_DIM + n_slc), BF16),
                        pltpu.VMEM((NSA_QT, rows, 2 * HEAD_DIM), F32), pltpu.VMEM((NSA_QT, rows, LANES), F32)],
    )
    vmem = 2 * 4 * s * HEAD_DIM * 2 + (28 << 20)
    return pl.pallas_call(
        _nsa_main_kernel,
        grid_spec=grid_spec,
        out_shape=jax.ShapeDtypeStruct((s, NSA_W), BF16),
        compiler_params=_cparams(("parallel", "arbitrary"), vmem),
        name="nsa_select_window",
    )(tiles, counts, slopes, proj, proj, proj, proj, proj, sel, o_c, gates)


def kernel(x, attn_norm, w_in, pos_cmp_k, pos_cmp_v, w_cmp_k1, w_cmp_k2, w_cmp_v1, w_cmp_v2, norm_sb, norm_nsa,
           w_out, ffn_norm, w_gate, w_up, w_down, final_norm):
    batch, s, d_model = x.shape
    assert batch == 1 and s % 1024 == 0 and w_in.shape[2] == MAIN_COLS + GATE_COLS
    depth = w_in.shape[0]
    d_ff = w_gate.shape[2]
    assert d_ff % (2 * LANES) == 0 and (d_ff // 2) % LANES == 0
    scale = HEAD_DIM ** -0.5
    head_idx = jnp.arange(1, NSA_HEADS + 1, dtype=F32)
    slopes = 2.0 ** (-8.0 * head_idx / NSA_HEADS)
    col_scale = jnp.concatenate([jnp.full((SB_W,), scale, F32), jnp.ones((2 * SB_W,), F32),
                                 jnp.full((NSA_W,), scale, F32), jnp.ones((MAIN_COLS - 3 * SB_W - NSA_W,), F32)])
    col_scale = col_scale.reshape(1, MAIN_COLS)
    xs = x[0]
    w_in_t = jnp.swapaxes(w_in, 1, 2)
    for l in range(depth):
        w_g = w_in[l][:, MAIN_COLS:].reshape(d_model, NSA_KV, 3 * NSA_HPG)
        w_g = jnp.pad(w_g, ((0, 0), (0, 0), (0, LANES - 3 * NSA_HPG))).reshape(d_model, NSA_KV * LANES)

        h = _rmsnorm(xs, attn_norm[l], BF16)
        proj, w_gate_bf16, w_up_bf16 = _in_proj(h, w_in_t, l, MAIN_COLS, col_scale, w_gate, w_up, ffn_norm[l])
        gates = _gate_proj(h, w_g)

        o_sb, w_out_bf16 = _sb_attention(proj, w_out, l, jnp.concatenate([norm_sb[l], norm_nsa[l]]))

        def blocks_view(cb):
            cols = proj[:, cb * HEAD_DIM:(cb + NSA_KV) * HEAD_DIM]
            r = cols.reshape(s // STRIDE_CMP, STRIDE_CMP, NSA_KV, HEAD_DIM).transpose(2, 0, 1, 3)
            return r.reshape(NSA_KV, s // STRIDE_CMP, STRIDE_CMP * HEAD_DIM)

        k_cmp = _compress(blocks_view(CB_KC), pos_cmp_k[l], w_cmp_k1[l], w_cmp_k2[l])
        v_cmp = _compress(blocks_view(CB_VC), pos_cmp_v[l], w_cmp_v1[l], w_cmp_v2[l])
        o_c, sel, tile_any = _nsa_cmp(proj, k_cmp, v_cmp, slopes)
        o_nsa = _nsa_main(proj, sel, tile_any, o_c, gates, slopes)

        xs, xs_bf16 = _out_proj(o_sb, o_nsa, w_out_bf16, xs)
        act, w_down_bf16 = _ffn_up(xs_bf16, w_gate_bf16, w_up_bf16, w_down, l)
        xs = _ffn_down(act, w_down_bf16, xs)
    return _rmsnorm(xs, final_norm, F32)[None]
```

```python
import functools

import numpy as np
import jax
import jax.numpy as jnp
from jax import lax
from jax.experimental import pallas as pl
from jax.experimental.pallas import tpu as pltpu

HEAD_DIM = 128
SB_HEADS = 16
NSA_HEADS = 16
NSA_KV = 2
NSA_HPG = NSA_HEADS // NSA_KV
L_CMP = 32
STRIDE_CMP = 16
CMP_HIDDEN = 256
L_SLC = 64
LOG2_L_SLC = 6
N_TOPK = 16
WINDOW = 512
EPS = 1e-6
NEG_INF = -1e30
FORCE_BONUS = 1e6

LANES = 128
SUBLANES = 8
TQ = 128
TK = 128
SB_WIN = 3
SB_QT = 8
NSA_TK = 256
NSA_QT = 2
VMEM_CAP = 56 * 1024 * 1024
SB_DEAD_LOG = -105.0

F32 = jnp.float32
BF16 = jnp.bfloat16

SB_W = SB_HEADS * HEAD_DIM
NSA_W = NSA_HEADS * HEAD_DIM
GROUP_W = NSA_HPG * HEAD_DIM
MAIN_COLS = 3 * SB_W + NSA_W + 3 * 2 * NSA_KV * HEAD_DIM
GATE_COLS = 3 * NSA_HEADS
CB_KC, CB_VC, CB_KS, CB_VS, CB_KW, CB_VW = 64, 66, 68, 70, 72, 74


def _cparams(sem, vmem_bytes):
    return pltpu.CompilerParams(dimension_semantics=sem, vmem_limit_bytes=int(min(vmem_bytes, VMEM_CAP)))


def _dot(a, b):
    return jnp.dot(a, b, preferred_element_type=F32)


def _dot_nt(a, b):
    return lax.dot_general(a, b, (((1,), (1,)), ((), ())), preferred_element_type=F32)


def _by_row_halves(dot_fn, a, b):
    half = a.shape[0] // 2
    return jnp.concatenate([dot_fn(a[:half], b), dot_fn(a[half:], b)], axis=0)


def _split_bf16(x):
    hi = x.astype(BF16)
    lo = (x - hi.astype(F32)).astype(BF16)
    return hi, lo


def _rms_kernel(x_ref, g_ref, o_ref):
    x = x_ref[...]
    ms = jnp.mean(x * x, axis=-1, keepdims=True)
    o_ref[...] = (x * lax.rsqrt(ms + EPS) * g_ref[...]).astype(o_ref.dtype)


def _rmsnorm(x, g, out_dtype, tm=256):
    m, d = x.shape
    return pl.pallas_call(
        _rms_kernel,
        grid=(m // tm,),
        in_specs=[pl.BlockSpec((tm, d), lambda i: (i, 0)), pl.BlockSpec((1, d), lambda i: (0, 0))],
        out_specs=pl.BlockSpec((tm, d), lambda i: (i, 0)),
        out_shape=jax.ShapeDtypeStruct((m, d), out_dtype),
        compiler_params=_cparams(("parallel",), 6 * tm * d * 4),
        name="rmsnorm",
    )(x, g.reshape(1, d))


def _rms_gate_kernel(x_ref, g_ref, wg_ref, o_ref, gate_ref):
    x = x_ref[...]
    ms = jnp.mean(x * x, axis=-1, keepdims=True)
    h = (x * lax.rsqrt(ms + EPS) * g_ref[...]).astype(o_ref.dtype)
    o_ref[...] = h
    gate_ref[...] = jax.nn.sigmoid(_dot(h, wg_ref[...].astype(BF16)))


def _rmsnorm_gates(x, g, w_gates, tm=256):
    m, d = x.shape
    n = w_gates.shape[1]
    return pl.pallas_call(
        _rms_gate_kernel,
        grid=(m // tm,),
        in_specs=[pl.BlockSpec((tm, d), lambda i: (i, 0)), pl.BlockSpec((1, d), lambda i: (0, 0)),
                  pl.BlockSpec((d, n), lambda i: (0, 0))],
        out_specs=[pl.BlockSpec((tm, d), lambda i: (i, 0)), pl.BlockSpec((tm, n), lambda i: (i, 0))],
        out_shape=[jax.ShapeDtypeStruct((m, d), BF16), jax.ShapeDtypeStruct((m, n), F32)],
        compiler_params=_cparams(("parallel",), 6 * tm * d * 4 + 3 * d * n * 4),
        name="rmsnorm_gates",
    )(x, g.reshape(1, d), w_gates)


def _row_scale(a_ref):
    a = a_ref[...].astype(F32)
    ms = jnp.mean(a * a, axis=-1, keepdims=True)
    return jnp.broadcast_to(lax.rsqrt(ms + EPS), (a.shape[0], LANES))


def _lanes(x, n):
    return jnp.concatenate([x] * (n // LANES), axis=1)


def _gained_slab(w_ref, g_ref, o_ref):
    o_ref[...] = (w_ref[...] * _lanes(g_ref[...], w_ref.shape[1])).astype(o_ref.dtype)


def _slab_specs(w, layer, n_steps, step_of):
    rows, cols = w.shape[1], w.shape[2]
    slab = rows // n_steps
    assert slab * n_steps == rows and slab % (2 * SUBLANES) == 0
    return (pl.BlockSpec((None, slab, cols), lambda *idx: (layer, step_of(*idx), 0)),
            pl.BlockSpec((slab, LANES), lambda *idx: (step_of(*idx), 0)),
            pl.BlockSpec((slab, cols), lambda *idx: (step_of(*idx), 0)),
            jax.ShapeDtypeStruct((rows, cols), BF16), slab * cols * 6 + slab * LANES * 4)


def _row_gains(g):
    return jnp.broadcast_to(g.reshape(-1, 1), (g.shape[0], LANES))


def _out_proj_kernel(a1_ref, a2_ref, w_ref, res_ref, o_ref, ob_ref, r_ref):
    @pl.when(pl.program_id(1) == 0)
    def _():
        r_ref[0] = _row_scale(a1_ref)
        r_ref[1] = _row_scale(a2_ref)

    k1 = a1_ref.shape[1]
    tn = o_ref.shape[1]
    y1 = _dot(a1_ref[...], w_ref[:k1, :])
    y2 = _dot(a2_ref[...], w_ref[k1:, :])
    out = res_ref[...] + (_lanes(r_ref[0], tn) * y1 + _lanes(r_ref[1], tn) * y2)
    o_ref[...] = out
    ob_ref[...] = out.astype(ob_ref.dtype)


def _out_proj(a1, a2, w, res, tm=1024, tn=512):
    m, k1 = a1.shape
    k2 = a2.shape[1]
    n = w.shape[1]
    vmem = 2 * tm * (k1 + k2) * 2 + 2 * (k1 + k2) * tn * 2 + 2 * tm * tn * (4 + 4 + 2) + 3 * tm * tn * 4 + (4 << 20)
    return pl.pallas_call(
        _out_proj_kernel,
        grid=(m // tm, n // tn),
        in_specs=[pl.BlockSpec((tm, k1), lambda i, j: (i, 0)), pl.BlockSpec((tm, k2), lambda i, j: (i, 0)),
                  pl.BlockSpec((k1 + k2, tn), lambda i, j: (0, j)),
                  pl.BlockSpec((tm, tn), lambda i, j: (i, j))],
        out_specs=[pl.BlockSpec((tm, tn), lambda i, j: (i, j)), pl.BlockSpec((tm, tn), lambda i, j: (i, j))],
        out_shape=[jax.ShapeDtypeStruct((m, n), F32), jax.ShapeDtypeStruct((m, n), BF16)],
        scratch_shapes=[pltpu.VMEM((2, tm, LANES), F32)],
        compiler_params=_cparams(("parallel", "arbitrary"), vmem),
        name="out_proj",
    )(a1, a2, w, res)


def _ffn_up_kernel(a_ref, wg_ref, wu_ref, wd_ref, o_ref, wd_bf16_ref, r_ref):
    @pl.when(pl.program_id(1) == 0)
    def _():
        r_ref[...] = _row_scale(a_ref)

    a = a_ref[...]
    r = _lanes(r_ref[...], o_ref.shape[1])
    gate = r * _dot(a, wg_ref[...])
    up = r * _dot(a, wu_ref[...])
    o_ref[...] = (jax.nn.silu(gate) * up).astype(o_ref.dtype)
    wd_bf16_ref[...] = wd_ref[...].astype(wd_bf16_ref.dtype)


def _ffn_up(a, w_gate, w_up, w_down, layer, tm=2048, tn=2 * LANES):
    m, k = a.shape
    n = w_gate.shape[1]
    d_out = w_down.shape[2]
    nj = n // tn
    slab = n // ((m // tm) * nj)
    assert slab * (m // tm) * nj == n and slab % (2 * SUBLANES) == 0
    vmem = 2 * tm * k * 2 + 2 * 2 * k * tn * 2 + 2 * tm * tn * 2 + 5 * tm * tn * 4 + 2 * slab * d_out * 6 + (4 << 20)
    w_spec = pl.BlockSpec((k, tn), lambda i, j: (0, j))
    return pl.pallas_call(
        _ffn_up_kernel,
        grid=(m // tm, nj),
        in_specs=[pl.BlockSpec((tm, k), lambda i, j: (i, 0)), w_spec, w_spec,
                  pl.BlockSpec((None, slab, d_out), lambda i, j: (layer, i * nj + j, 0))],
        out_specs=[pl.BlockSpec((tm, tn), lambda i, j: (i, j)),
                   pl.BlockSpec((slab, d_out), lambda i, j: (i * nj + j, 0))],
        out_shape=[jax.ShapeDtypeStruct((m, n), BF16), jax.ShapeDtypeStruct((n, d_out), BF16)],
        scratch_shapes=[pltpu.VMEM((tm, LANES), F32)],
        compiler_params=_cparams(("parallel", "arbitrary"), vmem),
        name="ffn_gate_up",
    )(a, w_gate, w_up, w_down)


IN_PROJ_SLABS = 16


def _in_proj_kernel(a_ref, bt_ref, cs_ref, wa_ref, wb_ref, wgain_ref, o_ref, wa_bf16_ref, wb_bf16_ref):
    o_ref[...] = (_dot_nt(a_ref[...], bt_ref[...].astype(BF16)) * cs_ref[...]).astype(o_ref.dtype)

    @pl.when(pl.program_id(1) < IN_PROJ_SLABS)
    def _():
        _gained_slab(wa_ref, wgain_ref, wa_bf16_ref)
        _gained_slab(wb_ref, wgain_ref, wb_bf16_ref)


def _in_proj(a, w_t, layer, n, col_scale, w_a, w_b, next_gains, tm=1024, tn=512):
    m, k = a.shape
    ni, nj = m // tm, n // tn
    assert nj >= IN_PROJ_SLABS
    w_spec, g_spec, wo_spec, wo_shape, slab_bytes = _slab_specs(
        w_a, layer, ni * IN_PROJ_SLABS, lambda i, j: i * IN_PROJ_SLABS + jnp.minimum(j, IN_PROJ_SLABS - 1))
    vmem = 2 * tm * k * 2 + tn * k * (2 * 4 + 2) + 2 * tm * tn * 2 + 2 * tm * tn * 4 + 4 * slab_bytes + (4 << 20)
    return pl.pallas_call(
        _in_proj_kernel,
        grid=(ni, nj),
        in_specs=[pl.BlockSpec((tm, k), lambda i, j: (i, 0)),
                  pl.BlockSpec((None, tn, k), lambda i, j: (layer, j, 0)),
                  pl.BlockSpec((1, tn), lambda i, j: (0, j)),
                  w_spec, w_spec, g_spec],
        out_specs=[pl.BlockSpec((tm, tn), lambda i, j: (i, j)), wo_spec, wo_spec],
        out_shape=[jax.ShapeDtypeStruct((m, n), BF16), wo_shape, wo_shape],
        compiler_params=_cparams(("parallel", "arbitrary"), vmem),
        name="in_proj",
    )(a, w_t, col_scale, w_a, w_b, _row_gains(next_gains))


def _ffn_down_kernel(a_ref, b_ref, res_ref, o_ref, acc_ref):
    kk = pl.program_id(2)

    @pl.when(kk == 0)
    def _():
        acc_ref[...] = res_ref[...]

    acc_ref[...] += _dot(a_ref[...], b_ref[...])

    @pl.when(kk == pl.num_programs(2) - 1)
    def _():
        o_ref[...] = acc_ref[...]


def _ffn_down(a, b, res, tm=1024, tn=512):
    m, k = a.shape
    n = b.shape[1]
    tk = k // 2
    assert tk % LANES == 0
    vmem = 2 * tm * tk * 2 + 2 * tk * tn * 2 + 6 * tm * tn * 4 + (4 << 20)
    return pl.pallas_call(
        _ffn_down_kernel,
        grid=(m // tm, n // tn, k // tk),
        in_specs=[pl.BlockSpec((tm, tk), lambda i, j, kk: (i, kk)),
                  pl.BlockSpec((tk, tn), lambda i, j, kk: (kk, j)),
                  pl.BlockSpec((tm, tn), lambda i, j, kk: (i, j))],
        out_specs=pl.BlockSpec((tm, tn), lambda i, j, kk: (i, j)),
        out_shape=jax.ShapeDtypeStruct((m, n), F32),
        scratch_shapes=[pltpu.VMEM((tm, tn), F32)],
        compiler_params=_cparams(("parallel", "parallel", "arbitrary"), vmem),
        name="ffn_down",
    )(a, b, res)


def _sb_kernel(q_ref, k_ref, v_ref, u_ref, w_ref, wgain_ref, o_ref, w_bf16_ref, carry_ref, acc_ref, *, hg):
    _gained_slab(w_ref, wgain_ref, w_bf16_ref)
    step = pl.program_id(1)
    u = u_ref[...]
    heads = [slice(h * HEAD_DIM, (h + 1) * HEAD_DIM) for h in range(hg)]

    def logs(z):
        log1p_e = jnp.log(1.0 + jnp.exp(-jnp.abs(z)))
        log_sig = jnp.minimum(z, 0.0) - log1p_e
        return log_sig, log_sig - z

    def tile_sums(log_not, n):
        hi, lo = _split_bf16(log_not)
        return [jnp.concatenate([hi[:, t * TK:(t + 1) * TK], lo[:, t * TK:(t + 1) * TK]], axis=1) for t in range(n)]

    def window(sub):
        qt = step * SB_QT + sub
        rows = slice(sub * TQ, (sub + 1) * TQ)
        base = jnp.maximum(qt - (SB_WIN - 1), 0)
        row0 = pl.multiple_of(base * TK, TK)
        wk = SB_WIN * TK
        key_pos = base * TK + lax.broadcasted_iota(jnp.int32, (TQ, wk), 1)
        visible = key_pos < qt * TQ + lax.broadcasted_iota(jnp.int32, (TQ, wk), 0)
        zs = [jnp.where(visible, _dot_nt(q_ref[rows, cs], k_ref[pl.ds(row0, wk), cs]), NEG_INF) for cs in heads]
        lg = [logs(z) for z in zs]
        lhs = [part for (_, log_not) in lg for part in tile_sums(log_not, SB_WIN)]
        sums = _by_row_halves(_dot, jnp.concatenate(lhs, axis=0), u)
        worst = None
        for h, cs in enumerate(heads):
            carry = None
            shifted = [None] * SB_WIN
            for t in reversed(range(SB_WIN)):
                blk = sums[(h * SB_WIN + t) * TQ:(h * SB_WIN + t + 1) * TQ]
                tail, total = blk[:, :TK], blk[:, TK:]
                shifted[t] = tail if carry is None else tail + carry
                carry = total if carry is None else carry + total
            a = jnp.exp(lg[h][0] + jnp.concatenate(shifted, axis=1))
            acc_ref[sub * hg + h] = _dot(a.astype(BF16), v_ref[pl.ds(row0, wk), cs])
            carry_ref[sub * hg + h] = carry
            worst = carry if worst is None else jnp.maximum(worst, carry)
        return base, jnp.max(worst)

    def far_sweep(sub, base, worst):
        rows = slice(sub * TQ, (sub + 1) * TQ)

        def tile(kb):
            r0 = pl.multiple_of(kb * TK, TK)
            lg1 = [logs(_dot_nt(q_ref[rows, cs], k_ref[pl.ds(r0, TK), cs])) for cs in heads]
            sums1 = _dot(jnp.concatenate([tile_sums(log_not, 1)[0] for (_, log_not) in lg1], axis=0), u)
            far = None
            for h, cs in enumerate(heads):
                blk = sums1[h * TQ:(h + 1) * TQ]
                carry = carry_ref[sub * hg + h]
                a = jnp.exp(lg1[h][0] + blk[:, :TK] + carry)
                acc_ref[sub * hg + h] += _dot(a.astype(BF16), v_ref[pl.ds(r0, TK), cs])
                carry = carry + blk[:, TK:]
                carry_ref[sub * hg + h] = carry
                far = carry if far is None else jnp.maximum(far, carry)
            return jnp.max(far)

        def cond(c):
            kb, far = c
            return jnp.logical_and(kb >= 0, far > SB_DEAD_LOG)

        def body(c):
            kb, _ = c
            return kb - 1, tile(kb)

        lax.while_loop(cond, body, (base - 1, worst))

    starts = [window(sub) for sub in range(SB_QT)]
    for sub, (base, worst) in enumerate(starts):
        far_sweep(sub, base, worst)
    for sub in range(SB_QT):
        for h, cs in enumerate(heads):
            o_ref[sub * TQ:(sub + 1) * TQ, cs] = acc_ref[sub * hg + h].astype(o_ref.dtype)


def _sb_attention(proj, w_next, layer, next_gains, hg=4):
    s = proj.shape[0]
    w = hg * HEAD_DIM
    nblk = SB_W // w
    tri = np.arange(TK)[:, None] > np.arange(TK)[None, :]
    half = np.concatenate([tri, np.ones((TK, TK), bool)], axis=1)
    u = jnp.asarray(np.concatenate([half, half], axis=0), dtype=BF16)
    tq = SB_QT * TQ
    nq = s // tq
    w_spec, g_spec, wo_spec, wo_shape, slab_bytes = _slab_specs(w_next, layer, nblk * nq, lambda g, i: g * nq + i)
    vmem = (2 * 2 * s * w * 2 + 2 * tq * w * 2 + 2 * tq * w * 4 + 2 * SB_QT * hg * TQ * TK * 4 + 2 * slab_bytes
            + (16 << 20))
    return pl.pallas_call(
        functools.partial(_sb_kernel, hg=hg),
        grid=(nblk, nq),
        in_specs=[pl.BlockSpec((tq, w), lambda g, i: (i, g)),
                  pl.BlockSpec((s, w), lambda g, i: (0, nblk + g)),
                  pl.BlockSpec((s, w), lambda g, i: (0, 2 * nblk + g)),
                  pl.BlockSpec((2 * TK, 2 * TK), lambda g, i: (0, 0)),
                  w_spec, g_spec],
        out_specs=[pl.BlockSpec((tq, w), lambda g, i: (i, g)), wo_spec],
        out_shape=[jax.ShapeDtypeStruct((s, SB_W), BF16), wo_shape],
        scratch_shapes=[pltpu.VMEM((SB_QT * hg, TQ, TK), F32), pltpu.VMEM((SB_QT * hg, TQ, HEAD_DIM), F32)],
        compiler_params=_cparams(("parallel", "arbitrary"), vmem),
        name="sb_attention",
    )(proj, proj, proj, u, w_next, _row_gains(next_gains))


def _compress_kernel(r_ref, pos_ref, w1_ref, w2_ref, o_ref):
    r = r_ref[0]
    half = r.shape[1]
    first = _dot(r, w1_ref[:half, :])
    second = _dot(r, w1_ref[half:, :])
    second = pltpu.roll(second, second.shape[0] - 1, 0)
    pos_term = _dot(pos_ref[...].astype(BF16), w1_ref[...])[0:1, :]
    hidden = jax.nn.gelu(first + second + pos_term)
    o_ref[0] = _dot(hidden.astype(BF16), w2_ref[...]).astype(o_ref.dtype)


def _compress(r, pos, w1, w2):
    g, n, half = r.shape
    pos_rows = jnp.zeros((SUBLANES, 2 * half), F32).at[0].set(pos.reshape(-1))
    return pl.pallas_call(
        _compress_kernel,
        grid=(g,),
        in_specs=[pl.BlockSpec((1, n, half), lambda i: (i, 0, 0)),
                  pl.BlockSpec((SUBLANES, 2 * half), lambda i: (0, 0)),
                  pl.BlockSpec((2 * half, CMP_HIDDEN), lambda i: (0, 0)),
                  pl.BlockSpec((CMP_HIDDEN, HEAD_DIM), lambda i: (0, 0))],
        out_specs=pl.BlockSpec((1, n, HEAD_DIM), lambda i: (i, 0, 0)),
        out_shape=jax.ShapeDtypeStruct((g, n, HEAD_DIM), BF16),
        compiler_params=_cparams(("parallel",), 24 << 20),
        name="nsa_compress",
    )(r, pos_rows, w1.astype(BF16), w2.astype(BF16))


CMP_QT = 4


def _nsa_cmp_kernel(slopes_ref, q_ref, kc_ref, vc_ref, msel_ref, oc_ref, sel_ref, flag_ref, qs_ref, *, n_top):
    n_cmp = kc_ref.shape[1]
    n_slc = msel_ref.shape[1]
    msel = msel_ref[...]
    rows = NSA_HPG * TQ
    t_rel = lax.broadcasted_iota(jnp.int32, (TQ, n_cmp), 0)
    blk = lax.broadcasted_iota(jnp.int32, (n_slc, TQ), 0)
    blk_f = blk.astype(F32)
    for sub in range(CMP_QT):
        i = pl.program_id(0) * CMP_QT + sub
        tile = slice(sub * TQ, (sub + 1) * TQ)
        end_rel = lax.broadcasted_iota(jnp.int32, (TQ, n_cmp), 1) * STRIDE_CMP + (L_CMP - 1) - i * TQ
        bias = jnp.where(end_rel <= t_rel, 0.0, NEG_INF)
        end_row = end_rel[0:1, :].astype(F32)
        t_col = i * TQ + lax.broadcasted_iota(jnp.int32, (TQ, 1), 0)
        row_valid = jnp.where(t_col >= L_CMP - 1, 1.0, 0.0)
        cur = lax.shift_right_logical(i * TQ + lax.broadcasted_iota(jnp.int32, (n_slc, TQ), 1), LOG2_L_SLC)
        valid = blk <= cur
        bonus = jnp.where(blk == 0, FORCE_BONUS, 0.0)
        bonus = jnp.where(blk == cur, FORCE_BONUS, bonus)
        bonus = jnp.where(blk == cur - 1, FORCE_BONUS, bonus)
        for g in range(NSA_KV):
            base = (sub * NSA_KV + g) * rows
            for h in range(NSA_HPG):
                c0 = (g * NSA_HPG + h) * HEAD_DIM
                qs_ref[base + h * TQ:base + (h + 1) * TQ, :] = q_ref[tile, c0:c0 + HEAD_DIM]
            z = _by_row_halves(_dot_nt, qs_ref[base:base + rows, :], kc_ref[g])
            p_sum = jnp.zeros((TQ, n_cmp), F32)
            ps = []
            for h in range(NSA_HPG):
                s = z[h * TQ:(h + 1) * TQ] + (bias + slopes_ref[g * NSA_HPG + h] * end_row)
                m = jnp.max(s, axis=-1, keepdims=True)
                e = jnp.exp(s - m)
                p = e * (row_valid / jnp.sum(e, axis=-1, keepdims=True))
                ps.append(p.astype(BF16))
                p_sum = p_sum + p
            oc = _by_row_halves(_dot, jnp.concatenate(ps, axis=0), vc_ref[g])
            for h in range(NSA_HPG):
                c0 = (g * NSA_HPG + h) * HEAD_DIM
                oc_ref[tile, c0:c0 + HEAD_DIM] = oc[h * TQ:(h + 1) * TQ]
            hi, lo = _split_bf16(p_sum)
            imp = _dot(hi, msel) + _dot(lo, msel)
            score = jnp.where(valid, imp.T + bonus, NEG_INF)
            taken = jnp.zeros((n_slc, TQ), F32)
            for _ in range(n_top):
                best = jnp.max(score, axis=0, keepdims=True)
                first = jnp.min(jnp.where(score == best, blk_f, float(n_slc)), axis=0, keepdims=True)
                hit = blk_f == first
                taken = jnp.where(hit, 1.0, taken)
                score = jnp.where(hit, -jnp.inf, score)
            chosen_q = jnp.where(valid, taken, 0.0).T
            sel_ref[g, sub] = chosen_q.astype(sel_ref.dtype)
            flag_ref[g, sub] = jnp.max(chosen_q, axis=0, keepdims=True)


def _select_matrix(n_cmp_pad, n_slc):
    rs = L_SLC // STRIDE_CMP
    rc = L_CMP // STRIDE_CMP
    m = np.zeros((n_cmp_pad, n_slc), np.float32)
    for j in range(n_slc):
        for a in range(rs):
            for b in range(rc):
                src = rs * j - a - b
                if 0 <= src < n_cmp_pad - 1:
                    m[src, j] += 1.0
    return jnp.asarray(m, dtype=BF16)


def _nsa_cmp(proj, k_cmp, v_cmp, slopes):
    s = proj.shape[0]
    n_cmp = k_cmp.shape[1]
    n_slc = s // L_SLC
    nq = s // TQ
    msel = _select_matrix(n_cmp, n_slc)
    qblk = (3 * SB_W) // NSA_W
    return pl.pallas_call(
        functools.partial(_nsa_cmp_kernel, n_top=min(N_TOPK, n_slc)),
        grid=(nq // CMP_QT,),
        in_specs=[pl.BlockSpec(memory_space=pltpu.SMEM),
                  pl.BlockSpec((CMP_QT * TQ, NSA_W), lambda i: (i, qblk)),
                  pl.BlockSpec((NSA_KV, n_cmp, HEAD_DIM), lambda i: (0, 0, 0)),
                  pl.BlockSpec((NSA_KV, n_cmp, HEAD_DIM), lambda i: (0, 0, 0)),
                  pl.BlockSpec((n_cmp, n_slc), lambda i: (0, 0))],
        out_specs=[pl.BlockSpec((CMP_QT * TQ, NSA_W), lambda i: (i, 0)),
                   pl.BlockSpec((NSA_KV, CMP_QT, TQ, n_slc), lambda i: (0, i, 0, 0)),
                   pl.BlockSpec((NSA_KV, CMP_QT, 1, n_slc), lambda i: (0, i, 0, 0))],
        out_shape=[jax.ShapeDtypeStruct((s, NSA_W), F32),
                   jax.ShapeDtypeStruct((NSA_KV, nq, TQ, n_slc), BF16),
                   jax.ShapeDtypeStruct((NSA_KV, nq, 1, n_slc), F32)],
        scratch_shapes=[pltpu.VMEM((CMP_QT * NSA_HEADS * TQ, HEAD_DIM), BF16)],
        compiler_params=_cparams(("parallel",), 48 << 20),
        name="nsa_cmp_select",
    )(slopes, proj, k_cmp, v_cmp, msel)


def _nsa_main_kernel(tiles_ref, count_ref, slopes_ref, q_ref, ks_ref, vs_ref, kw_ref, vw_ref, sel_ref, oc_ref,
                     gate_ref, o_ref, qa_ref, acc_s, m_s):
    g = pl.program_id(0)
    nq = pl.num_programs(1) * NSA_QT
    n_kt = nq * TQ // NSA_TK
    n_slc = sel_ref.shape[3]
    slopes = [slopes_ref[g * NSA_HPG + h] for h in range(NSA_HPG)]
    tiles_of_step = [pl.program_id(1) * NSA_QT + sub for sub in range(NSA_QT)]

    for sub in range(NSA_QT):
        not_picked = jnp.where(sel_ref[0, sub].astype(F32) > 0.5, 0.0, NEG_INF).astype(BF16)
        for h in range(NSA_HPG):
            qa_ref[sub, h * TQ:(h + 1) * TQ, :HEAD_DIM] = q_ref[sub * TQ:(sub + 1) * TQ, h * HEAD_DIM:(h + 1) * HEAD_DIM]
            qa_ref[sub, h * TQ:(h + 1) * TQ, HEAD_DIM:] = not_picked

    def scores(sub, k_ref, v_ref, pieces, bias, select):
        i = tiles_of_step[sub]
        zs = []
        for r0, n in pieces:
            k = k_ref[pl.ds(r0, n), :]
            if select:
                blk_of_key = lax.shift_right_logical(r0 + lax.broadcasted_iota(jnp.int32, (n, n_slc), 0), LOG2_L_SLC)
                one_hot = jnp.where(lax.broadcasted_iota(jnp.int32, (n, n_slc), 1) == blk_of_key, 1.0, 0.0)
                zs.append(_by_row_halves(_dot_nt, qa_ref[sub], jnp.concatenate([k, one_hot.astype(BF16)], axis=1)))
            else:
                zs.append(_by_row_halves(_dot_nt, qa_ref[sub, :, :HEAD_DIM], k))
        z = jnp.concatenate(zs, axis=1)
        v_aug = jnp.concatenate(
            [jnp.concatenate([v_ref[pl.ds(r0, n), :], jnp.ones((n, HEAD_DIM), BF16)], axis=1) for r0, n in pieces],
            axis=0)
        key_rel = jnp.concatenate(
            [(r0 - i * TQ + lax.broadcasted_iota(jnp.int32, (1, n), 1)).astype(F32) for r0, n in pieces], axis=1)
        if bias is None:
            return [z[h * TQ:(h + 1) * TQ] + slopes[h] * key_rel for h in range(NSA_HPG)], v_aug
        return [z[h * TQ:(h + 1) * TQ] + (bias + slopes[h] * key_rel) for h in range(NSA_HPG)], v_aug

    def attend(sub, pieces, bias, first):
        s_all, v_aug = scores(sub, ks_ref, vs_ref, pieces, bias, True)
        width = s_all[0].shape[1]
        ps, alphas = [], []
        for h, s in enumerate(s_all):
            rows = slice(h * TQ, (h + 1) * TQ)
            m_tile = jnp.max(s, axis=-1, keepdims=True)
            if first:
                m_new = jnp.broadcast_to(m_tile, (TQ, LANES))
            else:
                m_old = m_s[sub, rows]
                m_new = jnp.maximum(m_old, m_tile)
                alphas.append(jnp.exp(m_old - m_new))
            m_s[sub, rows] = m_new
            ps.append(jnp.exp(s - jnp.concatenate([m_new] * (width // LANES), axis=1)).astype(BF16))
        pv = _by_row_halves(_dot, jnp.concatenate(ps, axis=0), v_aug)
        for h in range(NSA_HPG):
            rows = slice(h * TQ, (h + 1) * TQ)
            if first:
                acc_s[sub, rows] = pv[rows]
            else:
                acc_s[sub, rows] = acc_s[sub, rows] * jnp.concatenate([alphas[h], alphas[h]], axis=1) + pv[rows]

    def tile_rows(kt):
        return pl.multiple_of(kt * NSA_TK, NSA_TK), NSA_TK

    for sub, i in enumerate(tiles_of_step):
        near = jnp.maximum(lax.div(i * TQ, NSA_TK) - 1, 0)
        near0 = pl.multiple_of(near * NSA_TK, NSA_TK)
        nw = 2 * NSA_TK
        ahead = ((near0 + lax.broadcasted_iota(jnp.int32, (TQ, nw), 1))
                 - (i * TQ + lax.broadcasted_iota(jnp.int32, (TQ, nw), 0)))
        attend(sub, [(near0, nw)], jnp.where(ahead <= 0, 0.0, NEG_INF), True)

    for sub, i in enumerate(tiles_of_step):
        step = g * nq + i
        n_far = count_ref[step]

        def pair_body(p, _, sub=sub, step=step):
            ta = tiles_ref[step * n_kt + 2 * p]
            tb = tiles_ref[step * n_kt + 2 * p + 1]
            attend(sub, [tile_rows(ta), tile_rows(tb)], None, False)
            return 0

        lax.fori_loop(0, lax.shift_right_logical(n_far, 1), pair_body, 0)

        @pl.when(lax.rem(n_far, 2) == 1)
        def _(sub=sub, step=step, n_far=n_far):
            attend(sub, [tile_rows(tiles_ref[step * n_kt + n_far - 1])], None, False)

    wn = WINDOW + TQ
    pv_win = []
    for sub, i in enumerate(tiles_of_step):
        w0 = pl.multiple_of(jnp.maximum(i * TQ - WINDOW, 0), TQ)
        dist = (i * TQ + lax.broadcasted_iota(jnp.int32, (TQ, wn), 0)
                - (w0 + lax.broadcasted_iota(jnp.int32, (TQ, wn), 1)))
        w_bias = jnp.where(dist >= 0, jnp.where(dist < WINDOW, 0.0, NEG_INF), NEG_INF)
        s_win, v_win = scores(sub, kw_ref, vw_ref, [(w0, wn)], w_bias, False)
        p_win = [jnp.exp(s - jnp.max(s, axis=-1, keepdims=True)).astype(BF16) for s in s_win]
        pv_win.append(_by_row_halves(_dot, jnp.concatenate(p_win, axis=0), v_win))

    gates = gate_ref[...]
    for sub in range(NSA_QT):
        tile = slice(sub * TQ, (sub + 1) * TQ)
        for h in range(NSA_HPG):
            cs = slice(h * HEAD_DIM, (h + 1) * HEAD_DIM)
            a_s = acc_s[sub, h * TQ:(h + 1) * TQ]
            a_w = pv_win[sub][h * TQ:(h + 1) * TQ]
            o_sel = a_s[:, :HEAD_DIM] / a_s[:, HEAD_DIM:HEAD_DIM + 1]
            o_win = a_w[:, :HEAD_DIM] / a_w[:, HEAD_DIM:HEAD_DIM + 1]
            o_ref[tile, cs] = (gates[tile, 3 * h:3 * h + 1] * oc_ref[tile, cs]
                               + gates[tile, 3 * h + 1:3 * h + 2] * o_sel
                               + gates[tile, 3 * h + 2:3 * h + 3] * o_win).astype(o_ref.dtype)


def _far_tile_lists(tile_any, s):
    nq = s // TQ
    n_kt = s // NSA_TK
    hit = tile_any.reshape(NSA_KV, nq, n_kt, NSA_TK // L_SLC).max(axis=-1) > 0.5
    kt = jnp.arange(n_kt, dtype=jnp.int32)
    diag = (jnp.arange(nq, dtype=jnp.int32) * TQ) // NSA_TK
    far = jnp.logical_and(hit, kt[None, None, :] < diag[None, :, None] - 1)
    tiles = jnp.sort(jnp.where(far, kt[None, None, :], n_kt), axis=-1)
    tiles = jnp.minimum(tiles, n_kt - 1)
    return tiles.reshape(-1), jnp.sum(far, axis=-1, dtype=jnp.int32).reshape(-1)


def _nsa_main(proj, sel, tile_any, o_c, gates, slopes):
    s = proj.shape[0]
    nq = s // TQ
    tq = NSA_QT * TQ
    n_slc = s // L_SLC
    qblk = (3 * SB_W) // GROUP_W
    tiles, counts = _far_tile_lists(tile_any, s)
    kv_spec = lambda cb: pl.BlockSpec((s, HEAD_DIM), lambda g, i, t, c: (0, cb + g))
    rows = NSA_HPG * TQ
    grid_spec = pltpu.PrefetchScalarGridSpec(
        num_scalar_prefetch=2,
        grid=(NSA_KV, nq // NSA_QT),
        in_specs=[pl.BlockSpec(memory_space=pltpu.SMEM),
                  pl.BlockSpec((tq, GROUP_W), lambda g, i, t, c: (i, qblk + g)),
                  kv_spec(CB_KS), kv_spec(CB_VS), kv_spec(CB_KW), kv_spec(CB_VW),
                  pl.BlockSpec((1, NSA_QT, TQ, n_slc), lambda g, i, t, c: (g, i, 0, 0)),
                  pl.BlockSpec((tq, GROUP_W), lambda g, i, t, c: (i, g)),
                  pl.BlockSpec((tq, LANES), lambda g, i, t, c: (i, g))],
        out_specs=pl.BlockSpec((tq, GROUP_W), lambda g, i, t, c: (i, g)),
        scratch_shapes=[pltpu.VMEM((NSA_QT, rows, HEAD_DIM + n_slc), BF16),
                        pltpu.VMEM((NSA_QT, rows, 2 * HEAD_DIM), F32), pltpu.VMEM((NSA_QT, rows, LANES), F32)],
    )
    vmem = 2 * 4 * s * HEAD_DIM * 2 + (28 << 20)
    return pl.pallas_call(
        _nsa_main_kernel,
        grid_spec=grid_spec,
        out_shape=jax.ShapeDtypeStruct((s, NSA_W), BF16),
        compiler_params=_cparams(("parallel", "arbitrary"), vmem),
        name="nsa_select_window",
    )(tiles, counts, slopes, proj, proj, proj, proj, proj, sel, o_c, gates)


def kernel(x, attn_norm, w_in, pos_cmp_k, pos_cmp_v, w_cmp_k1, w_cmp_k2, w_cmp_v1, w_cmp_v2, norm_sb, norm_nsa,
           w_out, ffn_norm, w_gate, w_up, w_down, final_norm):
    batch, s, d_model = x.shape
    assert batch == 1 and s % 1024 == 0 and w_in.shape[2] == MAIN_COLS + GATE_COLS
    depth = w_in.shape[0]
    d_ff = w_gate.shape[2]
    assert d_ff % (2 * LANES) == 0 and (d_ff // 2) % LANES == 0
    scale = HEAD_DIM ** -0.5
    head_idx = jnp.arange(1, NSA_HEADS + 1, dtype=F32)
    slopes = 2.0 ** (-8.0 * head_idx / NSA_HEADS)
    col_scale = jnp.concatenate([jnp.full((SB_W,), scale, F32), jnp.ones((2 * SB_W,), F32),
                                 jnp.full((NSA_W,), scale, F32), jnp.ones((MAIN_COLS - 3 * SB_W - NSA_W,), F32)])
    col_scale = col_scale.reshape(1, MAIN_COLS)
    xs = x[0]
    w_in_t = jnp.swapaxes(w_in, 1, 2)
    for l in range(depth):
        w_g = w_in[l][:, MAIN_COLS:].reshape(d_model, NSA_KV, 3 * NSA_HPG)
        w_g = jnp.pad(w_g, ((0, 0), (0, 0), (0, LANES - 3 * NSA_HPG))).reshape(d_model, NSA_KV * LANES)

        h, gates = _rmsnorm_gates(xs, attn_norm[l], w_g)
        proj, w_gate_bf16, w_up_bf16 = _in_proj(h, w_in_t, l, MAIN_COLS, col_scale, w_gate, w_up, ffn_norm[l])

        o_sb, w_out_bf16 = _sb_attention(proj, w_out, l, jnp.concatenate([norm_sb[l], norm_nsa[l]]))

        def blocks_view(cb):
            cols = proj[:, cb * HEAD_DIM:(cb + NSA_KV) * HEAD_DIM]
            r = cols.reshape(s // STRIDE_CMP, STRIDE_CMP, NSA_KV, HEAD_DIM).transpose(2, 0, 1, 3)
            return r.reshape(NSA_KV, s // STRIDE_CMP, STRIDE_CMP * HEAD_DIM)

        k_cmp = _compress(blocks_view(CB_KC), pos_cmp_k[l], w_cmp_k1[l], w_cmp_k2[l])
        v_cmp = _compress(blocks_view(CB_VC), pos_cmp_v[l], w_cmp_v1[l], w_cmp_v2[l])
        o_c, sel, tile_any = _nsa_cmp(proj, k_cmp, v_cmp, slopes)
        o_nsa = _nsa_main(proj, sel, tile_any, o_c, gates, slopes)

        xs, xs_bf16 = _out_proj(o_sb, o_nsa, w_out_bf16, xs)
        act, w_down_bf16 = _ffn_up(xs_bf16, w_gate_bf16, w_up_bf16, w_down, l)
        xs = _ffn_down(act, w_down_bf16, xs)
    return _rmsnorm(xs, final_norm, F32)[None]
```

```python
import functools

import numpy as np
import jax
import jax.numpy as jnp
from jax import lax
from jax.experimental import pallas as pl
from jax.experimental.pallas import tpu as pltpu

HEAD_DIM = 128
SB_HEADS = 16
NSA_HEADS = 16
NSA_KV = 2
NSA_HPG = NSA_HEADS // NSA_KV
L_CMP = 32
STRIDE_CMP = 16
CMP_HIDDEN = 256
L_SLC = 64
LOG2_L_SLC = 6
N_TOPK = 16
WINDOW = 512
EPS = 1e-6
NEG_INF = -1e30
FORCE_BONUS = 1e6

LANES = 128
SUBLANES = 8
TQ = 128
TK = 128
SB_WIN = 3
SB_QT = 8
NSA_TK = 256
NSA_QT = 2
VMEM_CAP = 56 * 1024 * 1024
SB_DEAD_LOG = -105.0

F32 = jnp.float32
BF16 = jnp.bfloat16

SB_W = SB_HEADS * HEAD_DIM
NSA_W = NSA_HEADS * HEAD_DIM
GROUP_W = NSA_HPG * HEAD_DIM
MAIN_COLS = 3 * SB_W + NSA_W + 3 * 2 * NSA_KV * HEAD_DIM
GATE_COLS = 3 * NSA_HEADS
CB_KC, CB_VC, CB_KS, CB_VS, CB_KW, CB_VW = 64, 66, 68, 70, 72, 74


def _cparams(sem, vmem_bytes):
    return pltpu.CompilerParams(dimension_semantics=sem, vmem_limit_bytes=int(min(vmem_bytes, VMEM_CAP)))


def _dot(a, b):
    return jnp.dot(a, b, preferred_element_type=F32)


def _dot_nt(a, b):
    return lax.dot_general(a, b, (((1,), (1,)), ((), ())), preferred_element_type=F32)


def _by_row_halves(dot_fn, a, b):
    half = a.shape[0] // 2
    return jnp.concatenate([dot_fn(a[:half], b), dot_fn(a[half:], b)], axis=0)


def _split_bf16(x):
    hi = x.astype(BF16)
    lo = (x - hi.astype(F32)).astype(BF16)
    return hi, lo


def _rms_kernel(x_ref, g_ref, o_ref):
    x = x_ref[...]
    ms = jnp.mean(x * x, axis=-1, keepdims=True)
    o_ref[...] = (x * lax.rsqrt(ms + EPS) * g_ref[...]).astype(o_ref.dtype)


def _rmsnorm(x, g, out_dtype, tm=512):
    m, d = x.shape
    return pl.pallas_call(
        _rms_kernel,
        grid=(m // tm,),
        in_specs=[pl.BlockSpec((tm, d), lambda i: (i, 0)), pl.BlockSpec((1, d), lambda i: (0, 0))],
        out_specs=pl.BlockSpec((tm, d), lambda i: (i, 0)),
        out_shape=jax.ShapeDtypeStruct((m, d), out_dtype),
        compiler_params=_cparams(("parallel",), 6 * tm * d * 4),
        name="rmsnorm",
    )(x, g.reshape(1, d))


def _rms_gate_kernel(x_ref, g_ref, wg_ref, o_ref, gate_ref):
    x = x_ref[...]
    ms = jnp.mean(x * x, axis=-1, keepdims=True)
    h = (x * lax.rsqrt(ms + EPS) * g_ref[...]).astype(o_ref.dtype)
    o_ref[...] = h
    gate_ref[...] = jax.nn.sigmoid(_dot(h, wg_ref[...].astype(BF16)))


def _rmsnorm_gates(x, g, w_gates, tm=512):
    m, d = x.shape
    n = w_gates.shape[1]
    return pl.pallas_call(
        _rms_gate_kernel,
        grid=(m // tm,),
        in_specs=[pl.BlockSpec((tm, d), lambda i: (i, 0)), pl.BlockSpec((1, d), lambda i: (0, 0)),
                  pl.BlockSpec((d, n), lambda i: (0, 0))],
        out_specs=[pl.BlockSpec((tm, d), lambda i: (i, 0)), pl.BlockSpec((tm, n), lambda i: (i, 0))],
        out_shape=[jax.ShapeDtypeStruct((m, d), BF16), jax.ShapeDtypeStruct((m, n), F32)],
        compiler_params=_cparams(("parallel",), 6 * tm * d * 4 + 3 * d * n * 4),
        name="rmsnorm_gates",
    )(x, g.reshape(1, d), w_gates)


def _row_scale(a_ref):
    a = a_ref[...].astype(F32)
    ms = jnp.mean(a * a, axis=-1, keepdims=True)
    return jnp.broadcast_to(lax.rsqrt(ms + EPS), (a.shape[0], LANES))


def _lanes(x, n):
    return jnp.concatenate([x] * (n // LANES), axis=1)


def _gained_slab(w_ref, g_ref, o_ref):
    o_ref[...] = (w_ref[...] * _lanes(g_ref[...], w_ref.shape[1])).astype(o_ref.dtype)


def _slab_specs(w, layer, n_steps, step_of):
    rows, cols = w.shape[1], w.shape[2]
    slab = rows // n_steps
    assert slab * n_steps == rows and slab % (2 * SUBLANES) == 0
    return (pl.BlockSpec((None, slab, cols), lambda *idx: (layer, step_of(*idx), 0)),
            pl.BlockSpec((slab, LANES), lambda *idx: (step_of(*idx), 0)),
            pl.BlockSpec((slab, cols), lambda *idx: (step_of(*idx), 0)),
            jax.ShapeDtypeStruct((rows, cols), BF16), slab * cols * 6 + slab * LANES * 4)


def _row_gains(g):
    return jnp.broadcast_to(g.reshape(-1, 1), (g.shape[0], LANES))


def _out_proj_kernel(a1_ref, a2_ref, w_ref, res_ref, o_ref, ob_ref, r_ref):
    @pl.when(pl.program_id(1) == 0)
    def _():
        r_ref[0] = _row_scale(a1_ref)
        r_ref[1] = _row_scale(a2_ref)

    k1 = a1_ref.shape[1]
    tn = o_ref.shape[1]
    y1 = _dot(a1_ref[...], w_ref[:k1, :])
    y2 = _dot(a2_ref[...], w_ref[k1:, :])
    out = res_ref[...] + (_lanes(r_ref[0], tn) * y1 + _lanes(r_ref[1], tn) * y2)
    o_ref[...] = out
    ob_ref[...] = out.astype(ob_ref.dtype)


def _out_proj(a1, a2, w, res, tm=1024, tn=512):
    m, k1 = a1.shape
    k2 = a2.shape[1]
    n = w.shape[1]
    vmem = 2 * tm * (k1 + k2) * 2 + 2 * (k1 + k2) * tn * 2 + 2 * tm * tn * (4 + 4 + 2) + 3 * tm * tn * 4 + (4 << 20)
    return pl.pallas_call(
        _out_proj_kernel,
        grid=(m // tm, n // tn),
        in_specs=[pl.BlockSpec((tm, k1), lambda i, j: (i, 0)), pl.BlockSpec((tm, k2), lambda i, j: (i, 0)),
                  pl.BlockSpec((k1 + k2, tn), lambda i, j: (0, j)),
                  pl.BlockSpec((tm, tn), lambda i, j: (i, j))],
        out_specs=[pl.BlockSpec((tm, tn), lambda i, j: (i, j)), pl.BlockSpec((tm, tn), lambda i, j: (i, j))],
        out_shape=[jax.ShapeDtypeStruct((m, n), F32), jax.ShapeDtypeStruct((m, n), BF16)],
        scratch_shapes=[pltpu.VMEM((2, tm, LANES), F32)],
        compiler_params=_cparams(("parallel", "arbitrary"), vmem),
        name="out_proj",
    )(a1, a2, w, res)


def _ffn_up_kernel(a_ref, wg_ref, wu_ref, wd_ref, o_ref, wd_bf16_ref, r_ref):
    @pl.when(pl.program_id(1) == 0)
    def _():
        r_ref[...] = _row_scale(a_ref)

    a = a_ref[...]
    r = _lanes(r_ref[...], o_ref.shape[1])
    gate = r * _dot(a, wg_ref[...])
    up = r * _dot(a, wu_ref[...])
    o_ref[...] = (jax.nn.silu(gate) * up).astype(o_ref.dtype)
    wd_bf16_ref[...] = wd_ref[...].astype(wd_bf16_ref.dtype)


def _ffn_up(a, w_gate, w_up, w_down, layer, tm=2048, tn=2 * LANES):
    m, k = a.shape
    n = w_gate.shape[1]
    d_out = w_down.shape[2]
    nj = n // tn
    slab = n // ((m // tm) * nj)
    assert slab * (m // tm) * nj == n and slab % (2 * SUBLANES) == 0
    vmem = 2 * tm * k * 2 + 2 * 2 * k * tn * 2 + 2 * tm * tn * 2 + 5 * tm * tn * 4 + 2 * slab * d_out * 6 + (4 << 20)
    w_spec = pl.BlockSpec((k, tn), lambda i, j: (0, j))
    return pl.pallas_call(
        _ffn_up_kernel,
        grid=(m // tm, nj),
        in_specs=[pl.BlockSpec((tm, k), lambda i, j: (i, 0)), w_spec, w_spec,
                  pl.BlockSpec((None, slab, d_out), lambda i, j: (layer, i * nj + j, 0))],
        out_specs=[pl.BlockSpec((tm, tn), lambda i, j: (i, j)),
                   pl.BlockSpec((slab, d_out), lambda i, j: (i * nj + j, 0))],
        out_shape=[jax.ShapeDtypeStruct((m, n), BF16), jax.ShapeDtypeStruct((n, d_out), BF16)],
        scratch_shapes=[pltpu.VMEM((tm, LANES), F32)],
        compiler_params=_cparams(("parallel", "arbitrary"), vmem),
        name="ffn_gate_up",
    )(a, w_gate, w_up, w_down)


IN_PROJ_SLABS = 16


def _in_proj_kernel(a_ref, bt_ref, cs_ref, wa_ref, wb_ref, wgain_ref, o_ref, wa_bf16_ref, wb_bf16_ref):
    o_ref[...] = (_dot_nt(a_ref[...], bt_ref[...].astype(BF16)) * cs_ref[...]).astype(o_ref.dtype)

    @pl.when(pl.program_id(1) < IN_PROJ_SLABS)
    def _():
        _gained_slab(wa_ref, wgain_ref, wa_bf16_ref)
        _gained_slab(wb_ref, wgain_ref, wb_bf16_ref)


def _in_proj(a, w_t, layer, n, col_scale, w_a, w_b, next_gains, tm=1024, tn=512):
    m, k = a.shape
    ni, nj = m // tm, n // tn
    assert nj >= IN_PROJ_SLABS
    w_spec, g_spec, wo_spec, wo_shape, slab_bytes = _slab_specs(
        w_a, layer, ni * IN_PROJ_SLABS, lambda i, j: i * IN_PROJ_SLABS + jnp.minimum(j, IN_PROJ_SLABS - 1))
    vmem = 2 * tm * k * 2 + tn * k * (2 * 4 + 2) + 2 * tm * tn * 2 + 2 * tm * tn * 4 + 4 * slab_bytes + (4 << 20)
    return pl.pallas_call(
        _in_proj_kernel,
        grid=(ni, nj),
        in_specs=[pl.BlockSpec((tm, k), lambda i, j: (i, 0)),
                  pl.BlockSpec((None, tn, k), lambda i, j: (layer, j, 0)),
                  pl.BlockSpec((1, tn), lambda i, j: (0, j)),
                  w_spec, w_spec, g_spec],
        out_specs=[pl.BlockSpec((tm, tn), lambda i, j: (i, j)), wo_spec, wo_spec],
        out_shape=[jax.ShapeDtypeStruct((m, n), BF16), wo_shape, wo_shape],
        compiler_params=_cparams(("parallel", "arbitrary"), vmem),
        name="in_proj",
    )(a, w_t, col_scale, w_a, w_b, _row_gains(next_gains))


def _ffn_down_kernel(a_ref, b_ref, res_ref, o_ref, acc_ref):
    kk = pl.program_id(2)

    @pl.when(kk == 0)
    def _():
        acc_ref[...] = res_ref[...]

    acc_ref[...] += _dot(a_ref[...], b_ref[...])

    @pl.when(kk == pl.num_programs(2) - 1)
    def _():
        o_ref[...] = acc_ref[...]


def _ffn_down(a, b, res, tm=1024, tn=512):
    m, k = a.shape
    n = b.shape[1]
    tk = k // 2
    assert tk % LANES == 0
    vmem = 2 * tm * tk * 2 + 2 * tk * tn * 2 + 6 * tm * tn * 4 + (4 << 20)
    return pl.pallas_call(
        _ffn_down_kernel,
        grid=(m // tm, n // tn, k // tk),
        in_specs=[pl.BlockSpec((tm, tk), lambda i, j, kk: (i, kk)),
                  pl.BlockSpec((tk, tn), lambda i, j, kk: (kk, j)),
                  pl.BlockSpec((tm, tn), lambda i, j, kk: (i, j))],
        out_specs=pl.BlockSpec((tm, tn), lambda i, j, kk: (i, j)),
        out_shape=jax.ShapeDtypeStruct((m, n), F32),
        scratch_shapes=[pltpu.VMEM((tm, tn), F32)],
        compiler_params=_cparams(("parallel", "parallel", "arbitrary"), vmem),
        name="ffn_down",
    )(a, b, res)


def _sb_kernel(q_ref, k_ref, v_ref, u_ref, w_ref, wgain_ref, o_ref, w_bf16_ref, carry_ref, acc_ref, *, hg):
    _gained_slab(w_ref, wgain_ref, w_bf16_ref)
    step = pl.program_id(1)
    u = u_ref[...]
    heads = [slice(h * HEAD_DIM, (h + 1) * HEAD_DIM) for h in range(hg)]

    def logs(z):
        log1p_e = jnp.log(1.0 + jnp.exp(-jnp.abs(z)))
        log_sig = jnp.minimum(z, 0.0) - log1p_e
        return log_sig, log_sig - z

    def tile_sums(log_not, n):
        hi, lo = _split_bf16(log_not)
        return [jnp.concatenate([hi[:, t * TK:(t + 1) * TK], lo[:, t * TK:(t + 1) * TK]], axis=1) for t in range(n)]

    def window(sub):
        qt = step * SB_QT + sub
        rows = slice(sub * TQ, (sub + 1) * TQ)
        base = jnp.maximum(qt - (SB_WIN - 1), 0)
        row0 = pl.multiple_of(base * TK, TK)
        wk = SB_WIN * TK
        key_pos = base * TK + lax.broadcasted_iota(jnp.int32, (TQ, wk), 1)
        visible = key_pos < qt * TQ + lax.broadcasted_iota(jnp.int32, (TQ, wk), 0)
        zs = [jnp.where(visible, _dot_nt(q_ref[rows, cs], k_ref[pl.ds(row0, wk), cs]), NEG_INF) for cs in heads]
        lg = [logs(z) for z in zs]
        lhs = [part for (_, log_not) in lg for part in tile_sums(log_not, SB_WIN)]
        sums = _by_row_halves(_dot, jnp.concatenate(lhs, axis=0), u)
        worst = None
        for h, cs in enumerate(heads):
            carry = None
            shifted = [None] * SB_WIN
            for t in reversed(range(SB_WIN)):
                blk = sums[(h * SB_WIN + t) * TQ:(h * SB_WIN + t + 1) * TQ]
                tail, total = blk[:, :TK], blk[:, TK:]
                shifted[t] = tail if carry is None else tail + carry
                carry = total if carry is None else carry + total
            a = jnp.exp(lg[h][0] + jnp.concatenate(shifted, axis=1))
            acc_ref[sub * hg + h] = _dot(a.astype(BF16), v_ref[pl.ds(row0, wk), cs])
            carry_ref[sub * hg + h] = carry
            worst = carry if worst is None else jnp.maximum(worst, carry)
        return base, jnp.max(worst)

    def far_sweep(sub, base, worst):
        rows = slice(sub * TQ, (sub + 1) * TQ)

        def tile(kb):
            r0 = pl.multiple_of(kb * TK, TK)
            lg1 = [logs(_dot_nt(q_ref[rows, cs], k_ref[pl.ds(r0, TK), cs])) for cs in heads]
            sums1 = _dot(jnp.concatenate([tile_sums(log_not, 1)[0] for (_, log_not) in lg1], axis=0), u)
            far = None
            for h, cs in enumerate(heads):
                blk = sums1[h * TQ:(h + 1) * TQ]
                carry = carry_ref[sub * hg + h]
                a = jnp.exp(lg1[h][0] + blk[:, :TK] + carry)
                acc_ref[sub * hg + h] += _dot(a.astype(BF16), v_ref[pl.ds(r0, TK), cs])
                carry = carry + blk[:, TK:]
                carry_ref[sub * hg + h] = carry
                far = carry if far is None else jnp.maximum(far, carry)
            return jnp.max(far)

        def cond(c):
            kb, far = c
            return jnp.logical_and(kb >= 0, far > SB_DEAD_LOG)

        def body(c):
            kb, _ = c
            return kb - 1, tile(kb)

        lax.while_loop(cond, body, (base - 1, worst))

    starts = [window(sub) for sub in range(SB_QT)]
    for sub, (base, worst) in enumerate(starts):
        far_sweep(sub, base, worst)
    for sub in range(SB_QT):
        for h, cs in enumerate(heads):
            o_ref[sub * TQ:(sub + 1) * TQ, cs] = acc_ref[sub * hg + h].astype(o_ref.dtype)


def _sb_attention(proj, w_next, layer, next_gains, hg=4):
    s = proj.shape[0]
    w = hg * HEAD_DIM
    nblk = SB_W // w
    tri = np.arange(TK)[:, None] > np.arange(TK)[None, :]
    half = np.concatenate([tri, np.ones((TK, TK), bool)], axis=1)
    u = jnp.asarray(np.concatenate([half, half], axis=0), dtype=BF16)
    tq = SB_QT * TQ
    nq = s // tq
    w_spec, g_spec, wo_spec, wo_shape, slab_bytes = _slab_specs(w_next, layer, nblk * nq, lambda g, i: g * nq + i)
    vmem = (2 * 2 * s * w * 2 + 2 * tq * w * 2 + 2 * tq * w * 4 + 2 * SB_QT * hg * TQ * TK * 4 + 2 * slab_bytes
            + (16 << 20))
    return pl.pallas_call(
        functools.partial(_sb_kernel, hg=hg),
        grid=(nblk, nq),
        in_specs=[pl.BlockSpec((tq, w), lambda g, i: (i, g)),
                  pl.BlockSpec((s, w), lambda g, i: (0, nblk + g)),
                  pl.BlockSpec((s, w), lambda g, i: (0, 2 * nblk + g)),
                  pl.BlockSpec((2 * TK, 2 * TK), lambda g, i: (0, 0)),
                  w_spec, g_spec],
        out_specs=[pl.BlockSpec((tq, w), lambda g, i: (i, g)), wo_spec],
        out_shape=[jax.ShapeDtypeStruct((s, SB_W), BF16), wo_shape],
        scratch_shapes=[pltpu.VMEM((SB_QT * hg, TQ, TK), F32), pltpu.VMEM((SB_QT * hg, TQ, HEAD_DIM), F32)],
        compiler_params=_cparams(("parallel", "arbitrary"), vmem),
        name="sb_attention",
    )(proj, proj, proj, u, w_next, _row_gains(next_gains))


def _compress_kernel(r_ref, pos_ref, w1_ref, w2_ref, o_ref):
    r = r_ref[0]
    half = r.shape[1]
    first = _dot(r, w1_ref[:half, :])
    second = _dot(r, w1_ref[half:, :])
    second = pltpu.roll(second, second.shape[0] - 1, 0)
    pos_term = _dot(pos_ref[...].astype(BF16), w1_ref[...])[0:1, :]
    hidden = jax.nn.gelu(first + second + pos_term)
    o_ref[0] = _dot(hidden.astype(BF16), w2_ref[...]).astype(o_ref.dtype)


def _compress(r, pos, w1, w2):
    g, n, half = r.shape
    pos_rows = jnp.zeros((SUBLANES, 2 * half), F32).at[0].set(pos.reshape(-1))
    return pl.pallas_call(
        _compress_kernel,
        grid=(g,),
        in_specs=[pl.BlockSpec((1, n, half), lambda i: (i, 0, 0)),
                  pl.BlockSpec((SUBLANES, 2 * half), lambda i: (0, 0)),
                  pl.BlockSpec((2 * half, CMP_HIDDEN), lambda i: (0, 0)),
                  pl.BlockSpec((CMP_HIDDEN, HEAD_DIM), lambda i: (0, 0))],
        out_specs=pl.BlockSpec((1, n, HEAD_DIM), lambda i: (i, 0, 0)),
        out_shape=jax.ShapeDtypeStruct((g, n, HEAD_DIM), BF16),
        compiler_params=_cparams(("parallel",), 24 << 20),
        name="nsa_compress",
    )(r, pos_rows, w1.astype(BF16), w2.astype(BF16))


CMP_QT = 4


def _nsa_cmp_kernel(slopes_ref, q_ref, kc_ref, vc_ref, msel_ref, oc_ref, sel_ref, flag_ref, qs_ref, *, n_top):
    n_cmp = kc_ref.shape[1]
    n_slc = msel_ref.shape[1]
    msel = msel_ref[...]
    rows = NSA_HPG * TQ
    t_rel = lax.broadcasted_iota(jnp.int32, (TQ, n_cmp), 0)
    blk = lax.broadcasted_iota(jnp.int32, (n_slc, TQ), 0)
    blk_f = blk.astype(F32)
    for sub in range(CMP_QT):
        i = pl.program_id(0) * CMP_QT + sub
        tile = slice(sub * TQ, (sub + 1) * TQ)
        end_rel = lax.broadcasted_iota(jnp.int32, (TQ, n_cmp), 1) * STRIDE_CMP + (L_CMP - 1) - i * TQ
        bias = jnp.where(end_rel <= t_rel, 0.0, NEG_INF)
        end_row = end_rel[0:1, :].astype(F32)
        t_col = i * TQ + lax.broadcasted_iota(jnp.int32, (TQ, 1), 0)
        row_valid = jnp.where(t_col >= L_CMP - 1, 1.0, 0.0)
        cur = lax.shift_right_logical(i * TQ + lax.broadcasted_iota(jnp.int32, (n_slc, TQ), 1), LOG2_L_SLC)
        valid = blk <= cur
        bonus = jnp.where(blk == 0, FORCE_BONUS, 0.0)
        bonus = jnp.where(blk == cur, FORCE_BONUS, bonus)
        bonus = jnp.where(blk == cur - 1, FORCE_BONUS, bonus)
        for g in range(NSA_KV):
            base = (sub * NSA_KV + g) * rows
            for h in range(NSA_HPG):
                c0 = (g * NSA_HPG + h) * HEAD_DIM
                qs_ref[base + h * TQ:base + (h + 1) * TQ, :] = q_ref[tile, c0:c0 + HEAD_DIM]
            z = _by_row_halves(_dot_nt, qs_ref[base:base + rows, :], kc_ref[g])
            p_sum = jnp.zeros((TQ, n_cmp), F32)
            ps = []
            for h in range(NSA_HPG):
                s = z[h * TQ:(h + 1) * TQ] + (bias + slopes_ref[g * NSA_HPG + h] * end_row)
                m = jnp.max(s, axis=-1, keepdims=True)
                e = jnp.exp(s - m)
                p = e * (row_valid / jnp.sum(e, axis=-1, keepdims=True))
                ps.append(p.astype(BF16))
                p_sum = p_sum + p
            oc = _by_row_halves(_dot, jnp.concatenate(ps, axis=0), vc_ref[g])
            for h in range(NSA_HPG):
                c0 = (g * NSA_HPG + h) * HEAD_DIM
                oc_ref[tile, c0:c0 + HEAD_DIM] = oc[h * TQ:(h + 1) * TQ]
            hi, lo = _split_bf16(p_sum)
            imp = _dot(hi, msel) + _dot(lo, msel)
            score = jnp.where(valid, imp.T + bonus, NEG_INF)
            taken = jnp.zeros((n_slc, TQ), F32)
            for _ in range(n_top):
                best = jnp.max(score, axis=0, keepdims=True)
                first = jnp.min(jnp.where(score == best, blk_f, float(n_slc)), axis=0, keepdims=True)
                hit = blk_f == first
                taken = jnp.where(hit, 1.0, taken)
                score = jnp.where(hit, -jnp.inf, score)
            chosen_q = jnp.where(valid, taken, 0.0).T
            sel_ref[g, sub] = chosen_q.astype(sel_ref.dtype)
            flag_ref[g, sub] = jnp.max(chosen_q, axis=0, keepdims=True)


def _select_matrix(n_cmp_pad, n_slc):
    rs = L_SLC // STRIDE_CMP
    rc = L_CMP // STRIDE_CMP
    m = np.zeros((n_cmp_pad, n_slc), np.float32)
    for j in range(n_slc):
        for a in range(rs):
            for b in range(rc):
                src = rs * j - a - b
                if 0 <= src < n_cmp_pad - 1:
                    m[src, j] += 1.0
    return jnp.asarray(m, dtype=BF16)


def _nsa_cmp(proj, k_cmp, v_cmp, slopes):
    s = proj.shape[0]
    n_cmp = k_cmp.shape[1]
    n_slc = s // L_SLC
    nq = s // TQ
    msel = _select_matrix(n_cmp, n_slc)
    qblk = (3 * SB_W) // NSA_W
    return pl.pallas_call(
        functools.partial(_nsa_cmp_kernel, n_top=min(N_TOPK, n_slc)),
        grid=(nq // CMP_QT,),
        in_specs=[pl.BlockSpec(memory_space=pltpu.SMEM),
                  pl.BlockSpec((CMP_QT * TQ, NSA_W), lambda i: (i, qblk)),
                  pl.BlockSpec((NSA_KV, n_cmp, HEAD_DIM), lambda i: (0, 0, 0)),
                  pl.BlockSpec((NSA_KV, n_cmp, HEAD_DIM), lambda i: (0, 0, 0)),
                  pl.BlockSpec((n_cmp, n_slc), lambda i: (0, 0))],
        out_specs=[pl.BlockSpec((CMP_QT * TQ, NSA_W), lambda i: (i, 0)),
                   pl.BlockSpec((NSA_KV, CMP_QT, TQ, n_slc), lambda i: (0, i, 0, 0)),
                   pl.BlockSpec((NSA_KV, CMP_QT, 1, n_slc), lambda i: (0, i, 0, 0))],
        out_shape=[jax.ShapeDtypeStruct((s, NSA_W), F32),
                   jax.ShapeDtypeStruct((NSA_KV, nq, TQ, n_slc), BF16),
                   jax.ShapeDtypeStruct((NSA_KV, nq, 1, n_slc), F32)],
        scratch_shapes=[pltpu.VMEM((CMP_QT * NSA_HEADS * TQ, HEAD_DIM), BF16)],
        compiler_params=_cparams(("parallel",), 48 << 20),
        name="nsa_cmp_select",
    )(slopes, proj, k_cmp, v_cmp, msel)


def _nsa_main_kernel(tiles_ref, count_ref, slopes_ref, q_ref, ks_ref, vs_ref, kw_ref, vw_ref, sel_ref, oc_ref,
                     gate_ref, o_ref, qa_ref, acc_s, m_s):
    g = pl.program_id(0)
    nq = pl.num_programs(1) * NSA_QT
    n_kt = nq * TQ // NSA_TK
    n_slc = sel_ref.shape[3]
    slopes = [slopes_ref[g * NSA_HPG + h] for h in range(NSA_HPG)]
    tiles_of_step = [pl.program_id(1) * NSA_QT + sub for sub in range(NSA_QT)]

    for sub in range(NSA_QT):
        not_picked = jnp.where(sel_ref[0, sub].astype(F32) > 0.5, 0.0, NEG_INF).astype(BF16)
        for h in range(NSA_HPG):
            qa_ref[sub, h * TQ:(h + 1) * TQ, :HEAD_DIM] = q_ref[sub * TQ:(sub + 1) * TQ, h * HEAD_DIM:(h + 1) * HEAD_DIM]
            qa_ref[sub, h * TQ:(h + 1) * TQ, HEAD_DIM:] = not_picked

    def scores(sub, k_ref, v_ref, pieces, bias, select):
        i = tiles_of_step[sub]
        zs = []
        for r0, n in pieces:
            k = k_ref[pl.ds(r0, n), :]
            if select:
                blk_of_key = lax.shift_right_logical(r0 + lax.broadcasted_iota(jnp.int32, (n, n_slc), 0), LOG2_L_SLC)
                one_hot = jnp.where(lax.broadcasted_iota(jnp.int32, (n, n_slc), 1) == blk_of_key, 1.0, 0.0)
                zs.append(_by_row_halves(_dot_nt, qa_ref[sub], jnp.concatenate([k, one_hot.astype(BF16)], axis=1)))
            else:
                zs.append(_by_row_halves(_dot_nt, qa_ref[sub, :, :HEAD_DIM], k))
        z = jnp.concatenate(zs, axis=1)
        v_aug = jnp.concatenate(
            [jnp.concatenate([v_ref[pl.ds(r0, n), :], jnp.ones((n, HEAD_DIM), BF16)], axis=1) for r0, n in pieces],
            axis=0)
        key_rel = jnp.concatenate(
            [(r0 - i * TQ + lax.broadcasted_iota(jnp.int32, (1, n), 1)).astype(F32) for r0, n in pieces], axis=1)
        if bias is None:
            return [z[h * TQ:(h + 1) * TQ] + slopes[h] * key_rel for h in range(NSA_HPG)], v_aug
        return [z[h * TQ:(h + 1) * TQ] + (bias + slopes[h] * key_rel) for h in range(NSA_HPG)], v_aug

    def attend(sub, pieces, bias, first):
        s_all, v_aug = scores(sub, ks_ref, vs_ref, pieces, bias, True)
        width = s_all[0].shape[1]
        ps, alphas = [], []
        for h, s in enumerate(s_all):
            rows = slice(h * TQ, (h + 1) * TQ)
            m_tile = jnp.max(s, axis=-1, keepdims=True)
            if first:
                m_new = jnp.broadcast_to(m_tile, (TQ, LANES))
            else:
                m_old = m_s[sub, rows]
                m_new = jnp.maximum(m_old, m_tile)
                alphas.append(jnp.exp(m_old - m_new))
            m_s[sub, rows] = m_new
            ps.append(jnp.exp(s - jnp.concatenate([m_new] * (width // LANES), axis=1)).astype(BF16))
        pv = _by_row_halves(_dot, jnp.concatenate(ps, axis=0), v_aug)
        for h in range(NSA_HPG):
            rows = slice(h * TQ, (h + 1) * TQ)
            if first:
                acc_s[sub, rows] = pv[rows]
            else:
                acc_s[sub, rows] = acc_s[sub, rows] * jnp.concatenate([alphas[h], alphas[h]], axis=1) + pv[rows]

    def tile_rows(kt):
        return pl.multiple_of(kt * NSA_TK, NSA_TK), NSA_TK

    for sub, i in enumerate(tiles_of_step):
        near = jnp.maximum(lax.div(i * TQ, NSA_TK) - 1, 0)
        near0 = pl.multiple_of(near * NSA_TK, NSA_TK)
        nw = 2 * NSA_TK
        ahead = ((near0 + lax.broadcasted_iota(jnp.int32, (TQ, nw), 1))
                 - (i * TQ + lax.broadcasted_iota(jnp.int32, (TQ, nw), 0)))
        attend(sub, [(near0, nw)], jnp.where(ahead <= 0, 0.0, NEG_INF), True)

    for sub, i in enumerate(tiles_of_step):
        step = g * nq + i
        n_far = count_ref[step]

        def pair_body(p, _, sub=sub, step=step):
            ta = tiles_ref[step * n_kt + 2 * p]
            tb = tiles_ref[step * n_kt + 2 * p + 1]
            attend(sub, [tile_rows(ta), tile_rows(tb)], None, False)
            return 0

        lax.fori_loop(0, lax.shift_right_logical(n_far, 1), pair_body, 0)

        @pl.when(lax.rem(n_far, 2) == 1)
        def _(sub=sub, step=step, n_far=n_far):
            attend(sub, [tile_rows(tiles_ref[step * n_kt + n_far - 1])], None, False)

    wn = WINDOW + TQ
    pv_win = []
    for sub, i in enumerate(tiles_of_step):
        w0 = pl.multiple_of(jnp.maximum(i * TQ - WINDOW, 0), TQ)
        dist = (i * TQ + lax.broadcasted_iota(jnp.int32, (TQ, wn), 0)
                - (w0 + lax.broadcasted_iota(jnp.int32, (TQ, wn), 1)))
        w_bias = jnp.where(dist >= 0, jnp.where(dist < WINDOW, 0.0, NEG_INF), NEG_INF)
        s_win, v_win = scores(sub, kw_ref, vw_ref, [(w0, wn)], w_bias, False)
        p_win = [jnp.exp(s - jnp.max(s, axis=-1, keepdims=True)).astype(BF16) for s in s_win]
        pv_win.append(_by_row_halves(_dot, jnp.concatenate(p_win, axis=0), v_win))

    gates = gate_ref[...]
    for sub in range(NSA_QT):
        tile = slice(sub * TQ, (sub + 1) * TQ)
        for h in range(NSA_HPG):
            cs = slice(h * HEAD_DIM, (h + 1) * HEAD_DIM)
            a_s = acc_s[sub, h * TQ:(h + 1) * TQ]
            a_w = pv_win[sub][h * TQ:(h + 1) * TQ]
            o_sel = a_s[:, :HEAD_DIM] / a_s[:, HEAD_DIM:HEAD_DIM + 1]
            o_win = a_w[:, :HEAD_DIM] / a_w[:, HEAD_DIM:HEAD_DIM + 1]
            o_ref[tile, cs] = (gates[tile, 3 * h:3 * h + 1] * oc_ref[tile, cs]
                               + gates[tile, 3 * h + 1:3 * h + 2] * o_sel
                               + gates[tile, 3 * h + 2:3 * h + 3] * o_win).astype(o_ref.dtype)


def _far_tile_lists(tile_any, s):
    nq = s // TQ
    n_kt = s // NSA_TK
    hit = tile_any.reshape(NSA_KV, nq, n_kt, NSA_TK // L_SLC).max(axis=-1) > 0.5
    kt = jnp.arange(n_kt, dtype=jnp.int32)
    diag = (jnp.arange(nq, dtype=jnp.int32) * TQ) // NSA_TK
    far = jnp.logical_and(hit, kt[None, None, :] < diag[None, :, None] - 1)
    tiles = jnp.sort(jnp.where(far, kt[None, None, :], n_kt), axis=-1)
    tiles = jnp.minimum(tiles, n_kt - 1)
    return tiles.reshape(-1), jnp.sum(far, axis=-1, dtype=jnp.int32).reshape(-1)


def _nsa_main(proj, sel, tile_any, o_c, gates, slopes):
    s = proj.shape[0]
    nq = s // TQ
    tq = NSA_QT * TQ
    n_slc = s // L_SLC
    qblk = (3 * SB_W) // GROUP_W
    tiles, counts = _far_tile_lists(tile_any, s)
    kv_spec = lambda cb: pl.BlockSpec((s, HEAD_DIM), lambda g, i, t, c: (0, cb + g))
    rows = NSA_HPG * TQ
    grid_spec = pltpu.PrefetchScalarGridSpec(
        num_scalar_prefetch=2,
        grid=(NSA_KV, nq // NSA_QT),
        in_specs=[pl.BlockSpec(memory_space=pltpu.SMEM),
                  pl.BlockSpec((tq, GROUP_W), lambda g, i, t, c: (i, qblk + g)),
                  kv_spec(CB_KS), kv_spec(CB_VS), kv_spec(CB_KW), kv_spec(CB_VW),
                  pl.BlockSpec((1, NSA_QT, TQ, n_slc), lambda g, i, t, c: (g, i, 0, 0)),
                  pl.BlockSpec((tq, GROUP_W), lambda g, i, t, c: (i, g)),
                  pl.BlockSpec((tq, LANES), lambda g, i, t, c: (i, g))],
        out_specs=pl.BlockSpec((tq, GROUP_W), lambda g, i, t, c: (i, g)),
        scratch_shapes=[pltpu.VMEM((NSA_QT, rows, HEAD_DIM + n_slc), BF16),
                        pltpu.VMEM((NSA_QT, rows, 2 * HEAD_DIM), F32), pltpu.VMEM((NSA_QT, rows, LANES), F32)],
    )
    vmem = 2 * 4 * s * HEAD_DIM * 2 + (28 << 20)
    return pl.pallas_call(
        _nsa_main_kernel,
        grid_spec=grid_spec,
        out_shape=jax.ShapeDtypeStruct((s, NSA_W), BF16),
        compiler_params=_cparams(("parallel", "arbitrary"), vmem),
        name="nsa_select_window",
    )(tiles, counts, slopes, proj, proj, proj, proj, proj, sel, o_c, gates)


def kernel(x, attn_norm, w_in, pos_cmp_k, pos_cmp_v, w_cmp_k1, w_cmp_k2, w_cmp_v1, w_cmp_v2, norm_sb, norm_nsa,
           w_out, ffn_norm, w_gate, w_up, w_down, final_norm):
    batch, s, d_model = x.shape
    assert batch == 1 and s % 1024 == 0 and w_in.shape[2] == MAIN_COLS + GATE_COLS
    depth = w_in.shape[0]
    d_ff = w_gate.shape[2]
    assert d_ff % (2 * LANES) == 0 and (d_ff // 2) % LANES == 0
    scale = HEAD_DIM ** -0.5
    head_idx = jnp.arange(1, NSA_HEADS + 1, dtype=F32)
    slopes = 2.0 ** (-8.0 * head_idx / NSA_HEADS)
    col_scale = jnp.concatenate([jnp.full((SB_W,), scale, F32), jnp.ones((2 * SB_W,), F32),
                                 jnp.full((NSA_W,), scale, F32), jnp.ones((MAIN_COLS - 3 * SB_W - NSA_W,), F32)])
    col_scale = col_scale.reshape(1, MAIN_COLS)
    xs = x[0]
    w_in_t = jnp.swapaxes(w_in, 1, 2)
    for l in range(depth):
        w_g = w_in[l][:, MAIN_COLS:].reshape(d_model, NSA_KV, 3 * NSA_HPG)
        w_g = jnp.pad(w_g, ((0, 0), (0, 0), (0, LANES - 3 * NSA_HPG))).reshape(d_model, NSA_KV * LANES)

        h, gates = _rmsnorm_gates(xs, attn_norm[l], w_g)
        proj, w_gate_bf16, w_up_bf16 = _in_proj(h, w_in_t, l, MAIN_COLS, col_scale, w_gate, w_up, ffn_norm[l])

        o_sb, w_out_bf16 = _sb_attention(proj, w_out, l, jnp.concatenate([norm_sb[l], norm_nsa[l]]))

        def blocks_view(cb):
            cols = proj[:, cb * HEAD_DIM:(cb + NSA_KV) * HEAD_DIM]
            r = cols.reshape(s // STRIDE_CMP, STRIDE_CMP, NSA_KV, HEAD_DIM).transpose(2, 0, 1, 3)
            return r.reshape(NSA_KV, s // STRIDE_CMP, STRIDE_CMP * HEAD_DIM)

        k_cmp = _compress(blocks_view(CB_KC), pos_cmp_k[l], w_cmp_k1[l], w_cmp_k2[l])
        v_cmp = _compress(blocks_view(CB_VC), pos_cmp_v[l], w_cmp_v1[l], w_cmp_v2[l])
        o_c, sel, tile_any = _nsa_cmp(proj, k_cmp, v_cmp, slopes)
        o_nsa = _nsa_main(proj, sel, tile_any, o_c, gates, slopes)

        xs, xs_bf16 = _out_proj(o_sb, o_nsa, w_out_bf16, xs)
        act, w_down_bf16 = _ffn_up(xs_bf16, w_gate_bf16, w_up_bf16, w_down, l)
        xs = _ffn_down(act, w_down_bf16, xs)
    return _rmsnorm(xs, final_norm, F32)[None]
```
